```python
import math
import jax, jax.numpy as jnp
from jax import lax
import numpy as np

D_MODEL = 2048
BATCH = 2
SEQ = 16384
DEPTH = 1

HEAD_DIM = 64
N_Q_HEADS = (3 * D_MODEL // 4) // HEAD_DIM
N_KV_HEADS = max(1, N_Q_HEADS // 8)
Q_PER_KV = N_Q_HEADS // N_KV_HEADS
ATTN_WIDTH = N_Q_HEADS * HEAD_DIM
KV_WIDTH = N_KV_HEADS * HEAD_DIM
WINDOW = 128
BLOCK = 128
SSM_GROUP = 16
SSM_WIDTH = D_MODEL - ATTN_WIDTH
N_SSM_GROUPS = SSM_WIDTH // SSM_GROUP
STATE = 64
SSM_CHUNK = 128
DT_MIN = 1e-3
DT_MAX = 1e-1
MIX_WIDTH = ATTN_WIDTH + SSM_WIDTH
IN_WIDTH = 2 * ATTN_WIDTH + 2 * KV_WIDTH + 2 * SSM_WIDTH
SPLITS = (ATTN_WIDTH,
          ATTN_WIDTH + KV_WIDTH,
          ATTN_WIDTH + 2 * KV_WIDTH,
          2 * ATTN_WIDTH + 2 * KV_WIDTH,
          2 * ATTN_WIDTH + 2 * KV_WIDTH + SSM_WIDTH)
EPS = 1e-5

kernel_name = 'hybrid_swa_sink_s5_parallel_heads'


def rms_norm(x, gain):
    xf = x.astype(jnp.float32)
    var = jnp.mean(xf * xf, axis=-1, keepdims=True)
    return (xf * lax.rsqrt(var + EPS) * gain.astype(jnp.float32)).astype(x.dtype)


def sliding_window_attention(q, k, v, sinks):
    b, s, _ = q.shape
    nb = s // BLOCK
    q = q.reshape(b, nb, BLOCK, N_KV_HEADS, Q_PER_KV, HEAD_DIM) * (HEAD_DIM ** -0.5)
    k = k.reshape(b, nb, BLOCK, N_KV_HEADS, HEAD_DIM)
    v = v.reshape(b, nb, BLOCK, N_KV_HEADS, HEAD_DIM)
    pad = ((0, 0), (1, 0), (0, 0), (0, 0), (0, 0))
    k_band = jnp.concatenate([jnp.pad(k[:, :-1], pad), k], axis=2)
    v_band = jnp.concatenate([jnp.pad(v[:, :-1], pad), v], axis=2)
    qi = jnp.arange(BLOCK)[:, None]
    kj = jnp.arange(2 * BLOCK)[None, :]
    rel = kj - BLOCK - qi
    band = (rel <= 0) & (rel > -WINDOW)
    sink_logits = sinks.astype(jnp.float32).reshape(N_KV_HEADS, Q_PER_KV)

    def block_fn(args):
        qb, kb, vb, blk = args
        mask = band & ((blk > 0) | (kj >= BLOCK))
        scores = jnp.einsum('bqhgd,bkhd->bhgqk', qb.astype(jnp.float32), kb.astype(jnp.float32))
        scores = jnp.where(mask, scores, -jnp.inf)
        sink = jnp.broadcast_to(sink_logits[None, :, :, None, None], scores.shape[:-1] + (1,))
        probs = jax.nn.softmax(jnp.concatenate([scores, sink], axis=-1), axis=-1)[..., :-1]
        return jnp.einsum('bhgqk,bkhd->bqhgd', probs.astype(vb.dtype), vb)

    out = lax.map(block_fn, (jnp.moveaxis(q, 1, 0), jnp.moveaxis(k_band, 1, 0),
                             jnp.moveaxis(v_band, 1, 0), jnp.arange(nb)))
    return jnp.moveaxis(out, 0, 1).reshape(b, s, ATTN_WIDTH)


def _complex_affine_combine(e1, e2):
    a1r, a1i, b1r, b1i = e1
    a2r, a2i, b2r, b2i = e2
    return (a2r * a1r - a2i * a1i,
            a2r * a1i + a2i * a1r,
            a2r * b1r - a2i * b1i + b2r,
            a2r * b1i + a2i * b1r + b2i)


def s5_ssm(u, lam_re, lam_im, log_step, b_re, b_im, c_re, c_im, d_skip):
    f32 = jnp.float32
    bsz, s, _ = u.shape
    uf = u.astype(f32).reshape(bsz, s, N_SSM_GROUPS, SSM_GROUP)
    lr, li = lam_re.astype(f32), lam_im.astype(f32)
    step = jnp.exp(log_step.astype(f32))[:, None]
    decay = jnp.exp(lr * step)
    abar_re, abar_im = decay * jnp.cos(li * step), decay * jnp.sin(li * step)
    den = lr * lr + li * li
    nr, ni = abar_re - 1.0, abar_im
    coef_re = (nr * lr + ni * li) / den
    coef_im = (ni * lr - nr * li) / den
    br, bi = b_re.astype(f32), b_im.astype(f32)
    bbar_re = coef_re[..., None] * br - coef_im[..., None] * bi
    bbar_im = coef_re[..., None] * bi + coef_im[..., None] * br
    cr, ci = c_re.astype(f32), c_im.astype(f32)
    nc = s // SSM_CHUNK
    u_chunks = jnp.moveaxis(uf.reshape(bsz, nc, SSM_CHUNK, N_SSM_GROUPS, SSM_GROUP), 1, 0)
    a_re = jnp.broadcast_to(abar_re, (bsz, SSM_CHUNK, N_SSM_GROUPS, STATE))
    a_im = jnp.broadcast_to(abar_im, (bsz, SSM_CHUNK, N_SSM_GROUPS, STATE))

    def chunk_fn(carry, uc):
        h0r, h0i = carry
        bu_re = jnp.einsum('btgc,gpc->btgp', uc, bbar_re)
        bu_im = jnp.einsum('btgc,gpc->btgp', uc, bbar_im)
        acc_r, acc_i, hr, hi = lax.associative_scan(
            _complex_affine_combine, (a_re, a_im, bu_re, bu_im), axis=1)
        hr, hi = (hr + acc_r * h0r[:, None] - acc_i * h0i[:, None],
                  hi + acc_r * h0i[:, None] + acc_i * h0r[:, None])
        y = jnp.einsum('btgp,gcp->btgc', hr, cr) - jnp.einsum('btgp,gcp->btgc', hi, ci)
        return (hr[:, -1], hi[:, -1]), y

    init = (jnp.zeros((bsz, N_SSM_GROUPS, STATE), f32), jnp.zeros((bsz, N_SSM_GROUPS, STATE), f32))
    _, y = lax.scan(chunk_fn, init, u_chunks)
    y = jnp.moveaxis(y, 0, 1).reshape(bsz, s, N_SSM_GROUPS, SSM_GROUP)
    y = y + d_skip.astype(f32).reshape(N_SSM_GROUPS, SSM_GROUP) * uf
    return y.reshape(bsz, s, SSM_WIDTH).astype(u.dtype)


def hybrid_layer(x, c, w_ada, b_ada, norm_gain, w_in, b_in, attn_sinks, attn_out_gain,
                 lam_re, lam_im, log_step, b_re, b_im, c_re, c_im, d_skip,
                 glu_w, glu_b, ssm_out_gain, w_out):
    mod = jax.nn.silu(c.astype(jnp.float32)) @ w_ada.astype(jnp.float32) + b_ada.astype(jnp.float32)
    shift, scale, gate = jnp.split(mod, 3, axis=-1)
    h = (rms_norm(x, norm_gain) * (1.0 + scale[:, None]) + shift[:, None]).astype(x.dtype)
    proj = h @ w_in + b_in
    q, k, v, z_attn, u_ssm, z_ssm = jnp.split(proj, SPLITS, axis=-1)
    attn = sliding_window_attention(q, k, v, attn_sinks)
    attn = rms_norm(attn, attn_out_gain) * jax.nn.silu(z_attn)
    ssm = jax.nn.gelu(s5_ssm(u_ssm, lam_re, lam_im, log_step, b_re, b_im, c_re, c_im, d_skip), approximate=False)
    ssm = ssm * jax.nn.sigmoid(ssm @ glu_w + glu_b)
    ssm = rms_norm(ssm, ssm_out_gain) * jax.nn.silu(z_ssm)
    mixed = jnp.concatenate([attn, ssm], axis=-1)
    out = mixed @ w_out
    return (x + gate[:, None].astype(x.dtype) * out).astype(x.dtype)


def setup_inputs(seed: int = 0) -> dict:
    key = jax.random.key(seed)
    ks = jax.random.split(key, 24)
    f32 = jnp.float32
    L, D, G, P = DEPTH, D_MODEL, N_SSM_GROUPS, STATE
    nrm = lambda k, shp, s: jax.random.normal(k, shp, f32) * s
    lam_im = jnp.broadcast_to(math.pi * jnp.arange(P, dtype=f32), (L, G, P))
    return {
        'x': nrm(ks[0], (BATCH, SEQ, D), 1.0),
        'c': nrm(ks[1], (BATCH, D), 1.0),
        'w_ada': nrm(ks[2], (L, D, 3 * D), 0.5 * D ** -0.5),
        'b_ada': nrm(ks[3], (L, 3 * D), 0.01),
        'norm_gain': 1.0 + nrm(ks[4], (L, D), 0.01),
        'w_in': nrm(ks[5], (L, D, IN_WIDTH), D ** -0.5),
        'b_in': nrm(ks[6], (L, IN_WIDTH), 0.01),
        'attn_sinks': nrm(ks[7], (L, N_Q_HEADS), 1.0),
        'attn_out_gain': 1.0 + nrm(ks[8], (L, ATTN_WIDTH), 0.01),
        'ssm_lambda_re': -0.5 + nrm(ks[9], (L, G, P), 0.01),
        'ssm_lambda_im': lam_im + 0.0 * nrm(ks[10], (L, G, P), 1.0) if False else lam_im + nrm(ks[10], (L, G, P), 0.01),
        'ssm_log_step': jax.random.uniform(ks[11], (L, G), f32, math.log(DT_MIN), math.log(DT_MAX)),
        'ssm_b_re': nrm(ks[12], (L, G, P, SSM_GROUP), (2 * SSM_GROUP) ** -0.5),
        'ssm_b_im': nrm(ks[13], (L, G, P, SSM_GROUP), (2 * SSM_GROUP) ** -0.5),
        'ssm_c_re': nrm(ks[14], (L, G, SSM_GROUP, P), P ** -0.5),
        'ssm_c_im': nrm(ks[15], (L, G, SSM_GROUP, P), P ** -0.5),
        'ssm_d': nrm(ks[16], (L, SSM_WIDTH), 1.0),
        'glu_w': nrm(ks[17], (L, SSM_WIDTH, SSM_WIDTH), SSM_WIDTH ** -0.5),
        'glu_b': nrm(ks[18], (L, SSM_WIDTH), 0.01),
        'ssm_out_gain': 1.0 + nrm(ks[19], (L, SSM_WIDTH), 0.01),
        'w_out': nrm(ks[20], (L, MIX_WIDTH, D), MIX_WIDTH ** -0.5),
        'final_gain': 1.0 + nrm(ks[21], (D,), 0.01),
    }


def reference(x, c, w_ada, b_ada, norm_gain, w_in, b_in, attn_sinks, attn_out_gain,
              ssm_lambda_re, ssm_lambda_im, ssm_log_step, ssm_b_re, ssm_b_im,
              ssm_c_re, ssm_c_im, ssm_d, glu_w, glu_b, ssm_out_gain, w_out, final_gain):
    h = x
    for l in range(DEPTH):
        h = hybrid_layer(h, c, w_ada[l], b_ada[l], norm_gain[l], w_in[l], b_in[l],
                         attn_sinks[l], attn_out_gain[l],
                         ssm_lambda_re[l], ssm_lambda_im[l], ssm_log_step[l],
                         ssm_b_re[l], ssm_b_im[l], ssm_c_re[l], ssm_c_im[l], ssm_d[l],
                         glu_w[l], glu_b[l], ssm_out_gain[l], w_out[l])
    return rms_norm(h, final_gain)
```

```python
import functools
import math

import jax
import jax.numpy as jnp
from jax import lax
from jax.experimental import pallas as pl
from jax.experimental.pallas import tpu as pltpu

F32 = jnp.float32
BF16 = jnp.bfloat16

D_MODEL = 2048
HEAD_DIM = 64
N_Q_HEADS = 24
N_KV_HEADS = 3
Q_PER_KV = 8
ATTN_WIDTH = N_Q_HEADS * HEAD_DIM
BLOCK = 128
SSM_GROUP = 16
SSM_WIDTH = D_MODEL - ATTN_WIDTH
N_SSM_GROUPS = SSM_WIDTH // SSM_GROUP
STATE = 64
EPS = 1e-5
NEG = -1e30

SSM_CHUNK = 16
CHUNK_COLS = SSM_CHUNK * SSM_GROUP
KV_DUP = 2 * HEAD_DIM
KVD_WIDTH = N_KV_HEADS * KV_DUP

VMEM_LIMIT = 56 * 1024 * 1024

ROW_TILE = 512
ADA_COLS = 512
SCAN_TILE = 128


def _resident(shape):
    return pl.BlockSpec(shape, lambda *_: (0,) * len(shape), pipeline_mode=pl.Buffered(1))


def _ada_kernel(ct_ref, w_ref, b_ref, o_ref):
    ct = ct_ref[...]
    s = ct * jax.nn.sigmoid(ct)
    w = w_ref[...]
    rows = [jnp.sum(s[:, b:b + 1] * w, axis=0, keepdims=True) for b in range(ct.shape[1])]
    o_ref[...] = jnp.concatenate(rows, axis=0) + b_ref[...]


def _ada(c, w_ada, b_ada):
    bsz, d = c.shape
    n = w_ada.shape[1]
    return pl.pallas_call(
        _ada_kernel,
        out_shape=jax.ShapeDtypeStruct((bsz, n), F32),
        grid=(n // ADA_COLS,),
        in_specs=[pl.BlockSpec((d, bsz), lambda i: (0, 0)),
                  pl.BlockSpec((d, ADA_COLS), lambda i: (0, i)),
                  pl.BlockSpec((1, ADA_COLS), lambda i: (0, i))],
        out_specs=pl.BlockSpec((bsz, ADA_COLS), lambda i: (0, i)),
        compiler_params=pltpu.CompilerParams(dimension_semantics=("arbitrary",)),
        name="ada",
    )(c.T, w_ada, b_ada.reshape(1, n))


def _inproj_kernel(x_ref, sc_ref, sh_ref, g_ref, w_ref, b_ref,
                   q_ref, za_ref, kk_ref, vv_ref, u_ref, zs_ref, h_scr):
    x = x_ref[...]
    var = jnp.mean(x * x, axis=-1, keepdims=True)
    h = x * lax.rsqrt(var + EPS) * g_ref[...]
    h = h * (1.0 + sc_ref[0]) + sh_ref[0]
    h_scr[...] = h.astype(BF16)
    col = 0
    for ref in (q_ref, za_ref, kk_ref, vv_ref, u_ref, zs_ref):
        n = ref.shape[1]
        step = 512 if n % 512 == 0 else n
        for c0 in range(0, n, step):
            acc = jnp.dot(h_scr[...], w_ref[:, col + c0:col + c0 + step],
                          preferred_element_type=F32)
            ref[:, c0:c0 + step] = (acc + b_ref[:, col + c0:col + c0 + step]).astype(BF16)
        col += n


def _inproj(x2, scale, shift, gain, w_cat, b_cat, rows_per_batch):
    t, d = x2.shape
    widths = (ATTN_WIDTH, ATTN_WIDTH, KVD_WIDTH, KVD_WIDTH, SSM_WIDTH, SSM_WIDTH)
    n_all = sum(widths)
    tiles_per_batch = rows_per_batch // ROW_TILE
    mod_spec = pl.BlockSpec((1, 1, d), lambda i: (i // tiles_per_batch, 0, 0))
    return pl.pallas_call(
        _inproj_kernel,
        out_shape=[jax.ShapeDtypeStruct((t, w), BF16) for w in widths],
        grid=(t // ROW_TILE,),
        in_specs=[pl.BlockSpec((ROW_TILE, d), lambda i: (i, 0)),
                  mod_spec, mod_spec,
                  _resident((1, d)),
                  _resident((d, n_all)),
                  _resident((1, n_all))],
        out_specs=[pl.BlockSpec((ROW_TILE, w), lambda i: (i, 0)) for w in widths],
        scratch_shapes=[pltpu.VMEM((ROW_TILE, d), BF16)],
        compiler_params=pltpu.CompilerParams(dimension_semantics=("arbitrary",),
                                             vmem_limit_bytes=VMEM_LIMIT),
        name="inproj",
    )(x2, scale, shift, gain, w_cat, b_cat)


def _ssm_prep_kernel(lr_ref, li_ref, ls_ref, ba_ref, bb_ref, ca_ref, cb_ref, dd_ref,
                     k_ref, v_ref, wt_ref, a16_ref, y_scr):
    lr, li = lr_ref[...], li_ref[...]
    step = jnp.exp(ls_ref[...])
    decay = jnp.exp(lr * step)
    ar, ai = decay * jnp.cos(li * step), decay * jnp.sin(li * step)
    den = lr * lr + li * li
    nr, ni = ar - 1.0, ai
    coef_re = ((nr * lr + ni * li) / den)[:, None, :]
    coef_im = ((ni * lr - nr * li) / den)[:, None, :]
    lane = lax.broadcasted_iota(jnp.int32, (1, 1, 2 * STATE), 2)
    sgn = jnp.where(lane < STATE, 1.0, -1.0).astype(F32)
    ba, bb = ba_ref[...], bb_ref[...]
    bbar = coef_re * ba - sgn * coef_im * bb
    bbar_sw = coef_re * bb + sgn * coef_im * ba
    ca, cb = ca_ref[...], cb_ref[...]
    pr, pi = jnp.ones_like(ar), jnp.zeros_like(ai)
    powers = []
    for _ in range(SSM_CHUNK + 1):
        powers.append((pr, pi))
        pr, pi = pr * ar - pi * ai, pr * ai + pi * ar
    for tau in range(SSM_CHUNK + 1):
        pr, pi = powers[tau]
        e = pr[:, None, :] * ca - sgn * pi[:, None, :] * cb
        if tau < SSM_CHUNK:
            y_scr[:, tau * SSM_GROUP:(tau + 1) * SSM_GROUP, :] = e
        if tau >= 1:
            wt_ref[:, (tau - 1) * SSM_GROUP:tau * SSM_GROUP, :] = sgn * e
        if tau < SSM_CHUNK:
            s = SSM_CHUNK - 1 - tau
            v_ref[:, s * SSM_GROUP:(s + 1) * SSM_GROUP, :] = (
                pr[:, None, :] * bbar - sgn * pi[:, None, :] * bbar_sw)
    a16_ref[...] = jnp.where(lane[0] < STATE, powers[SSM_CHUNK][0], powers[SSM_CHUNK][1])
    k_ref[...] = jnp.einsum("gap,gnp->gan", sgn * bbar, y_scr[...],
                            precision=lax.Precision.HIGHEST,
                            preferred_element_type=F32) + dd_ref[...]


def _ssm_prep(lam_re, lam_im, log_step, b_re, b_im, c_re, c_im, d_skip):
    g = N_SSM_GROUPS
    dup = lambda a: jnp.concatenate([a, a], axis=-1)
    brt, bit = jnp.swapaxes(b_re, 1, 2), jnp.swapaxes(b_im, 1, 2)
    ba, bb = jnp.concatenate([brt, bit], -1), jnp.concatenate([bit, brt], -1)
    ca, cb = jnp.concatenate([c_re, c_im], -1), jnp.concatenate([c_im, c_re], -1)
    eye = jnp.eye(SSM_GROUP, dtype=F32)
    dd = jnp.zeros((g, SSM_GROUP, CHUNK_COLS), F32).at[:, :, :SSM_GROUP].set(
        eye[None] * d_skip.reshape(g, 1, SSM_GROUP))
    return pl.pallas_call(
        _ssm_prep_kernel,
        out_shape=[jax.ShapeDtypeStruct((g, SSM_GROUP, CHUNK_COLS), F32),
                   jax.ShapeDtypeStruct((g, CHUNK_COLS, 2 * STATE), F32),
                   jax.ShapeDtypeStruct((g, CHUNK_COLS, 2 * STATE), F32),
                   jax.ShapeDtypeStruct((g, 2 * STATE), F32)],
        scratch_shapes=[pltpu.VMEM((g, CHUNK_COLS, 2 * STATE), F32)],
        compiler_params=pltpu.CompilerParams(vmem_limit_bytes=VMEM_LIMIT),
        name="ssm_prep",
    )(dup(lam_re), dup(lam_im), log_step.reshape(g, 1), ba, bb, ca, cb, dd)


def _ssm_state_kernel(u_ref, v_ref, s_ref):
    s_ref[0] = jnp.dot(u_ref[0], v_ref[0], preferred_element_type=F32)


def _ssm_state(u_g, v_mat):
    g, n, _ = u_g.shape
    return pl.pallas_call(
        _ssm_state_kernel,
        out_shape=jax.ShapeDtypeStruct((g, n, 2 * STATE), F32),
        grid=(g,),
        in_specs=[pl.BlockSpec((1, n, CHUNK_COLS), lambda i: (i, 0, 0)),
                  pl.BlockSpec((1, CHUNK_COLS, 2 * STATE), lambda i: (i, 0, 0))],
        out_specs=pl.BlockSpec((1, n, 2 * STATE), lambda i: (i, 0, 0)),
        compiler_params=pltpu.CompilerParams(dimension_semantics=("arbitrary",)),
        name="ssm_state",
    )(u_g, v_mat)


def _ssm_scan_kernel(sr_ref, si_ref, ar_ref, ai_ref, hr_ref, hi_ref, cr_scr, ci_scr):
    @pl.when(pl.program_id(0) == 0)
    def _():
        cr_scr[...] = jnp.zeros_like(cr_scr)
        ci_scr[...] = jnp.zeros_like(ci_scr)

    ar, ai = ar_ref[...], ai_ref[...]

    def body(k, carry):
        hr, hi = carry
        hr_ref[k] = hr
        hi_ref[k] = hi
        return (ar * hr - ai * hi + sr_ref[k], ar * hi + ai * hr + si_ref[k])

    hr, hi = lax.fori_loop(0, SCAN_TILE, body, (cr_scr[...], ci_scr[...]), unroll=8)
    cr_scr[...] = hr
    ci_scr[...] = hi


def _ssm_scan(s_re, s_im, a_re, a_im):
    k, rows, lanes = s_re.shape
    blk = pl.BlockSpec((SCAN_TILE, rows, lanes), lambda i: (i, 0, 0))
    return pl.pallas_call(
        _ssm_scan_kernel,
        out_shape=[jax.ShapeDtypeStruct(s_re.shape, F32)] * 2,
        grid=(k // SCAN_TILE,),
        in_specs=[blk, blk, _resident((rows, lanes)), _resident((rows, lanes))],
        out_specs=[blk, blk],
        scratch_shapes=[pltpu.VMEM((rows, lanes), F32)] * 2,
        compiler_params=pltpu.CompilerParams(dimension_semantics=("arbitrary",)),
        name="ssm_scan",
    )(s_re, s_im, a_re, a_im)


def _ssm_out_kernel(u_ref, h_ref, t_ref, wt_ref, y_ref):
    y = jnp.dot(u_ref[0], t_ref[0], preferred_element_type=F32)
    y += lax.dot_general(h_ref[0], wt_ref[0], (((1,), (1,)), ((), ())),
                         preferred_element_type=F32)
    y_ref[0] = y


def _ssm_out(u_g, h_g, t_mat, wt_mat):
    g, n, _ = u_g.shape
    return pl.pallas_call(
        _ssm_out_kernel,
        out_shape=jax.ShapeDtypeStruct((g, n, CHUNK_COLS), F32),
        grid=(g,),
        in_specs=[pl.BlockSpec((1, n, CHUNK_COLS), lambda i: (i, 0, 0)),
                  pl.BlockSpec((1, n, 2 * STATE), lambda i: (i, 0, 0)),
                  pl.BlockSpec((1, CHUNK_COLS, CHUNK_COLS), lambda i: (i, 0, 0)),
                  pl.BlockSpec((1, CHUNK_COLS, 2 * STATE), lambda i: (i, 0, 0))],
        out_specs=pl.BlockSpec((1, n, CHUNK_COLS), lambda i: (i, 0, 0)),
        compiler_params=pltpu.CompilerParams(dimension_semantics=("arbitrary",)),
        name="ssm_out",
    )(u_g, h_g, t_mat, wt_mat)


def _s5(u, prep, bsz, seq):
    k_mat, v_mat, wt_mat, a16 = prep
    g, p = N_SSM_GROUPS, STATE
    t = u.shape[0]
    n = t // SSM_CHUNK
    kc = seq // SSM_CHUNK
    k4 = k_mat.reshape(g, SSM_GROUP, SSM_CHUNK, SSM_GROUP)
    lag = jnp.arange(SSM_CHUNK)[None, :] - jnp.arange(SSM_CHUNK)[:, None]
    t5 = jnp.where((lag >= 0)[None, None, :, :, None],
                   k4[:, :, jnp.clip(lag, 0), :], 0.0)
    t_mat = t5.transpose(0, 2, 1, 3, 4).reshape(g, CHUNK_COLS, CHUNK_COLS).astype(BF16)

    u_g = u.reshape(n, SSM_CHUNK, g, SSM_GROUP).transpose(2, 0, 1, 3).reshape(g, n, CHUNK_COLS)
    s_all = _ssm_state(u_g, v_mat.astype(BF16))
    s6 = s_all.reshape(g // 2, 2, bsz, kc, 2, p).transpose(4, 3, 2, 0, 1, 5)
    s6 = s6.reshape(2, kc, bsz * (g // 2), 2 * p)
    a2 = jnp.tile(a16.reshape(g // 2, 2, 2, p).transpose(2, 0, 1, 3).reshape(2, g // 2, 2 * p),
                  (1, bsz, 1))
    h_re, h_im = _ssm_scan(s6[0], s6[1], a2[0], a2[1])
    h6 = jnp.stack([h_re, h_im]).reshape(2, kc, bsz, g // 2, 2, p).transpose(3, 4, 2, 1, 0, 5)
    h_g = h6.reshape(g, n, 2 * p).astype(BF16)
    y_g = _ssm_out(u_g, h_g, t_mat, wt_mat.astype(BF16))
    return y_g.reshape(g, n, SSM_CHUNK, SSM_GROUP).transpose(1, 2, 0, 3).reshape(t, SSM_WIDTH)


def _attn_out_kernel(sink_ref, q_ref, za_ref, kk_ref, vv_ref, kkp_ref, vvp_ref, y_ref, zs_ref,
                     x_ref, gate_ref, ag_ref, gw_ref, gb_ref, sg_ref, wo_ref, fg_ref,
                     o_ref, kk_scr, vv_scr, attn_scr, mix_scr):
    first = pl.program_id(1) == 0
    keep = jnp.where(first, 0.0, 1.0).astype(BF16)
    kk_scr[0:BLOCK, :] = kkp_ref[...] * keep
    vv_scr[0:BLOCK, :] = vvp_ref[...] * keep
    kk_scr[BLOCK:, :] = kk_ref[...]
    vv_scr[BLOCK:, :] = vv_ref[...]

    qi = lax.broadcasted_iota(jnp.int32, (BLOCK, 2 * BLOCK), 0)
    kj = lax.broadcasted_iota(jnp.int32, (BLOCK, 2 * BLOCK), 1)
    in_band = (kj > qi) & (kj <= qi + BLOCK)
    bias_any = jnp.where(in_band, 0.0, NEG).astype(F32)
    bias_first = jnp.where(in_band & (kj >= BLOCK), 0.0, NEG).astype(F32)
    lane_lo_kv = lax.broadcasted_iota(jnp.int32, (2 * BLOCK, KV_DUP), 1) < HEAD_DIM
    lane_lo_o = lax.broadcasted_iota(jnp.int32, (BLOCK, KV_DUP), 1) < HEAD_DIM
    pairs = Q_PER_KV // 2
    contract_last = (((1,), (1,)), ((), ()))

    def block(j, carry):
        r0 = pl.multiple_of(j * BLOCK, BLOCK)
        bias = jnp.where(first & (j == 0), bias_first, bias_any)
        for h in range(N_KV_HEADS):
            kb = kk_scr[pl.ds(r0, 2 * BLOCK), h * KV_DUP:(h + 1) * KV_DUP]
            vb = vv_scr[pl.ds(r0, 2 * BLOCK), h * KV_DUP:(h + 1) * KV_DUP]
            zero = jnp.zeros_like(kb)
            k_par = (jnp.where(lane_lo_kv, kb, zero), jnp.where(lane_lo_kv, zero, kb))
            v_par = (jnp.where(lane_lo_kv, vb, zero), jnp.where(lane_lo_kv, zero, vb))
            q0 = h * Q_PER_KV * HEAD_DIM
            qs = jnp.concatenate(
                [q_ref[pl.ds(r0, BLOCK), q0 + p * KV_DUP:q0 + (p + 1) * KV_DUP] for p in range(pairs)],
                axis=0)
            acc = None
            inv = [[None, None] for _ in range(pairs)]
            for par in range(2):
                s_all = lax.dot_general(qs, k_par[par], contract_last, preferred_element_type=F32)
                probs = []
                for p in range(pairs):
                    s = s_all[p * BLOCK:(p + 1) * BLOCK] + bias
                    sink = sink_ref[h * Q_PER_KV + 2 * p + par]
                    m = jnp.maximum(jnp.max(s, axis=-1, keepdims=True), sink)
                    e = jnp.exp(s - m)
                    l = jnp.sum(e, axis=-1, keepdims=True) + jnp.exp(sink - m)
                    inv[p][par] = 1.0 / l
                    probs.append(e.astype(BF16))
                pv = jnp.dot(jnp.concatenate(probs, axis=0), v_par[par], preferred_element_type=F32)
                acc = pv if acc is None else acc + pv
            for p in range(pairs):
                o = acc[p * BLOCK:(p + 1) * BLOCK] * jnp.where(lane_lo_o, inv[p][0], inv[p][1])
                attn_scr[pl.ds(r0, BLOCK), q0 + p * KV_DUP:q0 + (p + 1) * KV_DUP] = o
        return carry

    lax.fori_loop(0, q_ref.shape[0] // BLOCK, block, 0)

    a = attn_scr[...]
    a = a * lax.rsqrt(jnp.mean(a * a, axis=-1, keepdims=True) + EPS) * ag_ref[...]
    za = za_ref[...].astype(F32)
    mix_scr[:, :ATTN_WIDTH] = (a * (za * jax.nn.sigmoid(za))).astype(BF16)
    y = y_ref[...]
    y = 0.5 * y * (1.0 + lax.erf(y * math.sqrt(0.5)))
    glu = jnp.dot(y.astype(BF16), gw_ref[...], preferred_element_type=F32) + gb_ref[...]
    y = y * jax.nn.sigmoid(glu)
    y = y * lax.rsqrt(jnp.mean(y * y, axis=-1, keepdims=True) + EPS) * sg_ref[...]
    zs = zs_ref[...].astype(F32)
    mix_scr[:, ATTN_WIDTH:] = (y * (zs * jax.nn.sigmoid(zs))).astype(BF16)
    out = jnp.dot(mix_scr[...], wo_ref[...], preferred_element_type=F32)
    res = x_ref[...] + gate_ref[0] * out
    o_ref[...] = res * lax.rsqrt(jnp.mean(res * res, axis=-1, keepdims=True) + EPS) * fg_ref[...]


def _attn_out(sinks, q, za, kk, vv, y, zs, x2, gate, attn_gain, glu_w, glu_b, ssm_gain, w_out,
              final_gain, rows_per_batch):
    t, d = x2.shape
    tiles = rows_per_batch // ROW_TILE
    blocks_per_tile = ROW_TILE // BLOCK
    row = lambda w: pl.BlockSpec((ROW_TILE, w), lambda b, i: (b * tiles + i, 0))
    prev = pl.BlockSpec(
        (BLOCK, KVD_WIDTH),
        lambda b, i: (jnp.maximum((b * tiles + i) * blocks_per_tile - 1, 0), 0))
    return pl.pallas_call(
        _attn_out_kernel,
        out_shape=jax.ShapeDtypeStruct((t, d), F32),
        grid=(t // rows_per_batch, tiles),
        in_specs=[pl.BlockSpec(memory_space=pltpu.SMEM),
                  row(ATTN_WIDTH), row(ATTN_WIDTH), row(KVD_WIDTH), row(KVD_WIDTH), prev, prev,
                  row(SSM_WIDTH), row(SSM_WIDTH), row(d),
                  pl.BlockSpec((1, 1, d), lambda b, i: (b, 0, 0)),
                  _resident((1, ATTN_WIDTH)),
                  _resident((SSM_WIDTH, SSM_WIDTH)), _resident((1, SSM_WIDTH)),
                  _resident((1, SSM_WIDTH)),
                  _resident((d, d)), _resident((1, d))],
        out_specs=row(d),
        scratch_shapes=[pltpu.VMEM((ROW_TILE + BLOCK, KVD_WIDTH), BF16),
                        pltpu.VMEM((ROW_TILE + BLOCK, KVD_WIDTH), BF16),
                        pltpu.VMEM((ROW_TILE, ATTN_WIDTH), F32),
                        pltpu.VMEM((ROW_TILE, d), BF16)],
        compiler_params=pltpu.CompilerParams(dimension_semantics=("arbitrary", "arbitrary"),
                                             vmem_limit_bytes=VMEM_LIMIT),
        name="attn_out",
    )(sinks, q, za, kk, vv, kk, vv, y, zs, x2, gate, attn_gain, glu_w, glu_b, ssm_gain, w_out,
      final_gain)


def _pack_in_weights(w_in, b_in):
    a, kv = ATTN_WIDTH, N_KV_HEADS * HEAD_DIM
    o_k, o_v, o_za, o_u, o_zs = a, a + kv, a + 2 * kv, 2 * a + 2 * kv, 2 * a + 2 * kv + SSM_WIDTH

    def pack(m):
        dup = lambda blk: jnp.concatenate(
            [blk.reshape(-1, N_KV_HEADS, 1, HEAD_DIM)] * 2, axis=2).reshape(-1, KVD_WIDTH)
        return jnp.concatenate(
            [m[:, :a] * (HEAD_DIM ** -0.5), m[:, o_za:o_u], dup(m[:, o_k:o_v]), dup(m[:, o_v:o_za]),
             m[:, o_u:o_zs], m[:, o_zs:]], axis=1)

    return pack(w_in).astype(BF16), pack(b_in.reshape(1, -1))


def kernel(x, c, w_ada, b_ada, norm_gain, w_in, b_in, attn_sinks, attn_out_gain, ssm_lambda_re,
           ssm_lambda_im, ssm_log_step, ssm_b_re, ssm_b_im, ssm_c_re, ssm_c_im, ssm_d, glu_w, glu_b,
           ssm_out_gain, w_out, final_gain):
    bsz, seq, d = x.shape
    assert w_ada.shape[0] == 1, "single-layer trunk only"
    x2 = x.reshape(bsz * seq, d)
    mod = _ada(c, w_ada[0], b_ada[0])
    shift, scale, gate = (m.reshape(bsz, 1, d) for m in jnp.split(mod, 3, axis=-1))
    w_cat, b_cat = _pack_in_weights(w_in[0], b_in[0])
    q, za, kk, vv, u, zs = _inproj(x2, scale, shift, norm_gain[0].reshape(1, d), w_cat, b_cat, seq)
    prep = _ssm_prep(ssm_lambda_re[0], ssm_lambda_im[0], ssm_log_step[0], ssm_b_re[0],
                     ssm_b_im[0], ssm_c_re[0], ssm_c_im[0], ssm_d[0])
    y = _s5(u, prep, bsz, seq)
    out = _attn_out(attn_sinks[0], q, za, kk, vv, y, zs, x2, gate,
                    attn_out_gain[0].reshape(1, -1), glu_w[0].astype(BF16),
                    glu_b[0].reshape(1, -1), ssm_out_gain[0].reshape(1, -1),
                    w_out[0].astype(BF16), final_gain.reshape(1, d), seq)
    return out.reshape(bsz, seq, d)
```

```python
import math

import jax
import jax.numpy as jnp
from jax import lax
from jax.experimental import pallas as pl
from jax.experimental.pallas import tpu as pltpu

F32 = jnp.float32
BF16 = jnp.bfloat16

D_MODEL = 2048
HEAD_DIM = 64
N_Q_HEADS = 24
N_KV_HEADS = 3
Q_PER_KV = 8
ATTN_WIDTH = N_Q_HEADS * HEAD_DIM
BLOCK = 128
SSM_GROUP = 16
SSM_WIDTH = D_MODEL - ATTN_WIDTH
N_SSM_GROUPS = SSM_WIDTH // SSM_GROUP
STATE = 64
EPS = 1e-5
NEG = -1e30

LANES = 128
SSM_CHUNK = 16
CHUNK_COLS = SSM_CHUNK * SSM_GROUP
GROUPS_PER_BLOCK = LANES // SSM_GROUP
N_GROUP_BLOCKS = N_SSM_GROUPS // GROUPS_PER_BLOCK
SLAB = SSM_CHUNK * LANES
STATE_COLS = GROUPS_PER_BLOCK * STATE
KV_DUP = 2 * HEAD_DIM
KVD_WIDTH = N_KV_HEADS * KV_DUP

VMEM_LIMIT = 56 * 1024 * 1024

ROW_TILE = 512
CHUNK_ROWS = ROW_TILE // SSM_CHUNK
ADA_COLS = 512
SSM_TILE = 512
SCAN_TILE = 128


def _resident(shape):
    return pl.BlockSpec(shape, lambda *_: (0,) * len(shape), pipeline_mode=pl.Buffered(1))


def _ada_kernel(ct_ref, w_ref, b_ref, o_ref):
    ct = ct_ref[...]
    s = ct * jax.nn.sigmoid(ct)
    w = w_ref[...]
    rows = [jnp.sum(s[:, b:b + 1] * w, axis=0, keepdims=True) for b in range(ct.shape[1])]
    o_ref[...] = jnp.concatenate(rows, axis=0) + b_ref[...]


def _ada(c, w_ada, b_ada):
    bsz, d = c.shape
    n = w_ada.shape[1]
    return pl.pallas_call(
        _ada_kernel,
        out_shape=jax.ShapeDtypeStruct((bsz, n), F32),
        grid=(n // ADA_COLS,),
        in_specs=[pl.BlockSpec((d, bsz), lambda i: (0, 0)),
                  pl.BlockSpec((d, ADA_COLS), lambda i: (0, i)),
                  pl.BlockSpec((1, ADA_COLS), lambda i: (0, i))],
        out_specs=pl.BlockSpec((bsz, ADA_COLS), lambda i: (0, i)),
        compiler_params=pltpu.CompilerParams(dimension_semantics=("arbitrary",)),
        name="ada",
    )(c.T, w_ada, b_ada.reshape(1, n))


def _inproj_kernel(x_ref, sc_ref, sh_ref, g_ref, w_ref, b_ref,
                   q_ref, za_ref, kk_ref, vv_ref, zs_ref, u2_ref, h_scr, u_scr):
    x = x_ref[...]
    var = jnp.mean(x * x, axis=-1, keepdims=True)
    h = x * lax.rsqrt(var + EPS) * g_ref[...]
    h = h * (1.0 + sc_ref[0]) + sh_ref[0]
    h_scr[...] = h.astype(BF16)

    def project(col, width):
        acc = jnp.dot(h_scr[...], w_ref[:, col:col + width], preferred_element_type=F32)
        return acc + b_ref[:, col:col + width]

    col = 0
    for ref in (q_ref, za_ref, kk_ref, vv_ref, zs_ref):
        n = ref.shape[1]
        step = 512 if n % 512 == 0 else n
        for c0 in range(0, n, step):
            ref[:, c0:c0 + step] = project(col + c0, step).astype(BF16)
        col += n
    u = project(col, SSM_WIDTH)
    for gb in range(N_GROUP_BLOCKS):
        u_scr[gb] = u[:, gb * LANES:(gb + 1) * LANES]
        for tok in range(SSM_CHUNK):
            c0 = gb * SLAB + tok * LANES
            u2_ref[:, c0:c0 + LANES] = (
                u_scr[gb, pl.ds(tok, CHUNK_ROWS, stride=SSM_CHUNK), :].astype(BF16))


def _inproj(x2, scale, shift, gain, w_cat, b_cat, rows_per_batch):
    t, d = x2.shape
    widths = (ATTN_WIDTH, ATTN_WIDTH, KVD_WIDTH, KVD_WIDTH, SSM_WIDTH)
    n_all = sum(widths) + SSM_WIDTH
    tiles_per_batch = rows_per_batch // ROW_TILE
    mod_spec = pl.BlockSpec((1, 1, d), lambda i: (i // tiles_per_batch, 0, 0))
    u2_cols = N_GROUP_BLOCKS * SLAB
    return pl.pallas_call(
        _inproj_kernel,
        out_shape=[jax.ShapeDtypeStruct((t, w), BF16) for w in widths]
        + [jax.ShapeDtypeStruct((t // SSM_CHUNK, u2_cols), BF16)],
        grid=(t // ROW_TILE,),
        in_specs=[pl.BlockSpec((ROW_TILE, d), lambda i: (i, 0)),
                  mod_spec, mod_spec,
                  _resident((1, d)),
                  _resident((d, n_all)),
                  _resident((1, n_all))],
        out_specs=[pl.BlockSpec((ROW_TILE, w), lambda i: (i, 0)) for w in widths]
        + [pl.BlockSpec((CHUNK_ROWS, u2_cols), lambda i: (i, 0))],
        scratch_shapes=[pltpu.VMEM((ROW_TILE, d), BF16),
                        pltpu.VMEM((N_GROUP_BLOCKS, ROW_TILE, LANES), F32)],
        compiler_params=pltpu.CompilerParams(dimension_semantics=("arbitrary",),
                                             vmem_limit_bytes=VMEM_LIMIT),
        name="inproj",
    )(x2, scale, shift, gain, w_cat, b_cat)


def _ssm_prep_kernel(lr_ref, li_ref, ls_ref, ba_ref, bb_ref, ca_ref, cb_ref, dd_ref,
                     k_ref, v_ref, wt_ref, a16_ref, y_scr):
    lr, li = lr_ref[...], li_ref[...]
    step = jnp.exp(ls_ref[...])
    decay = jnp.exp(lr * step)
    ar, ai = decay * jnp.cos(li * step), decay * jnp.sin(li * step)
    den = lr * lr + li * li
    nr, ni = ar - 1.0, ai
    coef_re = ((nr * lr + ni * li) / den)[:, None, :]
    coef_im = ((ni * lr - nr * li) / den)[:, None, :]
    lane = lax.broadcasted_iota(jnp.int32, (1, 1, 2 * STATE), 2)
    sgn = jnp.where(lane < STATE, 1.0, -1.0).astype(F32)
    ba, bb = ba_ref[...], bb_ref[...]
    bbar = coef_re * ba - sgn * coef_im * bb
    bbar_sw = coef_re * bb + sgn * coef_im * ba
    ca, cb = ca_ref[...], cb_ref[...]
    pr, pi = jnp.ones_like(ar), jnp.zeros_like(ai)
    powers = []
    for _ in range(SSM_CHUNK + 1):
        powers.append((pr, pi))
        pr, pi = pr * ar - pi * ai, pr * ai + pi * ar
    for tau in range(SSM_CHUNK + 1):
        pr, pi = powers[tau]
        e = pr[:, None, :] * ca - sgn * pi[:, None, :] * cb
        if tau < SSM_CHUNK:
            y_scr[:, tau * SSM_GROUP:(tau + 1) * SSM_GROUP, :] = e
        if tau >= 1:
            wt_ref[:, (tau - 1) * SSM_GROUP:tau * SSM_GROUP, :] = sgn * e
        if tau < SSM_CHUNK:
            s = SSM_CHUNK - 1 - tau
            v_ref[:, s * SSM_GROUP:(s + 1) * SSM_GROUP, :] = (
                pr[:, None, :] * bbar - sgn * pi[:, None, :] * bbar_sw)
    a16_ref[...] = jnp.where(lane[0] < STATE, powers[SSM_CHUNK][0], powers[SSM_CHUNK][1])
    k_ref[...] = jnp.einsum("gap,gnp->gan", sgn * bbar, y_scr[...],
                            precision=lax.Precision.HIGHEST,
                            preferred_element_type=F32) + dd_ref[...]


def _ssm_prep(lam_re, lam_im, log_step, b_re, b_im, c_re, c_im, d_skip):
    g = N_SSM_GROUPS
    dup = lambda a: jnp.concatenate([a, a], axis=-1)
    brt, bit = jnp.swapaxes(b_re, 1, 2), jnp.swapaxes(b_im, 1, 2)
    ba, bb = jnp.concatenate([brt, bit], -1), jnp.concatenate([bit, brt], -1)
    ca, cb = jnp.concatenate([c_re, c_im], -1), jnp.concatenate([c_im, c_re], -1)
    eye = jnp.eye(SSM_GROUP, dtype=bool)
    dd = jnp.zeros((g, SSM_GROUP, CHUNK_COLS), F32).at[:, :, :SSM_GROUP].set(
        jnp.where(eye[None], d_skip.reshape(g, 1, SSM_GROUP), 0.0))
    return pl.pallas_call(
        _ssm_prep_kernel,
        out_shape=[jax.ShapeDtypeStruct((g, SSM_GROUP, CHUNK_COLS), F32),
                   jax.ShapeDtypeStruct((g, CHUNK_COLS, 2 * STATE), F32),
                   jax.ShapeDtypeStruct((g, CHUNK_COLS, 2 * STATE), F32),
                   jax.ShapeDtypeStruct((g, 2 * STATE), F32)],
        scratch_shapes=[pltpu.VMEM((g, CHUNK_COLS, 2 * STATE), F32)],
        compiler_params=pltpu.CompilerParams(vmem_limit_bytes=VMEM_LIMIT),
        name="ssm_prep",
    )(dup(lam_re), dup(lam_im), log_step.reshape(g, 1), ba, bb, ca, cb, dd)


def _block_diagonal(base, row_axis):
    same = jnp.eye(GROUPS_PER_BLOCK, dtype=bool)
    shape = [1] * (base.ndim + 1)
    shape[row_axis], shape[-2] = GROUPS_PER_BLOCK, GROUPS_PER_BLOCK
    return jnp.where(same.reshape(shape), jnp.expand_dims(base, -2), 0.0)


def _chunk_matrices(prep):
    k_mat, v_mat, wt_mat, _ = prep
    nb, gi, c, ch, p = N_GROUP_BLOCKS, GROUPS_PER_BLOCK, SSM_GROUP, SSM_CHUNK, STATE
    k4 = k_mat.reshape(nb, gi, c, ch, c)
    lag = jnp.arange(ch)[None, :] - jnp.arange(ch)[:, None]
    t6 = jnp.where((lag >= 0)[None, None, None, :, :, None],
                   k4[:, :, :, jnp.clip(lag, 0), :], 0.0)
    t_mat = _block_diagonal(t6.transpose(0, 3, 1, 2, 4, 5), 2)
    t_mat = t_mat.reshape(nb, SLAB, SLAB).astype(BF16)
    v6 = v_mat.reshape(nb, gi, ch, c, 2, p).transpose(0, 2, 1, 3, 4, 5)
    v_blk = _block_diagonal(v6, 2).reshape(nb, SLAB, 2 * STATE_COLS).astype(BF16)
    w6 = wt_mat.reshape(nb, gi, ch, c, 2, p).transpose(0, 4, 1, 5, 2, 3)
    w_blk = _block_diagonal(w6, 2).reshape(nb, 2, STATE_COLS, SLAB).astype(BF16)
    return t_mat, v_blk, w_blk[:, 0], w_blk[:, 1]


def _ssm_state_kernel(u_ref, v_ref, sr_ref, si_ref):
    s = jnp.dot(u_ref[...], v_ref[0], preferred_element_type=F32)
    sr_ref[...] = s[:, :STATE_COLS]
    si_ref[...] = s[:, STATE_COLS:]


def _ssm_state(u2, v_blk):
    n = u2.shape[0]
    state = pl.BlockSpec((SSM_TILE, STATE_COLS), lambda gb, i: (i, gb))
    return pl.pallas_call(
        _ssm_state_kernel,
        out_shape=[jax.ShapeDtypeStruct((n, N_GROUP_BLOCKS * STATE_COLS), F32)] * 2,
        grid=(N_GROUP_BLOCKS, n // SSM_TILE),
        in_specs=[pl.BlockSpec((SSM_TILE, SLAB), lambda gb, i: (i, gb)),
                  pl.BlockSpec((1, SLAB, 2 * STATE_COLS), lambda gb, i: (gb, 0, 0))],
        out_specs=[state, state],
        compiler_params=pltpu.CompilerParams(dimension_semantics=("arbitrary", "arbitrary"),
                                             vmem_limit_bytes=VMEM_LIMIT),
        name="ssm_state",
    )(u2, v_blk)


def _ssm_scan_kernel(sr_ref, si_ref, ar_ref, ai_ref, hr_ref, hi_ref, cr_scr, ci_scr):
    @pl.when(pl.program_id(0) == 0)
    def _():
        cr_scr[...] = jnp.zeros_like(cr_scr)
        ci_scr[...] = jnp.zeros_like(ci_scr)

    ar, ai = ar_ref[...], ai_ref[...]
    nb = sr_ref.shape[0]

    def body(k, carry):
        out = []
        for b in range(nb):
            hr, hi = carry[2 * b], carry[2 * b + 1]
            hr_ref[b, k] = hr
            hi_ref[b, k] = hi
            out += [ar * hr - ai * hi + sr_ref[b, k], ar * hi + ai * hr + si_ref[b, k]]
        return tuple(out)

    init = tuple(scr[b] for b in range(nb) for scr in (cr_scr, ci_scr))
    fin = lax.fori_loop(0, SCAN_TILE, body, init, unroll=8)
    for b in range(nb):
        cr_scr[b] = fin[2 * b]
        ci_scr[b] = fin[2 * b + 1]


def _ssm_scan(s_re, s_im, a_re, a_im):
    bsz, k, rows, lanes = s_re.shape
    blk = pl.BlockSpec((bsz, SCAN_TILE, rows, lanes), lambda i: (0, i, 0, 0))
    return pl.pallas_call(
        _ssm_scan_kernel,
        out_shape=[jax.ShapeDtypeStruct(s_re.shape, F32)] * 2,
        grid=(k // SCAN_TILE,),
        in_specs=[blk, blk, _resident((rows, lanes)), _resident((rows, lanes))],
        out_specs=[blk, blk],
        scratch_shapes=[pltpu.VMEM((bsz, rows, lanes), F32)] * 2,
        compiler_params=pltpu.CompilerParams(dimension_semantics=("arbitrary",)),
        name="ssm_scan",
    )(s_re, s_im, a_re, a_im)


def _ssm_out_kernel(u_ref, hr_ref, hi_ref, t_ref, wr_ref, wi_ref, y_ref):
    pair = 2 * LANES
    for tb in range(SLAB // pair):
        k_rows = (tb + 1) * pair
        cols = slice(tb * pair, (tb + 1) * pair)
        y = jnp.dot(u_ref[:, :k_rows], t_ref[0, :k_rows, cols], preferred_element_type=F32)
        y += jnp.dot(hr_ref[...], wr_ref[0, :, cols], preferred_element_type=F32)
        y += jnp.dot(hi_ref[...], wi_ref[0, :, cols], preferred_element_type=F32)
        y_ref[:, cols] = y


def _ssm_out(u2, h_re, h_im, t_mat, w_re, w_im):
    n = u2.shape[0]
    state = pl.BlockSpec((SSM_TILE, STATE_COLS), lambda gb, i: (i, gb))
    slab = pl.BlockSpec((SSM_TILE, SLAB), lambda gb, i: (i, gb))
    w_spec = pl.BlockSpec((1, STATE_COLS, SLAB), lambda gb, i: (gb, 0, 0))
    return pl.pallas_call(
        _ssm_out_kernel,
        out_shape=jax.ShapeDtypeStruct(u2.shape, F32),
        grid=(N_GROUP_BLOCKS, n // SSM_TILE),
        in_specs=[slab, state, state,
                  pl.BlockSpec((1, SLAB, SLAB), lambda gb, i: (gb, 0, 0)), w_spec, w_spec],
        out_specs=slab,
        compiler_params=pltpu.CompilerParams(dimension_semantics=("arbitrary", "arbitrary"),
                                             vmem_limit_bytes=VMEM_LIMIT),
        name="ssm_out",
    )(u2, h_re, h_im, t_mat, w_re, w_im)


def _s5(u2, prep, bsz):
    t_mat, v_blk, w_re, w_im = _chunk_matrices(prep)
    a16 = prep[3]
    n = u2.shape[0]
    kc = n // bsz
    s_re, s_im = _ssm_state(u2, v_blk)
    scan_shape = (bsz, kc, N_SSM_GROUPS * STATE // LANES, LANES)
    a_re = a16[:, :STATE].reshape(scan_shape[2:])
    a_im = a16[:, STATE:].reshape(scan_shape[2:])
    h_re, h_im = _ssm_scan(s_re.reshape(scan_shape), s_im.reshape(scan_shape), a_re, a_im)
    h_re = h_re.reshape(n, -1).astype(BF16)
    h_im = h_im.reshape(n, -1).astype(BF16)
    return _ssm_out(u2, h_re, h_im, t_mat, w_re, w_im)


def _attn_out_kernel(sink_ref, q_ref, za_ref, kk_ref, vv_ref, kkp_ref, vvp_ref, y2_ref, zs_ref,
                     x_ref, gate_ref, ag_ref, gw_ref, gb_ref, sg_ref, wo_ref, fg_ref,
                     o_ref, kk_scr, vv_scr, attn_scr, y_scr, mix_scr):
    first = pl.program_id(1) == 0
    keep = jnp.where(first, 0.0, 1.0).astype(BF16)
    kk_scr[0:BLOCK, :] = kkp_ref[...] * keep
    vv_scr[0:BLOCK, :] = vvp_ref[...] * keep
    kk_scr[BLOCK:, :] = kk_ref[...]
    vv_scr[BLOCK:, :] = vv_ref[...]

    qi = lax.broadcasted_iota(jnp.int32, (BLOCK, 2 * BLOCK), 0)
    kj = lax.broadcasted_iota(jnp.int32, (BLOCK, 2 * BLOCK), 1)
    in_band = (kj > qi) & (kj <= qi + BLOCK)
    bias_any = jnp.where(in_band, 0.0, NEG).astype(F32)
    bias_first = jnp.where(in_band & (kj >= BLOCK), 0.0, NEG).astype(F32)
    lane_lo_kv = lax.broadcasted_iota(jnp.int32, (2 * BLOCK, KV_DUP), 1) < HEAD_DIM
    lane_lo_o = lax.broadcasted_iota(jnp.int32, (BLOCK, KV_DUP), 1) < HEAD_DIM
    pairs = Q_PER_KV // 2
    contract_last = (((1,), (1,)), ((), ()))

    def block(j, carry):
        r0 = pl.multiple_of(j * BLOCK, BLOCK)
        bias = jnp.where(first & (j == 0), bias_first, bias_any)
        for h in range(N_KV_HEADS):
            kb = kk_scr[pl.ds(r0, 2 * BLOCK), h * KV_DUP:(h + 1) * KV_DUP]
            vb = vv_scr[pl.ds(r0, 2 * BLOCK), h * KV_DUP:(h + 1) * KV_DUP]
            zero = jnp.zeros_like(kb)
            k_par = (jnp.where(lane_lo_kv, kb, zero), jnp.where(lane_lo_kv, zero, kb))
            v_par = (jnp.where(lane_lo_kv, vb, zero), jnp.where(lane_lo_kv, zero, vb))
            q0 = h * Q_PER_KV * HEAD_DIM
            qs = jnp.concatenate(
                [q_ref[pl.ds(r0, BLOCK), q0 + p * KV_DUP:q0 + (p + 1) * KV_DUP] for p in range(pairs)],
                axis=0)
            acc = None
            inv = [[None, None] for _ in range(pairs)]
            for par in range(2):
                s_all = lax.dot_general(qs, k_par[par], contract_last, preferred_element_type=F32)
                probs = []
                for p in range(pairs):
                    s = s_all[p * BLOCK:(p + 1) * BLOCK] + bias
                    sink = sink_ref[h * Q_PER_KV + 2 * p + par]
                    m = jnp.maximum(jnp.max(s, axis=-1, keepdims=True), sink)
                    e = jnp.exp(s - m)
                    l = jnp.sum(e, axis=-1, keepdims=True) + jnp.exp(sink - m)
                    inv[p][par] = 1.0 / l
                    probs.append(e.astype(BF16))
                pv = jnp.dot(jnp.concatenate(probs, axis=0), v_par[par], preferred_element_type=F32)
                acc = pv if acc is None else acc + pv
            for p in range(pairs):
                o = acc[p * BLOCK:(p + 1) * BLOCK] * jnp.where(lane_lo_o, inv[p][0], inv[p][1])
                attn_scr[pl.ds(r0, BLOCK), q0 + p * KV_DUP:q0 + (p + 1) * KV_DUP] = o
        return carry

    lax.fori_loop(0, q_ref.shape[0] // BLOCK, block, 0)

    a = attn_scr[...]
    a = a * lax.rsqrt(jnp.mean(a * a, axis=-1, keepdims=True) + EPS) * ag_ref[...]
    za = za_ref[...].astype(F32)
    mix_scr[:, :ATTN_WIDTH] = (a * (za * jax.nn.sigmoid(za))).astype(BF16)
    for gb in range(N_GROUP_BLOCKS):
        for tok in range(SSM_CHUNK):
            c0 = gb * SLAB + tok * LANES
            y_scr[gb, pl.ds(tok, CHUNK_ROWS, stride=SSM_CHUNK), :] = y2_ref[:, c0:c0 + LANES]
    y = jnp.concatenate([y_scr[gb] for gb in range(N_GROUP_BLOCKS)], axis=1)
    y = 0.5 * y * (1.0 + lax.erf(y * math.sqrt(0.5)))
    glu = jnp.dot(y.astype(BF16), gw_ref[...], preferred_element_type=F32) + gb_ref[...]
    y = y * jax.nn.sigmoid(glu)
    y = y * lax.rsqrt(jnp.mean(y * y, axis=-1, keepdims=True) + EPS) * sg_ref[...]
    zs = zs_ref[...].astype(F32)
    mix_scr[:, ATTN_WIDTH:] = (y * (zs * jax.nn.sigmoid(zs))).astype(BF16)
    out = jnp.dot(mix_scr[...], wo_ref[...], preferred_element_type=F32)
    res = x_ref[...] + gate_ref[0] * out
    o_ref[...] = res * lax.rsqrt(jnp.mean(res * res, axis=-1, keepdims=True) + EPS) * fg_ref[...]


def _attn_out(sinks, q, za, kk, vv, y2, zs, x2, gate, attn_gain, glu_w, glu_b, ssm_gain, w_out,
              final_gain, rows_per_batch):
    t, d = x2.shape
    tiles = rows_per_batch // ROW_TILE
    blocks_per_tile = ROW_TILE // BLOCK
    row = lambda w: pl.BlockSpec((ROW_TILE, w), lambda b, i: (b * tiles + i, 0))
    prev = pl.BlockSpec(
        (BLOCK, KVD_WIDTH),
        lambda b, i: (jnp.maximum((b * tiles + i) * blocks_per_tile - 1, 0), 0))
    return pl.pallas_call(
        _attn_out_kernel,
        out_shape=jax.ShapeDtypeStruct((t, d), F32),
        grid=(t // rows_per_batch, tiles),
        in_specs=[pl.BlockSpec(memory_space=pltpu.SMEM),
                  row(ATTN_WIDTH), row(ATTN_WIDTH), row(KVD_WIDTH), row(KVD_WIDTH), prev, prev,
                  pl.BlockSpec((CHUNK_ROWS, y2.shape[1]), lambda b, i: (b * tiles + i, 0)),
                  row(SSM_WIDTH), row(d),
                  pl.BlockSpec((1, 1, d), lambda b, i: (b, 0, 0)),
                  _resident((1, ATTN_WIDTH)),
                  _resident((SSM_WIDTH, SSM_WIDTH)), _resident((1, SSM_WIDTH)),
                  _resident((1, SSM_WIDTH)),
                  _resident((d, d)), _resident((1, d))],
        out_specs=row(d),
        scratch_shapes=[pltpu.VMEM((ROW_TILE + BLOCK, KVD_WIDTH), BF16),
                        pltpu.VMEM((ROW_TILE + BLOCK, KVD_WIDTH), BF16),
                        pltpu.VMEM((ROW_TILE, ATTN_WIDTH), F32),
                        pltpu.VMEM((N_GROUP_BLOCKS, ROW_TILE, LANES), F32),
                        pltpu.VMEM((ROW_TILE, d), BF16)],
        compiler_params=pltpu.CompilerParams(dimension_semantics=("arbitrary", "arbitrary"),
                                             vmem_limit_bytes=VMEM_LIMIT),
        name="attn_out",
    )(sinks, q, za, kk, vv, kk, vv, y2, zs, x2, gate, attn_gain, glu_w, glu_b, ssm_gain, w_out,
      final_gain)


def _pack_in_weights(w_in, b_in):
    a, kv = ATTN_WIDTH, N_KV_HEADS * HEAD_DIM
    o_k, o_v, o_za, o_u, o_zs = a, a + kv, a + 2 * kv, 2 * a + 2 * kv, 2 * a + 2 * kv + SSM_WIDTH

    def pack(m):
        dup = lambda blk: jnp.concatenate(
            [blk.reshape(-1, N_KV_HEADS, 1, HEAD_DIM)] * 2, axis=2).reshape(-1, KVD_WIDTH)
        return jnp.concatenate(
            [m[:, :a] * (HEAD_DIM ** -0.5), m[:, o_za:o_u], dup(m[:, o_k:o_v]), dup(m[:, o_v:o_za]),
             m[:, o_zs:], m[:, o_u:o_zs]], axis=1)

    return pack(w_in).astype(BF16), pack(b_in.reshape(1, -1))


def kernel(x, c, w_ada, b_ada, norm_gain, w_in, b_in, attn_sinks, attn_out_gain, ssm_lambda_re,
           ssm_lambda_im, ssm_log_step, ssm_b_re, ssm_b_im, ssm_c_re, ssm_c_im, ssm_d, glu_w, glu_b,
           ssm_out_gain, w_out, final_gain):
    bsz, seq, d = x.shape
    assert w_ada.shape[0] == 1, "single-layer trunk only"
    x2 = x.reshape(bsz * seq, d)
    mod = _ada(c, w_ada[0], b_ada[0])
    shift, scale, gate = (m.reshape(bsz, 1, d) for m in jnp.split(mod, 3, axis=-1))
    w_cat, b_cat = _pack_in_weights(w_in[0], b_in[0])
    q, za, kk, vv, zs, u2 = _inproj(x2, scale, shift, norm_gain[0].reshape(1, d), w_cat, b_cat, seq)
    prep = _ssm_prep(ssm_lambda_re[0], ssm_lambda_im[0], ssm_log_step[0], ssm_b_re[0],
                     ssm_b_im[0], ssm_c_re[0], ssm_c_im[0], ssm_d[0])
    y2 = _s5(u2, prep, bsz)
    out = _attn_out(attn_sinks[0], q, za, kk, vv, y2, zs, x2, gate,
                    attn_out_gain[0].reshape(1, -1), glu_w[0].astype(BF16),
                    glu_b[0].reshape(1, -1), ssm_out_gain[0].reshape(1, -1),
                    w_out[0].astype(BF16), final_gain.reshape(1, d), seq)
    return out.reshape(bsz, seq, d)
```

```python
import math

import jax
import jax.numpy as jnp
from jax import lax
from jax.experimental import pallas as pl
from jax.experimental.pallas import tpu as pltpu

F32 = jnp.float32
BF16 = jnp.bfloat16

D_MODEL = 2048
HEAD_DIM = 64
N_Q_HEADS = 24
N_KV_HEADS = 3
Q_PER_KV = 8
ATTN_WIDTH = N_Q_HEADS * HEAD_DIM
BLOCK = 128
SSM_GROUP = 16
SSM_WIDTH = D_MODEL - ATTN_WIDTH
N_SSM_GROUPS = SSM_WIDTH // SSM_GROUP
STATE = 64
EPS = 1e-5
NEG = -1e30

LANES = 128
SSM_CHUNK = 16
CHUNK_COLS = SSM_CHUNK * SSM_GROUP
GROUPS_PER_BLOCK = LANES // SSM_GROUP
N_GROUP_BLOCKS = N_SSM_GROUPS // GROUPS_PER_BLOCK
SLAB = SSM_CHUNK * LANES
STATE_COLS = GROUPS_PER_BLOCK * STATE
KV_DUP = 2 * HEAD_DIM
KVD_WIDTH = N_KV_HEADS * KV_DUP

VMEM_LIMIT = 56 * 1024 * 1024

ROW_TILE = 512
CHUNK_ROWS = ROW_TILE // SSM_CHUNK
ADA_COLS = 512
SSM_TILE = 512
SCAN_TILE = 128


def _resident(shape):
    return pl.BlockSpec(shape, lambda *_: (0,) * len(shape), pipeline_mode=pl.Buffered(1))


def _ada_kernel(ct_ref, w_ref, b_ref, o_ref):
    ct = ct_ref[...]
    s = ct * jax.nn.sigmoid(ct)
    w = w_ref[...]
    rows = [jnp.sum(s[:, b:b + 1] * w, axis=0, keepdims=True) for b in range(ct.shape[1])]
    o_ref[...] = jnp.concatenate(rows, axis=0) + b_ref[...]


def _ada(c, w_ada, b_ada):
    bsz, d = c.shape
    n = w_ada.shape[1]
    return pl.pallas_call(
        _ada_kernel,
        out_shape=jax.ShapeDtypeStruct((bsz, n), F32),
        grid=(n // ADA_COLS,),
        in_specs=[pl.BlockSpec((d, bsz), lambda i: (0, 0)),
                  pl.BlockSpec((d, ADA_COLS), lambda i: (0, i)),
                  pl.BlockSpec((1, ADA_COLS), lambda i: (0, i))],
        out_specs=pl.BlockSpec((bsz, ADA_COLS), lambda i: (0, i)),
        compiler_params=pltpu.CompilerParams(dimension_semantics=("arbitrary",)),
        name="ada",
    )(c.T, w_ada, b_ada.reshape(1, n))


def _inproj_kernel(x_ref, sc_ref, sh_ref, g_ref, w_ref, b_ref,
                   q_ref, za_ref, kk_ref, vv_ref, zs_ref, u2_ref, h_scr, u_scr):
    x = x_ref[...]
    var = jnp.mean(x * x, axis=-1, keepdims=True)
    h = x * lax.rsqrt(var + EPS) * g_ref[...]
    h = h * (1.0 + sc_ref[0]) + sh_ref[0]
    h_scr[...] = h.astype(BF16)

    def project(col, width):
        acc = jnp.dot(h_scr[...], w_ref[:, col:col + width], preferred_element_type=F32)
        return acc + b_ref[:, col:col + width]

    col = 0
    for ref in (q_ref, za_ref, kk_ref, vv_ref, zs_ref):
        n = ref.shape[1]
        step = 512 if n % 512 == 0 else n
        for c0 in range(0, n, step):
            ref[:, c0:c0 + step] = project(col + c0, step).astype(BF16)
        col += n
    u = project(col, SSM_WIDTH)
    for gb in range(N_GROUP_BLOCKS):
        u_scr[gb] = u[:, gb * LANES:(gb + 1) * LANES]
        for tok in range(SSM_CHUNK):
            c0 = gb * SLAB + tok * LANES
            u2_ref[:, c0:c0 + LANES] = (
                u_scr[gb, pl.ds(tok, CHUNK_ROWS, stride=SSM_CHUNK), :].astype(BF16))


def _inproj(x2, scale, shift, gain, w_cat, b_cat, rows_per_batch):
    t, d = x2.shape
    widths = (ATTN_WIDTH, ATTN_WIDTH, KVD_WIDTH, KVD_WIDTH, SSM_WIDTH)
    n_all = sum(widths) + SSM_WIDTH
    tiles_per_batch = rows_per_batch // ROW_TILE
    mod_spec = pl.BlockSpec((1, 1, d), lambda i: (i // tiles_per_batch, 0, 0))
    u2_cols = N_GROUP_BLOCKS * SLAB
    return pl.pallas_call(
        _inproj_kernel,
        out_shape=[jax.ShapeDtypeStruct((t, w), BF16) for w in widths]
        + [jax.ShapeDtypeStruct((t // SSM_CHUNK, u2_cols), BF16)],
        grid=(t // ROW_TILE,),
        in_specs=[pl.BlockSpec((ROW_TILE, d), lambda i: (i, 0)),
                  mod_spec, mod_spec,
                  _resident((1, d)),
                  _resident((d, n_all)),
                  _resident((1, n_all))],
        out_specs=[pl.BlockSpec((ROW_TILE, w), lambda i: (i, 0)) for w in widths]
        + [pl.BlockSpec((CHUNK_ROWS, u2_cols), lambda i: (i, 0))],
        scratch_shapes=[pltpu.VMEM((ROW_TILE, d), BF16),
                        pltpu.VMEM((N_GROUP_BLOCKS, ROW_TILE, LANES), F32)],
        compiler_params=pltpu.CompilerParams(dimension_semantics=("arbitrary",),
                                             vmem_limit_bytes=VMEM_LIMIT),
        name="inproj",
    )(x2, scale, shift, gain, w_cat, b_cat)


def _ssm_prep_kernel(lr_ref, li_ref, ls_ref, br_ref, bi_ref, cr_ref, ci_ref, d_ref,
                     bd_ref, vr_ref, vi_ref, wr_ref, wi_ref, a16_ref, y_scr):
    lr, li = lr_ref[...], li_ref[...]
    step = jnp.exp(ls_ref[...])
    decay = jnp.exp(lr * step)
    ar, ai = decay * jnp.cos(li * step), decay * jnp.sin(li * step)
    den = lr * lr + li * li
    nr, ni = ar - 1.0, ai
    coef_re = (nr * lr + ni * li) / den
    coef_im = (ni * lr - nr * li) / den
    br, bi = br_ref[...], bi_ref[...]
    bbar_re = coef_re * br - coef_im * bi
    bbar_im = coef_re * bi + coef_im * br
    cr, ci = cr_ref[...], ci_ref[...]
    lo = lax.broadcasted_iota(jnp.int32, lr.shape, 2) < STATE
    pr, pi = jnp.ones_like(ar), jnp.zeros_like(ai)
    for tau in range(SSM_CHUNK + 1):
        er, ei = pr * cr - pi * ci, pr * ci + pi * cr
        if tau < SSM_CHUNK:
            y_scr[:, tau * LANES:(tau + 1) * LANES, :] = jnp.where(lo, er, ei)
            s = SSM_CHUNK - 1 - tau
            vr_ref[:, s * LANES:(s + 1) * LANES, :] = pr * bbar_re - pi * bbar_im
            vi_ref[:, s * LANES:(s + 1) * LANES, :] = pr * bbar_im + pi * bbar_re
        if tau >= 1:
            wr_ref[:, (tau - 1) * LANES:tau * LANES, :] = er
            wi_ref[:, (tau - 1) * LANES:tau * LANES, :] = -ei
        if tau == SSM_CHUNK:
            a16_ref[...] = jnp.where(lo, pr, pi)
        pr, pi = pr * ar - pi * ai, pr * ai + pi * ar
    x = jnp.where(lo, bbar_re, -bbar_im)
    row = lax.broadcasted_iota(jnp.int32, (LANES, SLAB), 0)
    col = lax.broadcasted_iota(jnp.int32, (LANES, SLAB), 1)
    same_group = row // SSM_GROUP == (col % LANES) // SSM_GROUP
    diagonal = (lax.broadcasted_iota(jnp.int32, (LANES, LANES), 0)
                == lax.broadcasted_iota(jnp.int32, (LANES, LANES), 1))
    for gb in range(N_GROUP_BLOCKS):
        k = lax.dot_general(x[gb], y_scr[gb], (((1,), (1,)), ((), ())),
                            precision=lax.Precision.HIGHEST, preferred_element_type=F32)
        k = jnp.where(same_group, k, 0.0)
        bd_ref[gb, :, LANES:] = k[:, LANES:]
        bd_ref[gb, :, :LANES] = k[:, :LANES] + jnp.where(diagonal, d_ref[gb], 0.0)


def _ssm_prep(lam_re, lam_im, log_step, b_re, b_im, c_re, c_im, d_skip):
    nb = N_GROUP_BLOCKS
    rows = lambda a: jnp.concatenate([a, a], axis=-1).reshape(nb, LANES, 2 * STATE)
    per_group = lambda a: rows(jnp.repeat(a[:, None, :], SSM_GROUP, axis=1))
    compact = jax.ShapeDtypeStruct((nb, SLAB, 2 * STATE), F32)
    return pl.pallas_call(
        _ssm_prep_kernel,
        out_shape=[jax.ShapeDtypeStruct((nb, LANES, SLAB), F32),
                   compact, compact,
                   compact, compact,
                   jax.ShapeDtypeStruct((nb, LANES, 2 * STATE), F32)],
        scratch_shapes=[pltpu.VMEM((nb, SLAB, 2 * STATE), F32)],
        compiler_params=pltpu.CompilerParams(vmem_limit_bytes=VMEM_LIMIT),
        name="ssm_prep",
    )(per_group(lam_re), per_group(lam_im),
      jnp.broadcast_to(jnp.repeat(log_step, SSM_GROUP).reshape(nb, LANES, 1), (nb, LANES, 2 * STATE)),
      rows(jnp.swapaxes(b_re, 1, 2)), rows(jnp.swapaxes(b_im, 1, 2)), rows(c_re), rows(c_im),
      jnp.broadcast_to(d_skip.reshape(nb, 1, LANES), (nb, LANES, LANES)))


def _ssm_expand_kernel(bd_ref, vr_ref, vi_ref, wr_ref, wi_ref, t_ref, v_ref, wt_ref):
    bd = bd_ref[0].astype(BF16)
    for s in range(SSM_CHUNK):
        rows = slice(s * LANES, (s + 1) * LANES)
        if s:
            t_ref[0, rows, :s * LANES] = jnp.zeros((LANES, s * LANES), BF16)
        t_ref[0, rows, s * LANES:] = bd[:, :(SSM_CHUNK - s) * LANES]
    shape = (SLAB, 2 * STATE)
    row_group = (lax.broadcasted_iota(jnp.int32, shape, 0) % LANES) // SSM_GROUP
    half = lax.broadcasted_iota(jnp.int32, shape, 1) // STATE
    for j in range(STATE_COLS // LANES):
        own = row_group == 2 * j + half
        for src, dst, c0 in ((vr_ref, v_ref, 0), (vi_ref, v_ref, STATE_COLS),
                             (wr_ref, wt_ref, 0), (wi_ref, wt_ref, STATE_COLS)):
            dst[0, :, c0 + j * LANES:c0 + (j + 1) * LANES] = (
                jnp.where(own, src[0], 0.0).astype(BF16))


def _chunk_matrices(prep):
    nb = N_GROUP_BLOCKS
    per_block = lambda shape: pl.BlockSpec((1,) + shape, lambda gb: (gb, 0, 0))
    return pl.pallas_call(
        _ssm_expand_kernel,
        out_shape=[jax.ShapeDtypeStruct((nb, SLAB, SLAB), BF16),
                   jax.ShapeDtypeStruct((nb, SLAB, 2 * STATE_COLS), BF16),
                   jax.ShapeDtypeStruct((nb, SLAB, 2 * STATE_COLS), BF16)],
        grid=(nb,),
        in_specs=[per_block((LANES, SLAB))] + [per_block((SLAB, 2 * STATE))] * 4,
        out_specs=[per_block((SLAB, SLAB)), per_block((SLAB, 2 * STATE_COLS)),
                   per_block((SLAB, 2 * STATE_COLS))],
        compiler_params=pltpu.CompilerParams(dimension_semantics=("arbitrary",),
                                             vmem_limit_bytes=VMEM_LIMIT),
        name="ssm_expand",
    )(*prep[:5])


def _ssm_state_kernel(u_ref, v_ref, sr_ref, si_ref):
    s = jnp.dot(u_ref[...], v_ref[0], preferred_element_type=F32)
    sr_ref[...] = s[:, :STATE_COLS]
    si_ref[...] = s[:, STATE_COLS:]


def _ssm_state(u2, v_blk):
    n = u2.shape[0]
    state = pl.BlockSpec((SSM_TILE, STATE_COLS), lambda gb, i: (i, gb))
    return pl.pallas_call(
        _ssm_state_kernel,
        out_shape=[jax.ShapeDtypeStruct((n, N_GROUP_BLOCKS * STATE_COLS), F32)] * 2,
        grid=(N_GROUP_BLOCKS, n // SSM_TILE),
        in_specs=[pl.BlockSpec((SSM_TILE, SLAB), lambda gb, i: (i, gb)),
                  pl.BlockSpec((1, SLAB, 2 * STATE_COLS), lambda gb, i: (gb, 0, 0))],
        out_specs=[state, state],
        compiler_params=pltpu.CompilerParams(dimension_semantics=("arbitrary", "arbitrary"),
                                             vmem_limit_bytes=VMEM_LIMIT),
        name="ssm_state",
    )(u2, v_blk)


def _ssm_scan_kernel(sr_ref, si_ref, ar_ref, ai_ref, hr_ref, hi_ref, cr_scr, ci_scr):
    @pl.when(pl.program_id(0) == 0)
    def _():
        cr_scr[...] = jnp.zeros_like(cr_scr)
        ci_scr[...] = jnp.zeros_like(ci_scr)

    ar, ai = ar_ref[...], ai_ref[...]
    nb = sr_ref.shape[0]

    def body(k, carry):
        out = []
        for b in range(nb):
            hr, hi = carry[2 * b], carry[2 * b + 1]
            hr_ref[b, k] = hr
            hi_ref[b, k] = hi
            out += [ar * hr - ai * hi + sr_ref[b, k], ar * hi + ai * hr + si_ref[b, k]]
        return tuple(out)

    init = tuple(scr[b] for b in range(nb) for scr in (cr_scr, ci_scr))
    fin = lax.fori_loop(0, SCAN_TILE, body, init, unroll=8)
    for b in range(nb):
        cr_scr[b] = fin[2 * b]
        ci_scr[b] = fin[2 * b + 1]


def _ssm_scan(s_re, s_im, a_re, a_im):
    bsz, k, rows, lanes = s_re.shape
    blk = pl.BlockSpec((bsz, SCAN_TILE, rows, lanes), lambda i: (0, i, 0, 0))
    return pl.pallas_call(
        _ssm_scan_kernel,
        out_shape=[jax.ShapeDtypeStruct(s_re.shape, F32)] * 2,
        grid=(k // SCAN_TILE,),
        in_specs=[blk, blk, _resident((rows, lanes)), _resident((rows, lanes))],
        out_specs=[blk, blk],
        scratch_shapes=[pltpu.VMEM((bsz, rows, lanes), F32)] * 2,
        compiler_params=pltpu.CompilerParams(dimension_semantics=("arbitrary",)),
        name="ssm_scan",
    )(s_re, s_im, a_re, a_im)


def _ssm_out_kernel(u_ref, hr_ref, hi_ref, t_ref, wt_ref, y_ref):
    pair = 2 * LANES
    contract_last = (((1,), (1,)), ((), ()))
    for tb in range(SLAB // pair):
        k_rows = (tb + 1) * pair
        cols = slice(tb * pair, (tb + 1) * pair)
        y = jnp.dot(u_ref[:, :k_rows], t_ref[0, :k_rows, cols], preferred_element_type=F32)
        y += lax.dot_general(hr_ref[...], wt_ref[0, cols, :STATE_COLS], contract_last,
                             preferred_element_type=F32)
        y += lax.dot_general(hi_ref[...], wt_ref[0, cols, STATE_COLS:], contract_last,
                             preferred_element_type=F32)
        y_ref[:, cols] = y


def _ssm_out(u2, h_re, h_im, t_mat, wt_blk):
    n = u2.shape[0]
    state = pl.BlockSpec((SSM_TILE, STATE_COLS), lambda gb, i: (i, gb))
    slab = pl.BlockSpec((SSM_TILE, SLAB), lambda gb, i: (i, gb))
    return pl.pallas_call(
        _ssm_out_kernel,
        out_shape=jax.ShapeDtypeStruct(u2.shape, F32),
        grid=(N_GROUP_BLOCKS, n // SSM_TILE),
        in_specs=[slab, state, state,
                  pl.BlockSpec((1, SLAB, SLAB), lambda gb, i: (gb, 0, 0)),
                  pl.BlockSpec((1, SLAB, 2 * STATE_COLS), lambda gb, i: (gb, 0, 0))],
        out_specs=slab,
        compiler_params=pltpu.CompilerParams(dimension_semantics=("arbitrary", "arbitrary"),
                                             vmem_limit_bytes=VMEM_LIMIT),
        name="ssm_out",
    )(u2, h_re, h_im, t_mat, wt_blk)


def _s5(u2, prep, bsz):
    t_mat, v_blk, wt_blk = _chunk_matrices(prep)
    a16 = prep[5].reshape(N_SSM_GROUPS, SSM_GROUP, 2 * STATE)[:, 0]
    n = u2.shape[0]
    kc = n // bsz
    s_re, s_im = _ssm_state(u2, v_blk)
    scan_shape = (bsz, kc, N_SSM_GROUPS * STATE // LANES, LANES)
    a_re = a16[:, :STATE].reshape(scan_shape[2:])
    a_im = a16[:, STATE:].reshape(scan_shape[2:])
    h_re, h_im = _ssm_scan(s_re.reshape(scan_shape), s_im.reshape(scan_shape), a_re, a_im)
    h_re = h_re.reshape(n, -1).astype(BF16)
    h_im = h_im.reshape(n, -1).astype(BF16)
    return _ssm_out(u2, h_re, h_im, t_mat, wt_blk)


def _attn_out_kernel(sink_ref, q_ref, za_ref, kk_ref, vv_ref, kkp_ref, vvp_ref, y2_ref, zs_ref,
                     x_ref, gate_ref, ag_ref, gw_ref, gb_ref, sg_ref, wo_ref, fg_ref,
                     o_ref, kk_scr, vv_scr, attn_scr, y_scr, mix_scr):
    first = pl.program_id(1) == 0
    keep = jnp.where(first, 0.0, 1.0).astype(BF16)
    kk_scr[0:BLOCK, :] = kkp_ref[...] * keep
    vv_scr[0:BLOCK, :] = vvp_ref[...] * keep
    kk_scr[BLOCK:, :] = kk_ref[...]
    vv_scr[BLOCK:, :] = vv_ref[...]

    qi = lax.broadcasted_iota(jnp.int32, (BLOCK, 2 * BLOCK), 0)
    kj = lax.broadcasted_iota(jnp.int32, (BLOCK, 2 * BLOCK), 1)
    in_band = (kj > qi) & (kj <= qi + BLOCK)
    bias_any = jnp.where(in_band, 0.0, NEG).astype(F32)
    bias_first = jnp.where(in_band & (kj >= BLOCK), 0.0, NEG).astype(F32)
    lane_lo_kv = lax.broadcasted_iota(jnp.int32, (2 * BLOCK, KV_DUP), 1) < HEAD_DIM
    lane_lo_o = lax.broadcasted_iota(jnp.int32, (BLOCK, KV_DUP), 1) < HEAD_DIM
    pairs = Q_PER_KV // 2
    contract_last = (((1,), (1,)), ((), ()))

    def block(j, carry):
        r0 = pl.multiple_of(j * BLOCK, BLOCK)
        bias = jnp.where(first & (j == 0), bias_first, bias_any)
        for h in range(N_KV_HEADS):
            kb = kk_scr[pl.ds(r0, 2 * BLOCK), h * KV_DUP:(h + 1) * KV_DUP]
            vb = vv_scr[pl.ds(r0, 2 * BLOCK), h * KV_DUP:(h + 1) * KV_DUP]
            zero = jnp.zeros_like(kb)
            k_par = (jnp.where(lane_lo_kv, kb, zero), jnp.where(lane_lo_kv, zero, kb))
            v_par = (jnp.where(lane_lo_kv, vb, zero), jnp.where(lane_lo_kv, zero, vb))
            q0 = h * Q_PER_KV * HEAD_DIM
            qs = jnp.concatenate(
                [q_ref[pl.ds(r0, BLOCK), q0 + p * KV_DUP:q0 + (p + 1) * KV_DUP] for p in range(pairs)],
                axis=0)
            acc = None
            inv = [[None, None] for _ in range(pairs)]
            for par in range(2):
                s_all = lax.dot_general(qs, k_par[par], contract_last, preferred_element_type=F32)
                probs = []
                for p in range(pairs):
                    s = s_all[p * BLOCK:(p + 1) * BLOCK] + bias
                    sink = sink_ref[h * Q_PER_KV + 2 * p + par]
                    m = jnp.maximum(jnp.max(s, axis=-1, keepdims=True), sink)
                    e = jnp.exp(s - m)
                    l = jnp.sum(e, axis=-1, keepdims=True) + jnp.exp(sink - m)
                    inv[p][par] = 1.0 / l
                    probs.append(e.astype(BF16))
                pv = jnp.dot(jnp.concatenate(probs, axis=0), v_par[par], preferred_element_type=F32)
                acc = pv if acc is None else acc + pv
            for p in range(pairs):
                o = acc[p * BLOCK:(p + 1) * BLOCK] * jnp.where(lane_lo_o, inv[p][0], inv[p][1])
                attn_scr[pl.ds(r0, BLOCK), q0 + p * KV_DUP:q0 + (p + 1) * KV_DUP] = o
        return carry

    lax.fori_loop(0, q_ref.shape[0] // BLOCK, block, 0)

    a = attn_scr[...]
    a = a * lax.rsqrt(jnp.mean(a * a, axis=-1, keepdims=True) + EPS) * ag_ref[...]
    za = za_ref[...].astype(F32)
    mix_scr[:, :ATTN_WIDTH] = (a * (za * jax.nn.sigmoid(za))).astype(BF16)
    for gb in range(N_GROUP_BLOCKS):
        for tok in range(SSM_CHUNK):
            c0 = gb * SLAB + tok * LANES
            y_scr[gb, pl.ds(tok, CHUNK_ROWS, stride=SSM_CHUNK), :] = y2_ref[:, c0:c0 + LANES]
    y = jnp.concatenate([y_scr[gb] for gb in range(N_GROUP_BLOCKS)], axis=1)
    y = 0.5 * y * (1.0 + lax.erf(y * math.sqrt(0.5)))
    glu = jnp.dot(y.astype(BF16), gw_ref[...], preferred_element_type=F32) + gb_ref[...]
    y = y * jax.nn.sigmoid(glu)
    y = y * lax.rsqrt(jnp.mean(y * y, axis=-1, keepdims=True) + EPS) * sg_ref[...]
    zs = zs_ref[...].astype(F32)
    mix_scr[:, ATTN_WIDTH:] = (y * (zs * jax.nn.sigmoid(zs))).astype(BF16)
    out = jnp.dot(mix_scr[...], wo_ref[...], preferred_element_type=F32)
    res = x_ref[...] + gate_ref[0] * out
    o_ref[...] = res * lax.rsqrt(jnp.mean(res * res, axis=-1, keepdims=True) + EPS) * fg_ref[...]


def _attn_out(sinks, q, za, kk, vv, y2, zs, x2, gate, attn_gain, glu_w, glu_b, ssm_gain, w_out,
              final_gain, rows_per_batch):
    t, d = x2.shape
    tiles = rows_per_batch // ROW_TILE
    blocks_per_tile = ROW_TILE // BLOCK
    row = lambda w: pl.BlockSpec((ROW_TILE, w), lambda b, i: (b * tiles + i, 0))
    prev = pl.BlockSpec(
        (BLOCK, KVD_WIDTH),
        lambda b, i: (jnp.maximum((b * tiles + i) * blocks_per_tile - 1, 0), 0))
    return pl.pallas_call(
        _attn_out_kernel,
        out_shape=jax.ShapeDtypeStruct((t, d), F32),
        grid=(t // rows_per_batch, tiles),
        in_specs=[pl.BlockSpec(memory_space=pltpu.SMEM),
                  row(ATTN_WIDTH), row(ATTN_WIDTH), row(KVD_WIDTH), row(KVD_WIDTH), prev, prev,
                  pl.BlockSpec((CHUNK_ROWS, y2.shape[1]), lambda b, i: (b * tiles + i, 0)),
                  row(SSM_WIDTH), row(d),
                  pl.BlockSpec((1, 1, d), lambda b, i: (b, 0, 0)),
                  _resident((1, ATTN_WIDTH)),
                  _resident((SSM_WIDTH, SSM_WIDTH)), _resident((1, SSM_WIDTH)),
                  _resident((1, SSM_WIDTH)),
                  _resident((d, d)), _resident((1, d))],
        out_specs=row(d),
        scratch_shapes=[pltpu.VMEM((ROW_TILE + BLOCK, KVD_WIDTH), BF16),
                        pltpu.VMEM((ROW_TILE + BLOCK, KVD_WIDTH), BF16),
                        pltpu.VMEM((ROW_TILE, ATTN_WIDTH), F32),
                        pltpu.VMEM((N_GROUP_BLOCKS, ROW_TILE, LANES), F32),
                        pltpu.VMEM((ROW_TILE, d), BF16)],
        compiler_params=pltpu.CompilerParams(dimension_semantics=("arbitrary", "arbitrary"),
                                             vmem_limit_bytes=VMEM_LIMIT),
        name="attn_out",
    )(sinks, q, za, kk, vv, kk, vv, y2, zs, x2, gate, attn_gain, glu_w, glu_b, ssm_gain, w_out,
      final_gain)


def _pack_in_weights(w_in, b_in):
    a, kv = ATTN_WIDTH, N_KV_HEADS * HEAD_DIM
    o_k, o_v, o_za, o_u, o_zs = a, a + kv, a + 2 * kv, 2 * a + 2 * kv, 2 * a + 2 * kv + SSM_WIDTH

    def pack(m):
        dup = lambda blk: jnp.concatenate(
            [blk.reshape(-1, N_KV_HEADS, 1, HEAD_DIM)] * 2, axis=2).reshape(-1, KVD_WIDTH)
        return jnp.concatenate(
            [m[:, :a] * (HEAD_DIM ** -0.5), m[:, o_za:o_u], dup(m[:, o_k:o_v]), dup(m[:, o_v:o_za]),
             m[:, o_zs:], m[:, o_u:o_zs]], axis=1)

    return pack(w_in).astype(BF16), pack(b_in.reshape(1, -1))


def kernel(x, c, w_ada, b_ada, norm_gain, w_in, b_in, attn_sinks, attn_out_gain, ssm_lambda_re,
           ssm_lambda_im, ssm_log_step, ssm_b_re, ssm_b_im, ssm_c_re, ssm_c_im, ssm_d, glu_w, glu_b,
           ssm_out_gain, w_out, final_gain):
    bsz, seq, d = x.shape
    assert w_ada.shape[0] == 1, "single-layer trunk only"
    x2 = x.reshape(bsz * seq, d)
    mod = _ada(c, w_ada[0], b_ada[0])
    shift, scale, gate = (m.reshape(bsz, 1, d) for m in jnp.split(mod, 3, axis=-1))
    w_cat, b_cat = _pack_in_weights(w_in[0], b_in[0])
    q, za, kk, vv, zs, u2 = _inproj(x2, scale, shift, norm_gain[0].reshape(1, d), w_cat, b_cat, seq)
    prep = _ssm_prep(ssm_lambda_re[0], ssm_lambda_im[0], ssm_log_step[0], ssm_b_re[0],
                     ssm_b_im[0], ssm_c_re[0], ssm_c_im[0], ssm_d[0])
    y2 = _s5(u2, prep, bsz)
    out = _attn_out(attn_sinks[0], q, za, kk, vv, y2, zs, x2, gate,
                    attn_out_gain[0].reshape(1, -1), glu_w[0].astype(BF16),
                    glu_b[0].reshape(1, -1), ssm_out_gain[0].reshape(1, -1),
                    w_out[0].astype(BF16), final_gain.reshape(1, d), seq)
    return out.reshape(bsz, seq, d)
```

```python
import functools
import math

import jax
import jax.numpy as jnp
from jax import lax
from jax.experimental import pallas as pl
from jax.experimental.pallas import tpu as pltpu

F32 = jnp.float32
BF16 = jnp.bfloat16

D_MODEL = 2048
HEAD_DIM = 64
N_Q_HEADS = 24
N_KV_HEADS = 3
Q_PER_KV = 8
ATTN_WIDTH = N_Q_HEADS * HEAD_DIM
BLOCK = 128
SSM_GROUP = 16
SSM_WIDTH = D_MODEL - ATTN_WIDTH
N_SSM_GROUPS = SSM_WIDTH // SSM_GROUP
STATE = 64
EPS = 1e-5
NEG = -1e30
LOG2_E = math.log2(math.e)

LANES = 128
SSM_CHUNK = 16
CHUNK_COLS = SSM_CHUNK * SSM_GROUP
GROUPS_PER_BLOCK = LANES // SSM_GROUP
N_GROUP_BLOCKS = N_SSM_GROUPS // GROUPS_PER_BLOCK
SLAB = SSM_CHUNK * LANES
STATE_COLS = GROUPS_PER_BLOCK * STATE
KV_DUP = 2 * HEAD_DIM
KVD_WIDTH = N_KV_HEADS * KV_DUP

VMEM_LIMIT = 56 * 1024 * 1024

ROW_TILE = 512
CHUNK_ROWS = ROW_TILE // SSM_CHUNK
ADA_COLS = 512
OUT_CHUNK = D_MODEL // (ROW_TILE // BLOCK)
SSM_TILE = 512
SCAN_TILE = 128


def _resident(shape):
    return pl.BlockSpec(shape, lambda *_: (0,) * len(shape), pipeline_mode=pl.Buffered(1))


def _ada_kernel(ct_ref, w_ref, b_ref, o_ref):
    ct = ct_ref[...]
    s = ct * jax.nn.sigmoid(ct)
    w = w_ref[...]
    rows = [jnp.sum(s[:, b:b + 1] * w, axis=0, keepdims=True) for b in range(ct.shape[1])]
    o_ref[...] = jnp.concatenate(rows, axis=0) + b_ref[...]


def _ada(c, w_ada, b_ada):
    bsz, d = c.shape
    n = w_ada.shape[1]
    return pl.pallas_call(
        _ada_kernel,
        out_shape=jax.ShapeDtypeStruct((bsz, n), F32),
        grid=(n // ADA_COLS,),
        in_specs=[pl.BlockSpec((d, bsz), lambda i: (0, 0)),
                  pl.BlockSpec((d, ADA_COLS), lambda i: (0, i)),
                  pl.BlockSpec((1, ADA_COLS), lambda i: (0, i))],
        out_specs=pl.BlockSpec((bsz, ADA_COLS), lambda i: (0, i)),
        compiler_params=pltpu.CompilerParams(dimension_semantics=("arbitrary",)),
        name="ada",
    )(c.T, w_ada, b_ada.reshape(1, n))


def _inproj_kernel(x_ref, sc_ref, sh_ref, g_ref, w_ref, b_ref,
                   q_ref, za_ref, kk_ref, vv_ref, zs_ref, u2_ref, h_scr, u_scr):
    x = x_ref[...]
    var = jnp.mean(x * x, axis=-1, keepdims=True)
    h = x * lax.rsqrt(var + EPS) * g_ref[...]
    h = h * (1.0 + sc_ref[0]) + sh_ref[0]
    h_scr[...] = h.astype(BF16)

    def project(col, width):
        acc = jnp.dot(h_scr[...], w_ref[:, col:col + width], preferred_element_type=F32)
        return acc + b_ref[:, col:col + width]

    col = 0
    for ref in (q_ref, za_ref, kk_ref, vv_ref, zs_ref):
        n = ref.shape[1]
        step = 512 if n % 512 == 0 else n
        for c0 in range(0, n, step):
            ref[:, c0:c0 + step] = project(col + c0, step).astype(BF16)
        col += n
    u = project(col, SSM_WIDTH)
    for gb in range(N_GROUP_BLOCKS):
        u_scr[gb] = u[:, gb * LANES:(gb + 1) * LANES]
        for tok in range(SSM_CHUNK):
            c0 = gb * SLAB + tok * LANES
            u2_ref[:, c0:c0 + LANES] = (
                u_scr[gb, pl.ds(tok, CHUNK_ROWS, stride=SSM_CHUNK), :].astype(BF16))


def _inproj(x2, scale, shift, gain, w_cat, b_cat, rows_per_batch):
    t, d = x2.shape
    widths = (ATTN_WIDTH, ATTN_WIDTH, KVD_WIDTH, KVD_WIDTH, SSM_WIDTH)
    n_all = sum(widths) + SSM_WIDTH
    tiles_per_batch = rows_per_batch // ROW_TILE
    mod_spec = pl.BlockSpec((1, 1, d), lambda i: (i // tiles_per_batch, 0, 0))
    u2_cols = N_GROUP_BLOCKS * SLAB
    return pl.pallas_call(
        _inproj_kernel,
        out_shape=[jax.ShapeDtypeStruct((t, w), BF16) for w in widths]
        + [jax.ShapeDtypeStruct((t // SSM_CHUNK, u2_cols), BF16)],
        grid=(t // ROW_TILE,),
        in_specs=[pl.BlockSpec((ROW_TILE, d), lambda i: (i, 0)),
                  mod_spec, mod_spec,
                  _resident((1, d)),
                  _resident((d, n_all)),
                  _resident((1, n_all))],
        out_specs=[pl.BlockSpec((ROW_TILE, w), lambda i: (i, 0)) for w in widths]
        + [pl.BlockSpec((CHUNK_ROWS, u2_cols), lambda i: (i, 0))],
        scratch_shapes=[pltpu.VMEM((ROW_TILE, d), BF16),
                        pltpu.VMEM((N_GROUP_BLOCKS, ROW_TILE, LANES), F32)],
        compiler_params=pltpu.CompilerParams(dimension_semantics=("arbitrary",),
                                             vmem_limit_bytes=VMEM_LIMIT),
        name="inproj",
    )(x2, scale, shift, gain, w_cat, b_cat)


def _ssm_prep_kernel(lr_ref, li_ref, ls_ref, br_ref, bi_ref, cr_ref, ci_ref, d_ref,
                     bd_ref, vr_ref, vi_ref, wr_ref, wi_ref, a16_ref, y_scr):
    lr, li = lr_ref[...], li_ref[...]
    step = jnp.exp(ls_ref[...])
    decay = jnp.exp(lr * step)
    ar, ai = decay * jnp.cos(li * step), decay * jnp.sin(li * step)
    den = lr * lr + li * li
    nr, ni = ar - 1.0, ai
    coef_re = (nr * lr + ni * li) / den
    coef_im = (ni * lr - nr * li) / den
    br, bi = br_ref[...], bi_ref[...]
    bbar_re = coef_re * br - coef_im * bi
    bbar_im = coef_re * bi + coef_im * br
    cr, ci = cr_ref[...], ci_ref[...]
    lo = lax.broadcasted_iota(jnp.int32, lr.shape, 2) < STATE
    pr, pi = jnp.ones_like(ar), jnp.zeros_like(ai)
    for tau in range(SSM_CHUNK + 1):
        er, ei = pr * cr - pi * ci, pr * ci + pi * cr
        if tau < SSM_CHUNK:
            y_scr[:, tau * LANES:(tau + 1) * LANES, :] = jnp.where(lo, er, ei)
            s = SSM_CHUNK - 1 - tau
            vr_ref[:, s * LANES:(s + 1) * LANES, :] = pr * bbar_re - pi * bbar_im
            vi_ref[:, s * LANES:(s + 1) * LANES, :] = pr * bbar_im + pi * bbar_re
        if tau >= 1:
            wr_ref[:, (tau - 1) * LANES:tau * LANES, :] = er
            wi_ref[:, (tau - 1) * LANES:tau * LANES, :] = -ei
        if tau == SSM_CHUNK:
            a16_ref[...] = jnp.where(lo, pr, pi)
        pr, pi = pr * ar - pi * ai, pr * ai + pi * ar
    x = jnp.where(lo, bbar_re, -bbar_im)
    row = lax.broadcasted_iota(jnp.int32, (LANES, SLAB), 0)
    col = lax.broadcasted_iota(jnp.int32, (LANES, SLAB), 1)
    same_group = row // SSM_GROUP == (col % LANES) // SSM_GROUP
    diagonal = (lax.broadcasted_iota(jnp.int32, (LANES, LANES), 0)
                == lax.broadcasted_iota(jnp.int32, (LANES, LANES), 1))
    for gb in range(N_GROUP_BLOCKS):
        k = lax.dot_general(x[gb], y_scr[gb], (((1,), (1,)), ((), ())),
                            precision=lax.Precision.HIGHEST, preferred_element_type=F32)
        k = jnp.where(same_group, k, 0.0)
        bd_ref[gb, :, LANES:] = k[:, LANES:]
        bd_ref[gb, :, :LANES] = k[:, :LANES] + jnp.where(diagonal, d_ref[gb], 0.0)


def _ssm_prep(lam_re, lam_im, log_step, b_re, b_im, c_re, c_im, d_skip):
    nb = N_GROUP_BLOCKS
    rows = lambda a: jnp.concatenate([a, a], axis=-1).reshape(nb, LANES, 2 * STATE)
    per_group = lambda a: rows(jnp.repeat(a[:, None, :], SSM_GROUP, axis=1))
    compact = jax.ShapeDtypeStruct((nb, SLAB, 2 * STATE), F32)
    return pl.pallas_call(
        _ssm_prep_kernel,
        out_shape=[jax.ShapeDtypeStruct((nb, LANES, SLAB), F32),
                   compact, compact,
                   compact, compact,
                   jax.ShapeDtypeStruct((nb, LANES, 2 * STATE), F32)],
        scratch_shapes=[pltpu.VMEM((nb, SLAB, 2 * STATE), F32)],
        compiler_params=pltpu.CompilerParams(vmem_limit_bytes=VMEM_LIMIT),
        name="ssm_prep",
    )(per_group(lam_re), per_group(lam_im),
      jnp.broadcast_to(jnp.repeat(log_step, SSM_GROUP).reshape(nb, LANES, 1), (nb, LANES, 2 * STATE)),
      rows(jnp.swapaxes(b_re, 1, 2)), rows(jnp.swapaxes(b_im, 1, 2)), rows(c_re), rows(c_im),
      jnp.broadcast_to(d_skip.reshape(nb, 1, LANES), (nb, LANES, LANES)))


def _ssm_expand_kernel(bd_ref, vr_ref, vi_ref, wr_ref, wi_ref, t_ref, v_ref, wt_ref):
    bd = bd_ref[0].astype(BF16)
    for s in range(SSM_CHUNK):
        rows = slice(s * LANES, (s + 1) * LANES)
        if s:
            t_ref[0, rows, :s * LANES] = jnp.zeros((LANES, s * LANES), BF16)
        t_ref[0, rows, s * LANES:] = bd[:, :(SSM_CHUNK - s) * LANES]
    shape = (SLAB, 2 * STATE)
    row_group = (lax.broadcasted_iota(jnp.int32, shape, 0) % LANES) // SSM_GROUP
    half = lax.broadcasted_iota(jnp.int32, shape, 1) // STATE
    for j in range(STATE_COLS // LANES):
        own = row_group == 2 * j + half
        for src, dst, c0 in ((vr_ref, v_ref, 0), (vi_ref, v_ref, STATE_COLS),
                             (wr_ref, wt_ref, 0), (wi_ref, wt_ref, STATE_COLS)):
            dst[0, :, c0 + j * LANES:c0 + (j + 1) * LANES] = (
                jnp.where(own, src[0], 0.0).astype(BF16))


def _chunk_matrices(prep):
    nb = N_GROUP_BLOCKS
    per_block = lambda shape: pl.BlockSpec((1,) + shape, lambda gb: (gb, 0, 0))
    return pl.pallas_call(
        _ssm_expand_kernel,
        out_shape=[jax.ShapeDtypeStruct((nb, SLAB, SLAB), BF16),
                   jax.ShapeDtypeStruct((nb, SLAB, 2 * STATE_COLS), BF16),
                   jax.ShapeDtypeStruct((nb, SLAB, 2 * STATE_COLS), BF16)],
        grid=(nb,),
        in_specs=[per_block((LANES, SLAB))] + [per_block((SLAB, 2 * STATE))] * 4,
        out_specs=[per_block((SLAB, SLAB)), per_block((SLAB, 2 * STATE_COLS)),
                   per_block((SLAB, 2 * STATE_COLS))],
        compiler_params=pltpu.CompilerParams(dimension_semantics=("arbitrary",),
                                             vmem_limit_bytes=VMEM_LIMIT),
        name="ssm_expand",
    )(*prep[:5])


def _ssm_state_kernel(u_ref, v_ref, sr_ref, si_ref):
    s = jnp.dot(u_ref[...], v_ref[0], preferred_element_type=F32)
    sr_ref[...] = s[:, :STATE_COLS]
    si_ref[...] = s[:, STATE_COLS:]


def _ssm_state(u2, v_blk):
    n = u2.shape[0]
    state = pl.BlockSpec((SSM_TILE, STATE_COLS), lambda gb, i: (i, gb))
    return pl.pallas_call(
        _ssm_state_kernel,
        out_shape=[jax.ShapeDtypeStruct((n, N_GROUP_BLOCKS * STATE_COLS), F32)] * 2,
        grid=(N_GROUP_BLOCKS, n // SSM_TILE),
        in_specs=[pl.BlockSpec((SSM_TILE, SLAB), lambda gb, i: (i, gb)),
                  pl.BlockSpec((1, SLAB, 2 * STATE_COLS), lambda gb, i: (gb, 0, 0))],
        out_specs=[state, state],
        compiler_params=pltpu.CompilerParams(dimension_semantics=("arbitrary", "arbitrary"),
                                             vmem_limit_bytes=VMEM_LIMIT),
        name="ssm_state",
    )(u2, v_blk)


def _ssm_scan_kernel(sr_ref, si_ref, ar_ref, ai_ref, hr_ref, hi_ref, cr_scr, ci_scr):
    @pl.when(pl.program_id(0) == 0)
    def _():
        cr_scr[...] = jnp.zeros_like(cr_scr)
        ci_scr[...] = jnp.zeros_like(ci_scr)

    ar, ai = ar_ref[...], ai_ref[...]
    nb = sr_ref.shape[0]

    def body(k, carry):
        out = []
        for b in range(nb):
            hr, hi = carry[2 * b], carry[2 * b + 1]
            hr_ref[b, k] = hr
            hi_ref[b, k] = hi
            out += [ar * hr - ai * hi + sr_ref[b, k], ar * hi + ai * hr + si_ref[b, k]]
        return tuple(out)

    init = tuple(scr[b] for b in range(nb) for scr in (cr_scr, ci_scr))
    fin = lax.fori_loop(0, SCAN_TILE, body, init, unroll=8)
    for b in range(nb):
        cr_scr[b] = fin[2 * b]
        ci_scr[b] = fin[2 * b + 1]


def _ssm_scan(s_re, s_im, a_re, a_im):
    bsz, k, rows, lanes = s_re.shape
    blk = pl.BlockSpec((bsz, SCAN_TILE, rows, lanes), lambda i: (0, i, 0, 0))
    return pl.pallas_call(
        _ssm_scan_kernel,
        out_shape=[jax.ShapeDtypeStruct(s_re.shape, F32)] * 2,
        grid=(k // SCAN_TILE,),
        in_specs=[blk, blk, _resident((rows, lanes)), _resident((rows, lanes))],
        out_specs=[blk, blk],
        scratch_shapes=[pltpu.VMEM((bsz, rows, lanes), F32)] * 2,
        compiler_params=pltpu.CompilerParams(dimension_semantics=("arbitrary",)),
        name="ssm_scan",
    )(s_re, s_im, a_re, a_im)


def _ssm_out_kernel(u_ref, hr_ref, hi_ref, t_ref, wt_ref, y_ref):
    pair = 2 * LANES
    contract_last = (((1,), (1,)), ((), ()))
    for tb in range(SLAB // pair):
        k_rows = (tb + 1) * pair
        cols = slice(tb * pair, (tb + 1) * pair)
        y = jnp.dot(u_ref[:, :k_rows], t_ref[0, :k_rows, cols], preferred_element_type=F32)
        y += lax.dot_general(hr_ref[...], wt_ref[0, cols, :STATE_COLS], contract_last,
                             preferred_element_type=F32)
        y += lax.dot_general(hi_ref[...], wt_ref[0, cols, STATE_COLS:], contract_last,
                             preferred_element_type=F32)
        y_ref[:, cols] = y


def _ssm_out(u2, h_re, h_im, t_mat, wt_blk):
    n = u2.shape[0]
    state = pl.BlockSpec((SSM_TILE, STATE_COLS), lambda gb, i: (i, gb))
    slab = pl.BlockSpec((SSM_TILE, SLAB), lambda gb, i: (i, gb))
    return pl.pallas_call(
        _ssm_out_kernel,
        out_shape=jax.ShapeDtypeStruct(u2.shape, F32),
        grid=(N_GROUP_BLOCKS, n // SSM_TILE),
        in_specs=[slab, state, state,
                  pl.BlockSpec((1, SLAB, SLAB), lambda gb, i: (gb, 0, 0)),
                  pl.BlockSpec((1, SLAB, 2 * STATE_COLS), lambda gb, i: (gb, 0, 0))],
        out_specs=slab,
        compiler_params=pltpu.CompilerParams(dimension_semantics=("arbitrary", "arbitrary"),
                                             vmem_limit_bytes=VMEM_LIMIT),
        name="ssm_out",
    )(u2, h_re, h_im, t_mat, wt_blk)


def _s5(u2, prep, bsz):
    t_mat, v_blk, wt_blk = _chunk_matrices(prep)
    a16 = prep[5].reshape(N_SSM_GROUPS, SSM_GROUP, 2 * STATE)[:, 0]
    n = u2.shape[0]
    kc = n // bsz
    s_re, s_im = _ssm_state(u2, v_blk)
    scan_shape = (bsz, kc, N_SSM_GROUPS * STATE // LANES, LANES)
    a_re = a16[:, :STATE].reshape(scan_shape[2:])
    a_im = a16[:, STATE:].reshape(scan_shape[2:])
    h_re, h_im = _ssm_scan(s_re.reshape(scan_shape), s_im.reshape(scan_shape), a_re, a_im)
    h_re = h_re.reshape(n, -1).astype(BF16)
    h_im = h_im.reshape(n, -1).astype(BF16)
    return _ssm_out(u2, h_re, h_im, t_mat, wt_blk)


def _attn_out_kernel(sink_ref, q_ref, za_ref, kk_ref, vv_ref, kkp_ref, vvp_ref, y2_ref, zs_ref,
                     x_ref, gate_ref, ag_ref, gw_ref, gb_ref, sg_ref, wo_ref, fg_ref,
                     o_ref, kk_scr, vv_scr, attn_scr, y_scr, mix_scr, out_scr,
                     *, n_tiles, tiles_per_batch):
    step = pl.program_id(0)
    cur_slot = step % 2
    prev_slot = 1 - cur_slot

    @pl.when(step == 0)
    def _():
        mix_scr[1] = jnp.zeros(mix_scr.shape[1:], BF16)

    first = jnp.minimum(step, n_tiles - 1) % tiles_per_batch == 0
    keep = jnp.where(first, 0.0, 1.0).astype(BF16)
    kk_scr[0:BLOCK, :] = kkp_ref[...] * keep
    vv_scr[0:BLOCK, :] = vvp_ref[...] * keep
    kk_scr[BLOCK:, :] = kk_ref[...]
    vv_scr[BLOCK:, :] = vv_ref[...]

    qi = lax.broadcasted_iota(jnp.int32, (BLOCK, 2 * BLOCK), 0)
    kj = lax.broadcasted_iota(jnp.int32, (BLOCK, 2 * BLOCK), 1)
    in_band = (kj > qi) & (kj <= qi + BLOCK)
    bias_any = jnp.where(in_band, 0.0, NEG).astype(F32)
    bias_first = jnp.where(in_band & (kj >= BLOCK), 0.0, NEG).astype(F32)
    lane_lo_kv = lax.broadcasted_iota(jnp.int32, (2 * BLOCK, KV_DUP), 1) < HEAD_DIM
    pairs = Q_PER_KV // 2
    contract_last = (((1,), (1,)), ((), ()))
    for gb in range(N_GROUP_BLOCKS):
        for tok in range(SSM_CHUNK):
            c0 = gb * SLAB + tok * LANES
            y_scr[gb, pl.ds(tok, CHUNK_ROWS, stride=SSM_CHUNK), :] = y2_ref[:, c0:c0 + LANES]

    def block(j, carry):
        r0 = j * BLOCK
        rows = pl.ds(r0, BLOCK)
        out_scr[j] = jnp.dot(mix_scr[prev_slot], wo_ref[j], preferred_element_type=F32)
        bias = jnp.where(first & (j == 0), bias_first, bias_any)
        for h in range(N_KV_HEADS):
            kb = kk_scr[pl.ds(r0, 2 * BLOCK), h * KV_DUP:(h + 1) * KV_DUP]
            vb = vv_scr[pl.ds(r0, 2 * BLOCK), h * KV_DUP:(h + 1) * KV_DUP]
            zero = jnp.zeros_like(kb)
            k_par = (jnp.where(lane_lo_kv, kb, zero), jnp.where(lane_lo_kv, zero, kb))
            v_par = (jnp.where(lane_lo_kv, vb, zero), jnp.where(lane_lo_kv, zero, vb))
            q0 = h * Q_PER_KV * HEAD_DIM
            qs = jnp.concatenate(
                [q_ref[pl.ds(r0, BLOCK), q0 + p * KV_DUP:q0 + (p + 1) * KV_DUP] for p in range(pairs)],
                axis=0)
            acc = [None] * pairs
            for par in range(2):
                s_all = lax.dot_general(qs, k_par[par], contract_last, preferred_element_type=F32)
                probs, inv = [], []
                for p in range(pairs):
                    s = s_all[p * BLOCK:(p + 1) * BLOCK] + bias
                    sink = sink_ref[h * Q_PER_KV + 2 * p + par] * LOG2_E
                    m = jnp.maximum(jnp.max(s, axis=-1, keepdims=True), sink)
                    e = jnp.exp2(s - m)
                    l = jnp.sum(e, axis=-1, keepdims=True) + jnp.exp2(sink - m)
                    inv.append(1.0 / l)
                    probs.append(e.astype(BF16))
                pv = jnp.dot(jnp.concatenate(probs, axis=0), v_par[par], preferred_element_type=F32)
                for p in range(pairs):
                    o = pv[p * BLOCK:(p + 1) * BLOCK] * inv[p]
                    acc[p] = o if par == 0 else acc[p] + o
            for p in range(pairs):
                attn_scr[:, q0 + p * KV_DUP:q0 + (p + 1) * KV_DUP] = acc[p]
        a = attn_scr[...]
        a = a * lax.rsqrt(jnp.mean(a * a, axis=-1, keepdims=True) + EPS) * ag_ref[...]
        za = za_ref[rows, :].astype(F32)
        mix_scr[cur_slot, rows, :ATTN_WIDTH] = (a * (za * jax.nn.sigmoid(za))).astype(BF16)
        y = jnp.concatenate([y_scr[gb, rows, :] for gb in range(N_GROUP_BLOCKS)], axis=1)
        y = 0.5 * y * (1.0 + lax.erf(y * math.sqrt(0.5)))
        glu = jnp.dot(y.astype(BF16), gw_ref[...], preferred_element_type=F32) + gb_ref[...]
        y = y * jax.nn.sigmoid(glu)
        y = y * lax.rsqrt(jnp.mean(y * y, axis=-1, keepdims=True) + EPS) * sg_ref[...]
        zs = zs_ref[rows, :].astype(F32)
        mix_scr[cur_slot, rows, ATTN_WIDTH:] = (y * (zs * jax.nn.sigmoid(zs))).astype(BF16)
        return carry

    for j in range(q_ref.shape[0] // BLOCK):
        block(j, 0)

    out = jnp.concatenate([out_scr[c] for c in range(out_scr.shape[0])], axis=1)
    res = x_ref[...] + gate_ref[0] * out
    o_ref[...] = res * lax.rsqrt(jnp.mean(res * res, axis=-1, keepdims=True) + EPS) * fg_ref[...]


def _attn_out(sinks, q, za, kk, vv, y2, zs, x2, gate, attn_gain, glu_w, glu_b, ssm_gain, w_out,
              final_gain, rows_per_batch):
    t, d = x2.shape
    n_tiles = t // ROW_TILE
    tiles_per_batch = rows_per_batch // ROW_TILE
    blocks_per_tile = ROW_TILE // BLOCK
    cur = lambda i: jnp.minimum(i, n_tiles - 1)
    old = lambda i: jnp.maximum(i - 1, 0)
    row = lambda w: pl.BlockSpec((ROW_TILE, w), lambda i: (cur(i), 0))
    prev_block = pl.BlockSpec(
        (BLOCK, KVD_WIDTH), lambda i: (jnp.maximum(cur(i) * blocks_per_tile - 1, 0), 0))
    out_chunks = d // OUT_CHUNK
    w_chunks = w_out.reshape(d, out_chunks, OUT_CHUNK).transpose(1, 0, 2)
    return pl.pallas_call(
        functools.partial(_attn_out_kernel, n_tiles=n_tiles, tiles_per_batch=tiles_per_batch),
        out_shape=jax.ShapeDtypeStruct((t, d), F32),
        grid=(n_tiles + 1,),
        in_specs=[pl.BlockSpec(memory_space=pltpu.SMEM),
                  row(ATTN_WIDTH), row(ATTN_WIDTH), row(KVD_WIDTH), row(KVD_WIDTH),
                  prev_block, prev_block,
                  pl.BlockSpec((CHUNK_ROWS, y2.shape[1]), lambda i: (cur(i), 0)),
                  row(SSM_WIDTH),
                  pl.BlockSpec((ROW_TILE, d), lambda i: (old(i), 0)),
                  pl.BlockSpec((1, 1, d), lambda i: (old(i) // tiles_per_batch, 0, 0)),
                  _resident((1, ATTN_WIDTH)),
                  _resident((SSM_WIDTH, SSM_WIDTH)), _resident((1, SSM_WIDTH)),
                  _resident((1, SSM_WIDTH)),
                  _resident((out_chunks, d, OUT_CHUNK)), _resident((1, d))],
        out_specs=pl.BlockSpec((ROW_TILE, d), lambda i: (old(i), 0)),
        scratch_shapes=[pltpu.VMEM((ROW_TILE + BLOCK, KVD_WIDTH), BF16),
                        pltpu.VMEM((ROW_TILE + BLOCK, KVD_WIDTH), BF16),
                        pltpu.VMEM((BLOCK, ATTN_WIDTH), F32),
                        pltpu.VMEM((N_GROUP_BLOCKS, ROW_TILE, LANES), F32),
                        pltpu.VMEM((2, ROW_TILE, d), BF16),
                        pltpu.VMEM((out_chunks, ROW_TILE, OUT_CHUNK), F32)],
        compiler_params=pltpu.CompilerParams(dimension_semantics=("arbitrary",),
                                             vmem_limit_bytes=VMEM_LIMIT),
        name="attn_out",
    )(sinks, q, za, kk, vv, kk, vv, y2, zs, x2, gate, attn_gain, glu_w, glu_b, ssm_gain, w_chunks,
      final_gain)


def _pack_in_weights(w_in, b_in):
    a, kv = ATTN_WIDTH, N_KV_HEADS * HEAD_DIM
    o_k, o_v, o_za, o_u, o_zs = a, a + kv, a + 2 * kv, 2 * a + 2 * kv, 2 * a + 2 * kv + SSM_WIDTH

    def pack(m):
        dup = lambda blk: jnp.concatenate(
            [blk.reshape(-1, N_KV_HEADS, 1, HEAD_DIM)] * 2, axis=2).reshape(-1, KVD_WIDTH)
        return jnp.concatenate(
            [m[:, :a] * (LOG2_E * HEAD_DIM ** -0.5), m[:, o_za:o_u], dup(m[:, o_k:o_v]), dup(m[:, o_v:o_za]),
             m[:, o_zs:], m[:, o_u:o_zs]], axis=1)

    return pack(w_in).astype(BF16), pack(b_in.reshape(1, -1))


def kernel(x, c, w_ada, b_ada, norm_gain, w_in, b_in, attn_sinks, attn_out_gain, ssm_lambda_re,
           ssm_lambda_im, ssm_log_step, ssm_b_re, ssm_b_im, ssm_c_re, ssm_c_im, ssm_d, glu_w, glu_b,
           ssm_out_gain, w_out, final_gain):
    bsz, seq, d = x.shape
    assert w_ada.shape[0] == 1, "single-layer trunk only"
    x2 = x.reshape(bsz * seq, d)
    mod = _ada(c, w_ada[0], b_ada[0])
    shift, scale, gate = (m.reshape(bsz, 1, d) for m in jnp.split(mod, 3, axis=-1))
    w_cat, b_cat = _pack_in_weights(w_in[0], b_in[0])
    q, za, kk, vv, zs, u2 = _inproj(x2, scale, shift, norm_gain[0].reshape(1, d), w_cat, b_cat, seq)
    prep = _ssm_prep(ssm_lambda_re[0], ssm_lambda_im[0], ssm_log_step[0], ssm_b_re[0],
                     ssm_b_im[0], ssm_c_re[0], ssm_c_im[0], ssm_d[0])
    y2 = _s5(u2, prep, bsz)
    out = _attn_out(attn_sinks[0], q, za, kk, vv, y2, zs, x2, gate,
                    attn_out_gain[0].reshape(1, -1), glu_w[0].astype(BF16),
                    glu_b[0].reshape(1, -1), ssm_out_gain[0].reshape(1, -1),
                    w_out[0].astype(BF16), final_gain.reshape(1, d), seq)
    return out.reshape(bsz, seq, d)
```

```python
import functools
import math

import jax
import jax.numpy as jnp
from jax import lax
from jax.experimental import pallas as pl
from jax.experimental.pallas import tpu as pltpu

F32 = jnp.float32
BF16 = jnp.bfloat16

D_MODEL = 2048
HEAD_DIM = 64
N_Q_HEADS = 24
N_KV_HEADS = 3
Q_PER_KV = 8
ATTN_WIDTH = N_Q_HEADS * HEAD_DIM
BLOCK = 128
SSM_GROUP = 16
SSM_WIDTH = D_MODEL - ATTN_WIDTH
N_SSM_GROUPS = SSM_WIDTH // SSM_GROUP
STATE = 64
EPS = 1e-5
NEG = -1e30
LOG2_E = math.log2(math.e)

LANES = 128
SSM_CHUNK = 16
CHUNK_COLS = SSM_CHUNK * SSM_GROUP
GROUPS_PER_BLOCK = LANES // SSM_GROUP
N_GROUP_BLOCKS = N_SSM_GROUPS // GROUPS_PER_BLOCK
SLAB = SSM_CHUNK * LANES
STATE_COLS = GROUPS_PER_BLOCK * STATE
KV_WIDTH = N_KV_HEADS * HEAD_DIM
PAIR = 2 * HEAD_DIM
Q_SCALE = LOG2_E * HEAD_DIM ** -0.5
PROJ_COLS = 512

VMEM_LIMIT = 56 * 1024 * 1024

ROW_TILE = 512
CHUNK_ROWS = ROW_TILE // SSM_CHUNK
ADA_COLS = 512
OUT_CHUNK = D_MODEL // (ROW_TILE // BLOCK)
SSM_TILE = 512
SCAN_TILE = 128


def _resident(shape):
    return pl.BlockSpec(shape, lambda *_: (0,) * len(shape), pipeline_mode=pl.Buffered(1))


def _ada_kernel(ct_ref, w_ref, b_ref, o_ref):
    ct = ct_ref[...]
    s = ct * jax.nn.sigmoid(ct)
    w = w_ref[...]
    rows = [jnp.sum(s[:, b:b + 1] * w, axis=0, keepdims=True) for b in range(ct.shape[1])]
    o_ref[...] = jnp.concatenate(rows, axis=0) + b_ref[...]


def _ada(c, w_ada, b_ada):
    bsz, d = c.shape
    n = w_ada.shape[1]
    return pl.pallas_call(
        _ada_kernel,
        out_shape=jax.ShapeDtypeStruct((bsz, n), F32),
        grid=(n // ADA_COLS,),
        in_specs=[pl.BlockSpec((d, bsz), lambda i: (0, 0)),
                  pl.BlockSpec((d, ADA_COLS), lambda i: (0, i)),
                  pl.BlockSpec((1, ADA_COLS), lambda i: (0, i))],
        out_specs=pl.BlockSpec((bsz, ADA_COLS), lambda i: (0, i)),
        compiler_params=pltpu.CompilerParams(dimension_semantics=("arbitrary",)),
        name="ada",
    )(c.T, w_ada, b_ada.reshape(1, n))


def _inproj_kernel(x_ref, sc_ref, sh_ref, g_ref, w_ref, b_ref,
                   q_ref, kv_ref, za_ref, zs_ref, u2_ref, h_scr, u_scr):
    x = x_ref[...]
    var = jnp.mean(x * x, axis=-1, keepdims=True)
    h = x * lax.rsqrt(var + EPS) * g_ref[...]
    h = h * (1.0 + sc_ref[0]) + sh_ref[0]
    h_scr[...] = h.astype(BF16)

    def project(col, width):
        acc = jnp.dot(h_scr[...], w_ref[:, col:col + width], preferred_element_type=F32)
        return acc + b_ref[:, col:col + width]

    col_kv, col_za = ATTN_WIDTH, ATTN_WIDTH + 2 * KV_WIDTH
    col_u = col_za + ATTN_WIDTH
    col_zs = col_u + SSM_WIDTH
    for c0 in range(0, ATTN_WIDTH, PROJ_COLS):
        q_ref[:, c0:c0 + PROJ_COLS] = (project(c0, PROJ_COLS) * Q_SCALE).astype(BF16)
        za_ref[:, c0:c0 + PROJ_COLS] = project(col_za + c0, PROJ_COLS).astype(BF16)
    kv_ref[...] = project(col_kv, 2 * KV_WIDTH).astype(BF16)
    zs_ref[...] = project(col_zs, SSM_WIDTH).astype(BF16)
    u = project(col_u, SSM_WIDTH)
    for gb in range(N_GROUP_BLOCKS):
        u_scr[gb] = u[:, gb * LANES:(gb + 1) * LANES]
        for tok in range(SSM_CHUNK):
            c0 = gb * SLAB + tok * LANES
            u2_ref[:, c0:c0 + LANES] = (
                u_scr[gb, pl.ds(tok, CHUNK_ROWS, stride=SSM_CHUNK), :].astype(BF16))


def _inproj(x2, scale, shift, gain, w_cat, b_cat, rows_per_batch):
    t, d = x2.shape
    widths = (ATTN_WIDTH, 2 * KV_WIDTH, ATTN_WIDTH, SSM_WIDTH)
    n_all = w_cat.shape[1]
    tiles_per_batch = rows_per_batch // ROW_TILE
    mod_spec = pl.BlockSpec((1, 1, d), lambda i: (i // tiles_per_batch, 0, 0))
    u2_cols = N_GROUP_BLOCKS * SLAB
    return pl.pallas_call(
        _inproj_kernel,
        out_shape=[jax.ShapeDtypeStruct((t, w), BF16) for w in widths]
        + [jax.ShapeDtypeStruct((t // SSM_CHUNK, u2_cols), BF16)],
        grid=(t // ROW_TILE,),
        in_specs=[pl.BlockSpec((ROW_TILE, d), lambda i: (i, 0)),
                  mod_spec, mod_spec,
                  _resident((1, d)),
                  _resident((d, n_all)),
                  _resident((1, n_all))],
        out_specs=[pl.BlockSpec((ROW_TILE, w), lambda i: (i, 0)) for w in widths]
        + [pl.BlockSpec((CHUNK_ROWS, u2_cols), lambda i: (i, 0))],
        scratch_shapes=[pltpu.VMEM((ROW_TILE, d), BF16),
                        pltpu.VMEM((N_GROUP_BLOCKS, ROW_TILE, LANES), F32)],
        compiler_params=pltpu.CompilerParams(dimension_semantics=("arbitrary",),
                                             vmem_limit_bytes=VMEM_LIMIT),
        name="inproj",
    )(x2, scale, shift, gain, w_cat, b_cat)


def _ssm_prep_kernel(lr_ref, li_ref, ls_ref, br_ref, bi_ref, cr_ref, ci_ref, d_ref,
                     bd_ref, vr_ref, vi_ref, wr_ref, wi_ref, a16_ref, y_scr):
    lr, li = lr_ref[...], li_ref[...]
    step = jnp.exp(ls_ref[...])
    decay = jnp.exp(lr * step)
    ar, ai = decay * jnp.cos(li * step), decay * jnp.sin(li * step)
    den = lr * lr + li * li
    nr, ni = ar - 1.0, ai
    coef_re = (nr * lr + ni * li) / den
    coef_im = (ni * lr - nr * li) / den
    br, bi = br_ref[...], bi_ref[...]
    bbar_re = coef_re * br - coef_im * bi
    bbar_im = coef_re * bi + coef_im * br
    cr, ci = cr_ref[...], ci_ref[...]
    lo = lax.broadcasted_iota(jnp.int32, lr.shape, 2) < STATE
    pr, pi = jnp.ones_like(ar), jnp.zeros_like(ai)
    for tau in range(SSM_CHUNK + 1):
        er, ei = pr * cr - pi * ci, pr * ci + pi * cr
        if tau < SSM_CHUNK:
            y_scr[:, tau * LANES:(tau + 1) * LANES, :] = jnp.where(lo, er, ei)
            s = SSM_CHUNK - 1 - tau
            vr_ref[:, s * LANES:(s + 1) * LANES, :] = pr * bbar_re - pi * bbar_im
            vi_ref[:, s * LANES:(s + 1) * LANES, :] = pr * bbar_im + pi * bbar_re
        if tau >= 1:
            wr_ref[:, (tau - 1) * LANES:tau * LANES, :] = er
            wi_ref[:, (tau - 1) * LANES:tau * LANES, :] = -ei
        if tau == SSM_CHUNK:
            a16_ref[...] = jnp.where(lo, pr, pi)
        pr, pi = pr * ar - pi * ai, pr * ai + pi * ar
    x = jnp.where(lo, bbar_re, -bbar_im)
    row = lax.broadcasted_iota(jnp.int32, (LANES, SLAB), 0)
    col = lax.broadcasted_iota(jnp.int32, (LANES, SLAB), 1)
    same_group = row // SSM_GROUP == (col % LANES) // SSM_GROUP
    diagonal = (lax.broadcasted_iota(jnp.int32, (LANES, LANES), 0)
                == lax.broadcasted_iota(jnp.int32, (LANES, LANES), 1))
    for gb in range(N_GROUP_BLOCKS):
        k = lax.dot_general(x[gb], y_scr[gb], (((1,), (1,)), ((), ())),
                            precision=lax.Precision.HIGHEST, preferred_element_type=F32)
        k = jnp.where(same_group, k, 0.0)
        bd_ref[gb, :, LANES:] = k[:, LANES:]
        bd_ref[gb, :, :LANES] = k[:, :LANES] + jnp.where(diagonal, d_ref[gb], 0.0)


def _ssm_prep(lam_re, lam_im, log_step, b_re, b_im, c_re, c_im, d_skip):
    nb = N_GROUP_BLOCKS
    rows = lambda a: jnp.concatenate([a, a], axis=-1).reshape(nb, LANES, 2 * STATE)
    per_group = lambda a: rows(jnp.repeat(a[:, None, :], SSM_GROUP, axis=1))
    compact = jax.ShapeDtypeStruct((nb, SLAB, 2 * STATE), F32)
    return pl.pallas_call(
        _ssm_prep_kernel,
        out_shape=[jax.ShapeDtypeStruct((nb, LANES, SLAB), F32),
                   compact, compact,
                   compact, compact,
                   jax.ShapeDtypeStruct((nb, LANES, 2 * STATE), F32)],
        scratch_shapes=[pltpu.VMEM((nb, SLAB, 2 * STATE), F32)],
        compiler_params=pltpu.CompilerParams(vmem_limit_bytes=VMEM_LIMIT),
        name="ssm_prep",
    )(per_group(lam_re), per_group(lam_im),
      jnp.broadcast_to(jnp.repeat(log_step, SSM_GROUP).reshape(nb, LANES, 1), (nb, LANES, 2 * STATE)),
      rows(jnp.swapaxes(b_re, 1, 2)), rows(jnp.swapaxes(b_im, 1, 2)), rows(c_re), rows(c_im),
      jnp.broadcast_to(d_skip.reshape(nb, 1, LANES), (nb, LANES, LANES)))


def _ssm_expand_kernel(bd_ref, vr_ref, vi_ref, wr_ref, wi_ref, t_ref, v_ref, wt_ref):
    bd = bd_ref[0].astype(BF16)
    for s in range(SSM_CHUNK):
        rows = slice(s * LANES, (s + 1) * LANES)
        if s:
            t_ref[0, rows, :s * LANES] = jnp.zeros((LANES, s * LANES), BF16)
        t_ref[0, rows, s * LANES:] = bd[:, :(SSM_CHUNK - s) * LANES]
    shape = (SLAB, 2 * STATE)
    row_group = (lax.broadcasted_iota(jnp.int32, shape, 0) % LANES) // SSM_GROUP
    half = lax.broadcasted_iota(jnp.int32, shape, 1) // STATE
    for j in range(STATE_COLS // LANES):
        own = row_group == 2 * j + half
        for src, dst, c0 in ((vr_ref, v_ref, 0), (vi_ref, v_ref, STATE_COLS),
                             (wr_ref, wt_ref, 0), (wi_ref, wt_ref, STATE_COLS)):
            dst[0, :, c0 + j * LANES:c0 + (j + 1) * LANES] = (
                jnp.where(own, src[0], 0.0).astype(BF16))


def _chunk_matrices(prep):
    nb = N_GROUP_BLOCKS
    per_block = lambda shape: pl.BlockSpec((1,) + shape, lambda gb: (gb, 0, 0))
    return pl.pallas_call(
        _ssm_expand_kernel,
        out_shape=[jax.ShapeDtypeStruct((nb, SLAB, SLAB), BF16),
                   jax.ShapeDtypeStruct((nb, SLAB, 2 * STATE_COLS), BF16),
                   jax.ShapeDtypeStruct((nb, SLAB, 2 * STATE_COLS), BF16)],
        grid=(nb,),
        in_specs=[per_block((LANES, SLAB))] + [per_block((SLAB, 2 * STATE))] * 4,
        out_specs=[per_block((SLAB, SLAB)), per_block((SLAB, 2 * STATE_COLS)),
                   per_block((SLAB, 2 * STATE_COLS))],
        compiler_params=pltpu.CompilerParams(dimension_semantics=("arbitrary",),
                                             vmem_limit_bytes=VMEM_LIMIT),
        name="ssm_expand",
    )(*prep[:5])


def _ssm_state_kernel(u_ref, v_ref, sr_ref, si_ref):
    s = jnp.dot(u_ref[...], v_ref[0], preferred_element_type=F32)
    sr_ref[...] = s[:, :STATE_COLS]
    si_ref[...] = s[:, STATE_COLS:]


def _ssm_state(u2, v_blk):
    n = u2.shape[0]
    state = pl.BlockSpec((SSM_TILE, STATE_COLS), lambda gb, i: (i, gb))
    return pl.pallas_call(
        _ssm_state_kernel,
        out_shape=[jax.ShapeDtypeStruct((n, N_GROUP_BLOCKS * STATE_COLS), F32)] * 2,
        grid=(N_GROUP_BLOCKS, n // SSM_TILE),
        in_specs=[pl.BlockSpec((SSM_TILE, SLAB), lambda gb, i: (i, gb)),
                  pl.BlockSpec((1, SLAB, 2 * STATE_COLS), lambda gb, i: (gb, 0, 0))],
        out_specs=[state, state],
        compiler_params=pltpu.CompilerParams(dimension_semantics=("arbitrary", "arbitrary"),
                                             vmem_limit_bytes=VMEM_LIMIT),
        name="ssm_state",
    )(u2, v_blk)


def _ssm_scan_kernel(sr_ref, si_ref, ar_ref, ai_ref, hr_ref, hi_ref, cr_scr, ci_scr):
    @pl.when(pl.program_id(0) == 0)
    def _():
        cr_scr[...] = jnp.zeros_like(cr_scr)
        ci_scr[...] = jnp.zeros_like(ci_scr)

    ar, ai = ar_ref[...], ai_ref[...]
    nb = sr_ref.shape[0]

    def body(k, carry):
        out = []
        for b in range(nb):
            hr, hi = carry[2 * b], carry[2 * b + 1]
            hr_ref[b, k] = hr
            hi_ref[b, k] = hi
            out += [ar * hr - ai * hi + sr_ref[b, k], ar * hi + ai * hr + si_ref[b, k]]
        return tuple(out)

    init = tuple(scr[b] for b in range(nb) for scr in (cr_scr, ci_scr))
    fin = lax.fori_loop(0, SCAN_TILE, body, init, unroll=8)
    for b in range(nb):
        cr_scr[b] = fin[2 * b]
        ci_scr[b] = fin[2 * b + 1]


def _ssm_scan(s_re, s_im, a_re, a_im):
    bsz, k, rows, lanes = s_re.shape
    blk = pl.BlockSpec((bsz, SCAN_TILE, rows, lanes), lambda i: (0, i, 0, 0))
    return pl.pallas_call(
        _ssm_scan_kernel,
        out_shape=[jax.ShapeDtypeStruct(s_re.shape, F32)] * 2,
        grid=(k // SCAN_TILE,),
        in_specs=[blk, blk, _resident((rows, lanes)), _resident((rows, lanes))],
        out_specs=[blk, blk],
        scratch_shapes=[pltpu.VMEM((bsz, rows, lanes), F32)] * 2,
        compiler_params=pltpu.CompilerParams(dimension_semantics=("arbitrary",)),
        name="ssm_scan",
    )(s_re, s_im, a_re, a_im)


def _ssm_out_kernel(u_ref, hr_ref, hi_ref, t_ref, wt_ref, y_ref):
    pair = 2 * LANES
    contract_last = (((1,), (1,)), ((), ()))
    for tb in range(SLAB // pair):
        k_rows = (tb + 1) * pair
        cols = slice(tb * pair, (tb + 1) * pair)
        y = jnp.dot(u_ref[:, :k_rows], t_ref[0, :k_rows, cols], preferred_element_type=F32)
        y += lax.dot_general(hr_ref[...], wt_ref[0, cols, :STATE_COLS], contract_last,
                             preferred_element_type=F32)
        y += lax.dot_general(hi_ref[...], wt_ref[0, cols, STATE_COLS:], contract_last,
                             preferred_element_type=F32)
        y_ref[:, cols] = y


def _ssm_out(u2, h_re, h_im, t_mat, wt_blk):
    n = u2.shape[0]
    state = pl.BlockSpec((SSM_TILE, STATE_COLS), lambda gb, i: (i, gb))
    slab = pl.BlockSpec((SSM_TILE, SLAB), lambda gb, i: (i, gb))
    return pl.pallas_call(
        _ssm_out_kernel,
        out_shape=jax.ShapeDtypeStruct(u2.shape, F32),
        grid=(N_GROUP_BLOCKS, n // SSM_TILE),
        in_specs=[slab, state, state,
                  pl.BlockSpec((1, SLAB, SLAB), lambda gb, i: (gb, 0, 0)),
                  pl.BlockSpec((1, SLAB, 2 * STATE_COLS), lambda gb, i: (gb, 0, 0))],
        out_specs=slab,
        compiler_params=pltpu.CompilerParams(dimension_semantics=("arbitrary", "arbitrary"),
                                             vmem_limit_bytes=VMEM_LIMIT),
        name="ssm_out",
    )(u2, h_re, h_im, t_mat, wt_blk)


def _s5(u2, prep, bsz):
    t_mat, v_blk, wt_blk = _chunk_matrices(prep)
    a16 = prep[5].reshape(N_SSM_GROUPS, SSM_GROUP, 2 * STATE)[:, 0]
    n = u2.shape[0]
    kc = n // bsz
    s_re, s_im = _ssm_state(u2, v_blk)
    scan_shape = (bsz, kc, N_SSM_GROUPS * STATE // LANES, LANES)
    a_re = a16[:, :STATE].reshape(scan_shape[2:])
    a_im = a16[:, STATE:].reshape(scan_shape[2:])
    h_re, h_im = _ssm_scan(s_re.reshape(scan_shape), s_im.reshape(scan_shape), a_re, a_im)
    h_re = h_re.reshape(n, -1).astype(BF16)
    h_im = h_im.reshape(n, -1).astype(BF16)
    return _ssm_out(u2, h_re, h_im, t_mat, wt_blk)


def _swap_lane_halves(x):
    packed = pltpu.bitcast(x, jnp.uint32)
    return pltpu.bitcast(pltpu.roll(packed, HEAD_DIM, axis=1), BF16)


def _attn_out_kernel(sink_ref, q_ref, za_ref, kv_ref, kvp_ref, y2_ref, zs_ref,
                     x_ref, gate_ref, ag_ref, gw_ref, gb_ref, sg_ref, wo_ref, fg_ref,
                     o_ref, kv_scr, attn_scr, y_scr, mix_scr, out_scr,
                     *, n_tiles, tiles_per_batch):
    step = pl.program_id(0)
    cur_slot = step % 2
    prev_slot = 1 - cur_slot

    @pl.when(step == 0)
    def _():
        mix_scr[1] = jnp.zeros(mix_scr.shape[1:], BF16)

    first = jnp.minimum(step, n_tiles - 1) % tiles_per_batch == 0
    keep = jnp.where(first, 0.0, 1.0).astype(BF16)
    kv_scr[0:BLOCK, :] = kvp_ref[...] * keep
    kv_scr[BLOCK:, :] = kv_ref[...]

    qi = lax.broadcasted_iota(jnp.int32, (BLOCK, 2 * BLOCK), 0)
    kj = lax.broadcasted_iota(jnp.int32, (BLOCK, 2 * BLOCK), 1)
    in_band = (kj > qi) & (kj <= qi + BLOCK)
    bias_any = jnp.where(in_band, 0.0, NEG).astype(F32)
    bias_first = jnp.where(in_band & (kj >= BLOCK), 0.0, NEG).astype(F32)
    kv_half = lax.broadcasted_iota(jnp.int32, (2 * BLOCK, LANES), 1) // HEAD_DIM
    pairs = Q_PER_KV // 2

    def both_halves(r0, col):
        tile, half = col // LANES, (col % LANES) // HEAD_DIM
        band = kv_scr[pl.ds(r0, 2 * BLOCK), tile * LANES:(tile + 1) * LANES]
        own = jnp.where(kv_half == half, band, jnp.zeros_like(band))
        other = _swap_lane_halves(own)
        return (own, other) if half == 0 else (other, own)

    contract_last = (((1,), (1,)), ((), ()))
    for gb in range(N_GROUP_BLOCKS):
        for tok in range(SSM_CHUNK):
            c0 = gb * SLAB + tok * LANES
            y_scr[gb, pl.ds(tok, CHUNK_ROWS, stride=SSM_CHUNK), :] = y2_ref[:, c0:c0 + LANES]

    def block(j, carry):
        r0 = j * BLOCK
        rows = pl.ds(r0, BLOCK)
        out_scr[j] = jnp.dot(mix_scr[prev_slot], wo_ref[:, j * OUT_CHUNK:(j + 1) * OUT_CHUNK],
                             preferred_element_type=F32)
        bias = jnp.where(first & (j == 0), bias_first, bias_any)
        for h in range(N_KV_HEADS):
            k_par = both_halves(r0, h * HEAD_DIM)
            v_par = both_halves(r0, KV_WIDTH + h * HEAD_DIM)
            q0 = h * Q_PER_KV * HEAD_DIM
            qs = jnp.concatenate(
                [q_ref[pl.ds(r0, BLOCK), q0 + p * PAIR:q0 + (p + 1) * PAIR] for p in range(pairs)],
                axis=0)
            acc = [None] * pairs
            for par in range(2):
                s_all = lax.dot_general(qs, k_par[par], contract_last, preferred_element_type=F32)
                probs, inv = [], []
                for p in range(pairs):
                    s = s_all[p * BLOCK:(p + 1) * BLOCK] + bias
                    sink = sink_ref[h * Q_PER_KV + 2 * p + par] * LOG2_E
                    m = jnp.maximum(jnp.max(s, axis=-1, keepdims=True), sink)
                    e = jnp.exp2(s - m)
                    l = jnp.sum(e, axis=-1, keepdims=True) + jnp.exp2(sink - m)
                    inv.append(1.0 / l)
                    probs.append(e.astype(BF16))
                pv = jnp.dot(jnp.concatenate(probs, axis=0), v_par[par], preferred_element_type=F32)
                for p in range(pairs):
                    o = pv[p * BLOCK:(p + 1) * BLOCK] * inv[p]
                    acc[p] = o if par == 0 else acc[p] + o
            for p in range(pairs):
                attn_scr[:, q0 + p * PAIR:q0 + (p + 1) * PAIR] = acc[p]
        a = attn_scr[...]
        a = a * lax.rsqrt(jnp.mean(a * a, axis=-1, keepdims=True) + EPS) * ag_ref[...]
        za = za_ref[rows, :].astype(F32)
        mix_scr[cur_slot, rows, :ATTN_WIDTH] = (a * (za * jax.nn.sigmoid(za))).astype(BF16)
        y = jnp.concatenate([y_scr[gb, rows, :] for gb in range(N_GROUP_BLOCKS)], axis=1)
        y = 0.5 * y * (1.0 + lax.erf(y * math.sqrt(0.5)))
        glu = jnp.dot(y.astype(BF16), gw_ref[...], preferred_element_type=F32) + gb_ref[...]
        y = y * jax.nn.sigmoid(glu)
        y = y * lax.rsqrt(jnp.mean(y * y, axis=-1, keepdims=True) + EPS) * sg_ref[...]
        zs = zs_ref[rows, :].astype(F32)
        mix_scr[cur_slot, rows, ATTN_WIDTH:] = (y * (zs * jax.nn.sigmoid(zs))).astype(BF16)
        return carry

    for j in range(q_ref.shape[0] // BLOCK):
        block(j, 0)

    out = jnp.concatenate([out_scr[c] for c in range(out_scr.shape[0])], axis=1)
    res = x_ref[...] + gate_ref[0] * out
    o_ref[...] = res * lax.rsqrt(jnp.mean(res * res, axis=-1, keepdims=True) + EPS) * fg_ref[...]


def _attn_out(sinks, q, za, kv, y2, zs, x2, gate, attn_gain, glu_w, glu_b, ssm_gain, w_out,
              final_gain, rows_per_batch):
    t, d = x2.shape
    n_tiles = t // ROW_TILE
    tiles_per_batch = rows_per_batch // ROW_TILE
    blocks_per_tile = ROW_TILE // BLOCK
    cur = lambda i: jnp.minimum(i, n_tiles - 1)
    old = lambda i: jnp.maximum(i - 1, 0)
    row = lambda w: pl.BlockSpec((ROW_TILE, w), lambda i: (cur(i), 0))
    prev_block = pl.BlockSpec(
        (BLOCK, 2 * KV_WIDTH), lambda i: (jnp.maximum(cur(i) * blocks_per_tile - 1, 0), 0))
    out_chunks = d // OUT_CHUNK
    return pl.pallas_call(
        functools.partial(_attn_out_kernel, n_tiles=n_tiles, tiles_per_batch=tiles_per_batch),
        out_shape=jax.ShapeDtypeStruct((t, d), F32),
        grid=(n_tiles + 1,),
        in_specs=[pl.BlockSpec(memory_space=pltpu.SMEM),
                  row(ATTN_WIDTH), row(ATTN_WIDTH), row(2 * KV_WIDTH), prev_block,
                  pl.BlockSpec((CHUNK_ROWS, y2.shape[1]), lambda i: (cur(i), 0)),
                  row(SSM_WIDTH),
                  pl.BlockSpec((ROW_TILE, d), lambda i: (old(i), 0)),
                  pl.BlockSpec((1, 1, d), lambda i: (old(i) // tiles_per_batch, 0, 0)),
                  _resident((1, ATTN_WIDTH)),
                  _resident((SSM_WIDTH, SSM_WIDTH)), _resident((1, SSM_WIDTH)),
                  _resident((1, SSM_WIDTH)),
                  _resident((d, d)), _resident((1, d))],
        out_specs=pl.BlockSpec((ROW_TILE, d), lambda i: (old(i), 0)),
        scratch_shapes=[pltpu.VMEM((ROW_TILE + BLOCK, 2 * KV_WIDTH), BF16),
                        pltpu.VMEM((BLOCK, ATTN_WIDTH), F32),
                        pltpu.VMEM((N_GROUP_BLOCKS, ROW_TILE, LANES), F32),
                        pltpu.VMEM((2, ROW_TILE, d), BF16),
                        pltpu.VMEM((out_chunks, ROW_TILE, OUT_CHUNK), F32)],
        compiler_params=pltpu.CompilerParams(dimension_semantics=("arbitrary",),
                                             vmem_limit_bytes=VMEM_LIMIT),
        name="attn_out",
    )(sinks, q, za, kv, kv, y2, zs, x2, gate, attn_gain, glu_w, glu_b, ssm_gain, w_out, final_gain)


def kernel(x, c, w_ada, b_ada, norm_gain, w_in, b_in, attn_sinks, attn_out_gain, ssm_lambda_re,
           ssm_lambda_im, ssm_log_step, ssm_b_re, ssm_b_im, ssm_c_re, ssm_c_im, ssm_d, glu_w, glu_b,
           ssm_out_gain, w_out, final_gain):
    bsz, seq, d = x.shape
    assert w_ada.shape[0] == 1, "single-layer trunk only"
    x2 = x.reshape(bsz * seq, d)
    mod = _ada(c, w_ada[0], b_ada[0])
    shift, scale, gate = (m.reshape(bsz, 1, d) for m in jnp.split(mod, 3, axis=-1))
    q, kv, za, zs, u2 = _inproj(x2, scale, shift, norm_gain[0].reshape(1, d),
                                w_in[0].astype(BF16), b_in[0].reshape(1, -1), seq)
    prep = _ssm_prep(ssm_lambda_re[0], ssm_lambda_im[0], ssm_log_step[0], ssm_b_re[0],
                     ssm_b_im[0], ssm_c_re[0], ssm_c_im[0], ssm_d[0])
    y2 = _s5(u2, prep, bsz)
    out = _attn_out(attn_sinks[0], q, za, kv, y2, zs, x2, gate,
                    attn_out_gain[0].reshape(1, -1), glu_w[0].astype(BF16),
                    glu_b[0].reshape(1, -1), ssm_out_gain[0].reshape(1, -1),
                    w_out[0].astype(BF16), final_gain.reshape(1, d), seq)
    return out.reshape(bsz, seq, d)
```

```python
import functools
import math

import jax
import jax.numpy as jnp
from jax import lax
from jax.experimental import pallas as pl
from jax.experimental.pallas import tpu as pltpu

F32 = jnp.float32
BF16 = jnp.bfloat16

D_MODEL = 2048
HEAD_DIM = 64
N_Q_HEADS = 24
N_KV_HEADS = 3
Q_PER_KV = 8
ATTN_WIDTH = N_Q_HEADS * HEAD_DIM
BLOCK = 128
SSM_GROUP = 16
SSM_WIDTH = D_MODEL - ATTN_WIDTH
N_SSM_GROUPS = SSM_WIDTH // SSM_GROUP
STATE = 64
EPS = 1e-5
NEG = -1e30
LOG2_E = math.log2(math.e)

LANES = 128
SSM_CHUNK = 16
CHUNK_COLS = SSM_CHUNK * SSM_GROUP
GROUPS_PER_BLOCK = LANES // SSM_GROUP
N_GROUP_BLOCKS = N_SSM_GROUPS // GROUPS_PER_BLOCK
SLAB = SSM_CHUNK * LANES
STATE_COLS = GROUPS_PER_BLOCK * STATE
KV_WIDTH = N_KV_HEADS * HEAD_DIM
PAIR = 2 * HEAD_DIM
Q_SCALE = LOG2_E * HEAD_DIM ** -0.5
PROJ_COLS = 512

VMEM_LIMIT = 56 * 1024 * 1024

ROW_TILE = 512
CHUNK_ROWS = ROW_TILE // SSM_CHUNK
ADA_COLS = 512
OUT_CHUNK = D_MODEL // (ROW_TILE // BLOCK)
SSM_TILE = 512
SCAN_TILE = 128


def _resident(shape):
    return pl.BlockSpec(shape, lambda *_: (0,) * len(shape), pipeline_mode=pl.Buffered(1))


def _ada_kernel(ct_ref, w_ref, b_ref, o_ref):
    ct = ct_ref[...]
    s = ct * jax.nn.sigmoid(ct)
    w = w_ref[...]
    rows = [jnp.sum(s[:, b:b + 1] * w, axis=0, keepdims=True) for b in range(ct.shape[1])]
    o_ref[...] = jnp.concatenate(rows, axis=0) + b_ref[...]


def _ada(c, w_ada, b_ada):
    bsz, d = c.shape
    n = w_ada.shape[1]
    return pl.pallas_call(
        _ada_kernel,
        out_shape=jax.ShapeDtypeStruct((bsz, n), F32),
        grid=(n // ADA_COLS,),
        in_specs=[pl.BlockSpec((d, bsz), lambda i: (0, 0)),
                  pl.BlockSpec((d, ADA_COLS), lambda i: (0, i)),
                  pl.BlockSpec((1, ADA_COLS), lambda i: (0, i))],
        out_specs=pl.BlockSpec((bsz, ADA_COLS), lambda i: (0, i)),
        compiler_params=pltpu.CompilerParams(dimension_semantics=("arbitrary",)),
        name="ada",
    )(c.T, w_ada, b_ada.reshape(1, n))


def _inproj_kernel(x_ref, sc_ref, sh_ref, g_ref, w_ref, b_ref,
                   q_ref, kv_ref, za_ref, zs_ref, u2_ref, h_scr, u_scr):
    x = x_ref[...]
    var = jnp.mean(x * x, axis=-1, keepdims=True)
    h = x * lax.rsqrt(var + EPS) * g_ref[...]
    h = h * (1.0 + sc_ref[0]) + sh_ref[0]
    h_scr[...] = h.astype(BF16)

    def project(col, width):
        acc = jnp.dot(h_scr[...], w_ref[:, col:col + width], preferred_element_type=F32)
        return acc + b_ref[:, col:col + width]

    col_kv, col_za = ATTN_WIDTH, ATTN_WIDTH + 2 * KV_WIDTH
    col_u = col_za + ATTN_WIDTH
    col_zs = col_u + SSM_WIDTH
    for c0 in range(0, ATTN_WIDTH, PROJ_COLS):
        q_ref[:, c0:c0 + PROJ_COLS] = (project(c0, PROJ_COLS) * Q_SCALE).astype(BF16)
        za_ref[:, c0:c0 + PROJ_COLS] = project(col_za + c0, PROJ_COLS).astype(BF16)
    kv_ref[...] = project(col_kv, 2 * KV_WIDTH).astype(BF16)
    zs_ref[...] = project(col_zs, SSM_WIDTH).astype(BF16)
    u = project(col_u, SSM_WIDTH)
    for gb in range(N_GROUP_BLOCKS):
        u_scr[gb] = u[:, gb * LANES:(gb + 1) * LANES]
        for tok in range(SSM_CHUNK):
            c0 = gb * SLAB + tok * LANES
            u2_ref[:, c0:c0 + LANES] = (
                u_scr[gb, pl.ds(tok, CHUNK_ROWS, stride=SSM_CHUNK), :].astype(BF16))


def _inproj(x2, scale, shift, gain, w_cat, b_cat, rows_per_batch):
    t, d = x2.shape
    widths = (ATTN_WIDTH, 2 * KV_WIDTH, ATTN_WIDTH, SSM_WIDTH)
    n_all = w_cat.shape[1]
    tiles_per_batch = rows_per_batch // ROW_TILE
    mod_spec = pl.BlockSpec((1, 1, d), lambda i: (i // tiles_per_batch, 0, 0))
    u2_cols = N_GROUP_BLOCKS * SLAB
    return pl.pallas_call(
        _inproj_kernel,
        out_shape=[jax.ShapeDtypeStruct((t, w), BF16) for w in widths]
        + [jax.ShapeDtypeStruct((t // SSM_CHUNK, u2_cols), BF16)],
        grid=(t // ROW_TILE,),
        in_specs=[pl.BlockSpec((ROW_TILE, d), lambda i: (i, 0)),
                  mod_spec, mod_spec,
                  _resident((1, d)),
                  _resident((d, n_all)),
                  _resident((1, n_all))],
        out_specs=[pl.BlockSpec((ROW_TILE, w), lambda i: (i, 0)) for w in widths]
        + [pl.BlockSpec((CHUNK_ROWS, u2_cols), lambda i: (i, 0))],
        scratch_shapes=[pltpu.VMEM((ROW_TILE, d), BF16),
                        pltpu.VMEM((N_GROUP_BLOCKS, ROW_TILE, LANES), F32)],
        compiler_params=pltpu.CompilerParams(dimension_semantics=("arbitrary",),
                                             vmem_limit_bytes=VMEM_LIMIT),
        name="inproj",
    )(x2, scale, shift, gain, w_cat, b_cat)


def _ssm_prep_kernel(lr_ref, li_ref, ls_ref, br_ref, bi_ref, cr_ref, ci_ref, d_ref,
                     bd_ref, vr_ref, vi_ref, wr_ref, wi_ref, a16_ref, y_scr):
    lr, li = lr_ref[...], li_ref[...]
    step = jnp.exp(ls_ref[...])
    decay = jnp.exp(lr * step)
    ar, ai = decay * jnp.cos(li * step), decay * jnp.sin(li * step)
    den = lr * lr + li * li
    nr, ni = ar - 1.0, ai
    coef_re = (nr * lr + ni * li) / den
    coef_im = (ni * lr - nr * li) / den
    br, bi = br_ref[...], bi_ref[...]
    bbar_re = coef_re * br - coef_im * bi
    bbar_im = coef_re * bi + coef_im * br
    cr, ci = cr_ref[...], ci_ref[...]
    lo = lax.broadcasted_iota(jnp.int32, lr.shape, 2) < STATE
    pr, pi = jnp.ones_like(ar), jnp.zeros_like(ai)
    for tau in range(SSM_CHUNK + 1):
        er, ei = pr * cr - pi * ci, pr * ci + pi * cr
        if tau < SSM_CHUNK:
            y_scr[:, tau * LANES:(tau + 1) * LANES, :] = jnp.where(lo, er, ei)
            s = SSM_CHUNK - 1 - tau
            vr_ref[:, s * LANES:(s + 1) * LANES, :] = pr * bbar_re - pi * bbar_im
            vi_ref[:, s * LANES:(s + 1) * LANES, :] = pr * bbar_im + pi * bbar_re
        if tau >= 1:
            wr_ref[:, (tau - 1) * LANES:tau * LANES, :] = er
            wi_ref[:, (tau - 1) * LANES:tau * LANES, :] = -ei
        if tau == SSM_CHUNK:
            a16_ref[...] = jnp.where(lo, pr, pi)
        pr, pi = pr * ar - pi * ai, pr * ai + pi * ar
    x = jnp.where(lo, bbar_re, -bbar_im)
    row = lax.broadcasted_iota(jnp.int32, (LANES, SLAB), 0)
    col = lax.broadcasted_iota(jnp.int32, (LANES, SLAB), 1)
    same_group = row // SSM_GROUP == (col % LANES) // SSM_GROUP
    diagonal = (lax.broadcasted_iota(jnp.int32, (LANES, LANES), 0)
                == lax.broadcasted_iota(jnp.int32, (LANES, LANES), 1))
    for gb in range(N_GROUP_BLOCKS):
        k = lax.dot_general(x[gb], y_scr[gb], (((1,), (1,)), ((), ())),
                            precision=lax.Precision.HIGHEST, preferred_element_type=F32)
        k = jnp.where(same_group, k, 0.0)
        bd_ref[gb, :, LANES:] = k[:, LANES:]
        bd_ref[gb, :, :LANES] = k[:, :LANES] + jnp.where(diagonal, d_ref[gb], 0.0)


def _ssm_prep(lam_re, lam_im, log_step, b_re, b_im, c_re, c_im, d_skip):
    nb = N_GROUP_BLOCKS
    rows = lambda a: jnp.concatenate([a, a], axis=-1).reshape(nb, LANES, 2 * STATE)
    per_group = lambda a: rows(jnp.repeat(a[:, None, :], SSM_GROUP, axis=1))
    compact = jax.ShapeDtypeStruct((nb, SLAB, 2 * STATE), F32)
    return pl.pallas_call(
        _ssm_prep_kernel,
        out_shape=[jax.ShapeDtypeStruct((nb, LANES, SLAB), F32),
                   compact, compact,
                   compact, compact,
                   jax.ShapeDtypeStruct((nb, LANES, 2 * STATE), F32)],
        scratch_shapes=[pltpu.VMEM((nb, SLAB, 2 * STATE), F32)],
        compiler_params=pltpu.CompilerParams(vmem_limit_bytes=VMEM_LIMIT),
        name="ssm_prep",
    )(per_group(lam_re), per_group(lam_im),
      jnp.broadcast_to(jnp.repeat(log_step, SSM_GROUP).reshape(nb, LANES, 1), (nb, LANES, 2 * STATE)),
      rows(jnp.swapaxes(b_re, 1, 2)), rows(jnp.swapaxes(b_im, 1, 2)), rows(c_re), rows(c_im),
      jnp.broadcast_to(d_skip.reshape(nb, 1, LANES), (nb, LANES, LANES)))


def _ssm_expand_kernel(bd_ref, vr_ref, vi_ref, wr_ref, wi_ref, t_ref, v_ref, wt_ref):
    bd = bd_ref[0].astype(BF16)
    for s in range(SSM_CHUNK):
        rows = slice(s * LANES, (s + 1) * LANES)
        if s:
            t_ref[0, rows, :s * LANES] = jnp.zeros((LANES, s * LANES), BF16)
        t_ref[0, rows, s * LANES:] = bd[:, :(SSM_CHUNK - s) * LANES]
    shape = (SLAB, 2 * STATE)
    row_group = (lax.broadcasted_iota(jnp.int32, shape, 0) % LANES) // SSM_GROUP
    half = lax.broadcasted_iota(jnp.int32, shape, 1) // STATE
    for j in range(STATE_COLS // LANES):
        own = row_group == 2 * j + half
        for src, dst, c0 in ((vr_ref, v_ref, 0), (vi_ref, v_ref, STATE_COLS),
                             (wr_ref, wt_ref, 0), (wi_ref, wt_ref, STATE_COLS)):
            dst[0, :, c0 + j * LANES:c0 + (j + 1) * LANES] = (
                jnp.where(own, src[0], 0.0).astype(BF16))


def _chunk_matrices(prep):
    nb = N_GROUP_BLOCKS
    per_block = lambda shape: pl.BlockSpec((1,) + shape, lambda gb: (gb, 0, 0))
    return pl.pallas_call(
        _ssm_expand_kernel,
        out_shape=[jax.ShapeDtypeStruct((nb, SLAB, SLAB), BF16),
                   jax.ShapeDtypeStruct((nb, SLAB, 2 * STATE_COLS), BF16),
                   jax.ShapeDtypeStruct((nb, SLAB, 2 * STATE_COLS), BF16)],
        grid=(nb,),
        in_specs=[per_block((LANES, SLAB))] + [per_block((SLAB, 2 * STATE))] * 4,
        out_specs=[per_block((SLAB, SLAB)), per_block((SLAB, 2 * STATE_COLS)),
                   per_block((SLAB, 2 * STATE_COLS))],
        compiler_params=pltpu.CompilerParams(dimension_semantics=("arbitrary",),
                                             vmem_limit_bytes=VMEM_LIMIT),
        name="ssm_expand",
    )(*prep[:5])


def _ssm_state_kernel(u_ref, v_ref, sr_ref, si_ref):
    s = jnp.dot(u_ref[...], v_ref[0], preferred_element_type=F32)
    sr_ref[...] = s[:, :STATE_COLS]
    si_ref[...] = s[:, STATE_COLS:]


def _ssm_state(u2, v_blk):
    n = u2.shape[0]
    state = pl.BlockSpec((SSM_TILE, STATE_COLS), lambda gb, i: (i, gb))
    return pl.pallas_call(
        _ssm_state_kernel,
        out_shape=[jax.ShapeDtypeStruct((n, N_GROUP_BLOCKS * STATE_COLS), F32)] * 2,
        grid=(N_GROUP_BLOCKS, n // SSM_TILE),
        in_specs=[pl.BlockSpec((SSM_TILE, SLAB), lambda gb, i: (i, gb)),
                  pl.BlockSpec((1, SLAB, 2 * STATE_COLS), lambda gb, i: (gb, 0, 0))],
        out_specs=[state, state],
        compiler_params=pltpu.CompilerParams(dimension_semantics=("arbitrary", "arbitrary"),
                                             vmem_limit_bytes=VMEM_LIMIT),
        name="ssm_state",
    )(u2, v_blk)


def _ssm_scan_kernel(sr_ref, si_ref, ar_ref, ai_ref, hr_ref, hi_ref, cr_scr, ci_scr, s_scr, h_scr):
    @pl.when(pl.program_id(0) == 0)
    def _():
        cr_scr[...] = jnp.zeros_like(cr_scr)
        ci_scr[...] = jnp.zeros_like(ci_scr)

    ar, ai = ar_ref[...], ai_ref[...]
    nb = sr_ref.shape[0]
    tiles = ar.shape[0]
    for b in range(nb):
        for part, ref in enumerate((sr_ref, si_ref)):
            for r in range(tiles):
                s_scr[part, b, pl.ds(r, SCAN_TILE, stride=tiles), :] = ref[b, :, r * LANES:(r + 1) * LANES]

    def body(k, carry):
        rows = pl.ds(pl.multiple_of(k * tiles, tiles), tiles)
        out = []
        for b in range(nb):
            hr, hi = carry[2 * b], carry[2 * b + 1]
            h_scr[0, b, rows, :] = hr
            h_scr[1, b, rows, :] = hi
            out += [ar * hr - ai * hi + s_scr[0, b, rows, :], ar * hi + ai * hr + s_scr[1, b, rows, :]]
        return tuple(out)

    init = tuple(scr[b] for b in range(nb) for scr in (cr_scr, ci_scr))
    fin = lax.fori_loop(0, SCAN_TILE, body, init, unroll=8)
    for b in range(nb):
        cr_scr[b] = fin[2 * b]
        ci_scr[b] = fin[2 * b + 1]
        for part, ref in enumerate((hr_ref, hi_ref)):
            for r in range(tiles):
                ref[b, :, r * LANES:(r + 1) * LANES] = (
                    h_scr[part, b, pl.ds(r, SCAN_TILE, stride=tiles), :].astype(BF16))


def _ssm_scan(s_re, s_im, a_re, a_im):
    bsz, k, cols = s_re.shape
    rows, lanes = a_re.shape
    blk = pl.BlockSpec((bsz, SCAN_TILE, cols), lambda i: (0, i, 0))
    return pl.pallas_call(
        _ssm_scan_kernel,
        out_shape=[jax.ShapeDtypeStruct(s_re.shape, BF16)] * 2,
        grid=(k // SCAN_TILE,),
        in_specs=[blk, blk, _resident((rows, lanes)), _resident((rows, lanes))],
        out_specs=[blk, blk],
        scratch_shapes=[pltpu.VMEM((bsz, rows, lanes), F32)] * 2
        + [pltpu.VMEM((2, bsz, SCAN_TILE * rows, lanes), F32)] * 2,
        compiler_params=pltpu.CompilerParams(dimension_semantics=("arbitrary",)),
        name="ssm_scan",
    )(s_re, s_im, a_re, a_im)


def _ssm_out_kernel(u_ref, hr_ref, hi_ref, t_ref, wt_ref, y_ref):
    pair = 2 * LANES
    contract_last = (((1,), (1,)), ((), ()))
    for tb in range(SLAB // pair):
        k_rows = (tb + 1) * pair
        cols = slice(tb * pair, (tb + 1) * pair)
        y = jnp.dot(u_ref[:, :k_rows], t_ref[0, :k_rows, cols], preferred_element_type=F32)
        y += lax.dot_general(hr_ref[...], wt_ref[0, cols, :STATE_COLS], contract_last,
                             preferred_element_type=F32)
        y += lax.dot_general(hi_ref[...], wt_ref[0, cols, STATE_COLS:], contract_last,
                             preferred_element_type=F32)
        y_ref[:, cols] = y


def _ssm_out(u2, h_re, h_im, t_mat, wt_blk):
    n = u2.shape[0]
    state = pl.BlockSpec((SSM_TILE, STATE_COLS), lambda gb, i: (i, gb))
    slab = pl.BlockSpec((SSM_TILE, SLAB), lambda gb, i: (i, gb))
    return pl.pallas_call(
        _ssm_out_kernel,
        out_shape=jax.ShapeDtypeStruct(u2.shape, F32),
        grid=(N_GROUP_BLOCKS, n // SSM_TILE),
        in_specs=[slab, state, state,
                  pl.BlockSpec((1, SLAB, SLAB), lambda gb, i: (gb, 0, 0)),
                  pl.BlockSpec((1, SLAB, 2 * STATE_COLS), lambda gb, i: (gb, 0, 0))],
        out_specs=slab,
        compiler_params=pltpu.CompilerParams(dimension_semantics=("arbitrary", "arbitrary"),
                                             vmem_limit_bytes=VMEM_LIMIT),
        name="ssm_out",
    )(u2, h_re, h_im, t_mat, wt_blk)


def _s5(u2, prep, bsz):
    t_mat, v_blk, wt_blk = _chunk_matrices(prep)
    a16 = prep[5].reshape(N_SSM_GROUPS, SSM_GROUP, 2 * STATE)[:, 0]
    n = u2.shape[0]
    kc = n // bsz
    s_re, s_im = _ssm_state(u2, v_blk)
    tile_shape = (N_SSM_GROUPS * STATE // LANES, LANES)
    a_re = a16[:, :STATE].reshape(tile_shape)
    a_im = a16[:, STATE:].reshape(tile_shape)
    h_re, h_im = _ssm_scan(s_re.reshape(bsz, kc, -1), s_im.reshape(bsz, kc, -1), a_re, a_im)
    return _ssm_out(u2, h_re.reshape(n, -1), h_im.reshape(n, -1), t_mat, wt_blk)


def _swap_lane_halves(x):
    packed = pltpu.bitcast(x, jnp.uint32)
    return pltpu.bitcast(pltpu.roll(packed, HEAD_DIM, axis=1), BF16)


def _attn_out_kernel(sink_ref, q_ref, za_ref, kv_ref, kvp_ref, y2_ref, zs_ref,
                     x_ref, gate_ref, ag_ref, gw_ref, gb_ref, sg_ref, wo_ref, fg_ref,
                     o_ref, kv_scr, attn_scr, y_scr, mix_scr, old_mix_scr, out_scr,
                     *, n_tiles, tiles_per_batch):
    step = pl.program_id(0)

    @pl.when(step == 0)
    def _():
        mix_scr[...] = jnp.zeros(mix_scr.shape, BF16)

    old_mix_scr[...] = mix_scr[...]

    first = jnp.minimum(step, n_tiles - 1) % tiles_per_batch == 0
    keep = jnp.where(first, 0.0, 1.0).astype(BF16)
    kv_scr[0:BLOCK, :] = kvp_ref[...] * keep
    kv_scr[BLOCK:, :] = kv_ref[...]

    kj = lax.broadcasted_iota(jnp.int32, (2 * BLOCK, BLOCK), 0)
    qi = lax.broadcasted_iota(jnp.int32, (2 * BLOCK, BLOCK), 1)
    in_band = (kj > qi) & (kj <= qi + BLOCK)
    bias_any = jnp.where(in_band, 0.0, NEG).astype(F32)
    bias_first = jnp.where(in_band & (kj >= BLOCK), 0.0, NEG).astype(F32)
    kv_half = lax.broadcasted_iota(jnp.int32, (2 * BLOCK, LANES), 1) // HEAD_DIM
    pairs = Q_PER_KV // 2

    def both_halves(r0, col):
        tile, half = col // LANES, (col % LANES) // HEAD_DIM
        band = kv_scr[pl.ds(r0, 2 * BLOCK), tile * LANES:(tile + 1) * LANES]
        own = jnp.where(kv_half == half, band, jnp.zeros_like(band))
        other = _swap_lane_halves(own)
        return (own, other) if half == 0 else (other, own)

    contract_last = (((1,), (1,)), ((), ()))
    contract_first = (((0,), (0,)), ((), ()))
    for gb in range(N_GROUP_BLOCKS):
        for tok in range(SSM_CHUNK):
            c0 = gb * SLAB + tok * LANES
            y_scr[gb, pl.ds(tok, CHUNK_ROWS, stride=SSM_CHUNK), :] = y2_ref[:, c0:c0 + LANES]

    def block(j, carry):
        r0 = j * BLOCK
        rows = pl.ds(r0, BLOCK)
        out_scr[j] = jnp.dot(old_mix_scr[...], wo_ref[:, j * OUT_CHUNK:(j + 1) * OUT_CHUNK],
                             preferred_element_type=F32)
        bias = jnp.where(first & (j == 0), bias_first, bias_any)
        for h in range(N_KV_HEADS):
            k_par = both_halves(r0, h * HEAD_DIM)
            v_par = both_halves(r0, KV_WIDTH + h * HEAD_DIM)
            q0 = h * Q_PER_KV * HEAD_DIM
            qs = jnp.concatenate(
                [q_ref[pl.ds(r0, BLOCK), q0 + p * PAIR:q0 + (p + 1) * PAIR] for p in range(pairs)],
                axis=0)
            acc = None
            for par in range(2):
                s_all = lax.dot_general(k_par[par], qs, contract_last, preferred_element_type=F32)
                probs = []
                for p in range(pairs):
                    s = s_all[:, p * BLOCK:(p + 1) * BLOCK] + bias
                    sink = sink_ref[h * Q_PER_KV + 2 * p + par] * LOG2_E
                    m = jnp.maximum(jnp.max(s, axis=0, keepdims=True), sink)
                    e = jnp.exp2(s - m)
                    l = jnp.sum(e, axis=0, keepdims=True) + jnp.exp2(sink - m)
                    probs.append((e * (1.0 / l)).astype(BF16))
                pv = lax.dot_general(jnp.concatenate(probs, axis=1), v_par[par], contract_first,
                                     preferred_element_type=F32)
                acc = pv if acc is None else acc + pv
            for p in range(pairs):
                attn_scr[j, :, q0 + p * PAIR:q0 + (p + 1) * PAIR] = acc[p * BLOCK:(p + 1) * BLOCK]
        a = attn_scr[j]
        a = a * lax.rsqrt(jnp.mean(a * a, axis=-1, keepdims=True) + EPS) * ag_ref[...]
        za = za_ref[rows, :].astype(F32)
        mix_scr[rows, :ATTN_WIDTH] = (a * (za * jax.nn.sigmoid(za))).astype(BF16)
        y = jnp.concatenate([y_scr[gb, rows, :] for gb in range(N_GROUP_BLOCKS)], axis=1)
        y = 0.5 * y * (1.0 + lax.erf(y * math.sqrt(0.5)))
        glu = jnp.dot(y.astype(BF16), gw_ref[...], preferred_element_type=F32) + gb_ref[...]
        y = y * jax.nn.sigmoid(glu)
        y = y * lax.rsqrt(jnp.mean(y * y, axis=-1, keepdims=True) + EPS) * sg_ref[...]
        zs = zs_ref[rows, :].astype(F32)
        mix_scr[rows, ATTN_WIDTH:] = (y * (zs * jax.nn.sigmoid(zs))).astype(BF16)
        return carry

    for j in range(q_ref.shape[0] // BLOCK):
        block(j, 0)

    out = jnp.concatenate([out_scr[c] for c in range(out_scr.shape[0])], axis=1)
    res = x_ref[...] + gate_ref[0] * out
    o_ref[...] = res * lax.rsqrt(jnp.mean(res * res, axis=-1, keepdims=True) + EPS) * fg_ref[...]


def _attn_out(sinks, q, za, kv, y2, zs, x2, gate, attn_gain, glu_w, glu_b, ssm_gain, w_out,
              final_gain, rows_per_batch):
    t, d = x2.shape
    n_tiles = t // ROW_TILE
    tiles_per_batch = rows_per_batch // ROW_TILE
    blocks_per_tile = ROW_TILE // BLOCK
    cur = lambda i: jnp.minimum(i, n_tiles - 1)
    old = lambda i: jnp.maximum(i - 1, 0)
    row = lambda w: pl.BlockSpec((ROW_TILE, w), lambda i: (cur(i), 0))
    prev_block = pl.BlockSpec(
        (BLOCK, 2 * KV_WIDTH), lambda i: (jnp.maximum(cur(i) * blocks_per_tile - 1, 0), 0))
    out_chunks = d // OUT_CHUNK
    return pl.pallas_call(
        functools.partial(_attn_out_kernel, n_tiles=n_tiles, tiles_per_batch=tiles_per_batch),
        out_shape=jax.ShapeDtypeStruct((t, d), F32),
        grid=(n_tiles + 1,),
        in_specs=[pl.BlockSpec(memory_space=pltpu.SMEM),
                  row(ATTN_WIDTH), row(ATTN_WIDTH), row(2 * KV_WIDTH), prev_block,
                  pl.BlockSpec((CHUNK_ROWS, y2.shape[1]), lambda i: (cur(i), 0)),
                  row(SSM_WIDTH),
                  pl.BlockSpec((ROW_TILE, d), lambda i: (old(i), 0)),
                  pl.BlockSpec((1, 1, d), lambda i: (old(i) // tiles_per_batch, 0, 0)),
                  _resident((1, ATTN_WIDTH)),
                  _resident((SSM_WIDTH, SSM_WIDTH)), _resident((1, SSM_WIDTH)),
                  _resident((1, SSM_WIDTH)),
                  _resident((d, d)), _resident((1, d))],
        out_specs=pl.BlockSpec((ROW_TILE, d), lambda i: (old(i), 0)),
        scratch_shapes=[pltpu.VMEM((ROW_TILE + BLOCK, 2 * KV_WIDTH), BF16),
                        pltpu.VMEM((blocks_per_tile, BLOCK, ATTN_WIDTH), F32),
                        pltpu.VMEM((N_GROUP_BLOCKS, ROW_TILE, LANES), F32),
                        pltpu.VMEM((ROW_TILE, d), BF16),
                        pltpu.VMEM((ROW_TILE, d), BF16),
                        pltpu.VMEM((out_chunks, ROW_TILE, OUT_CHUNK), F32)],
        compiler_params=pltpu.CompilerParams(dimension_semantics=("arbitrary",),
                                             vmem_limit_bytes=VMEM_LIMIT),
        name="attn_out",
    )(sinks, q, za, kv, kv, y2, zs, x2, gate, attn_gain, glu_w, glu_b, ssm_gain, w_out, final_gain)


def kernel(x, c, w_ada, b_ada, norm_gain, w_in, b_in, attn_sinks, attn_out_gain, ssm_lambda_re,
           ssm_lambda_im, ssm_log_step, ssm_b_re, ssm_b_im, ssm_c_re, ssm_c_im, ssm_d, glu_w, glu_b,
           ssm_out_gain, w_out, final_gain):
    bsz, seq, d = x.shape
    assert w_ada.shape[0] == 1, "single-layer trunk only"
    x2 = x.reshape(bsz * seq, d)
    mod = _ada(c, w_ada[0], b_ada[0])
    shift, scale, gate = (m.reshape(bsz, 1, d) for m in jnp.split(mod, 3, axis=-1))
    q, kv, za, zs, u2 = _inproj(x2, scale, shift, norm_gain[0].reshape(1, d),
                                w_in[0].astype(BF16), b_in[0].reshape(1, -1), seq)
    prep = _ssm_prep(ssm_lambda_re[0], ssm_lambda_im[0], ssm_log_step[0], ssm_b_re[0],
                     ssm_b_im[0], ssm_c_re[0], ssm_c_im[0], ssm_d[0])
    y2 = _s5(u2, prep, bsz)
    out = _attn_out(attn_sinks[0], q, za, kv, y2, zs, x2, gate,
                    attn_out_gain[0].reshape(1, -1), glu_w[0].astype(BF16),
                    glu_b[0].reshape(1, -1), ssm_out_gain[0].reshape(1, -1),
                    w_out[0].astype(BF16), final_gain.reshape(1, d), seq)
    return out.reshape(bsz, seq, d)
```

```python
import functools
import math

import jax
import jax.numpy as jnp
from jax import lax
from jax.experimental import pallas as pl
from jax.experimental.pallas import tpu as pltpu

F32 = jnp.float32
BF16 = jnp.bfloat16

D_MODEL = 2048
HEAD_DIM = 64
N_Q_HEADS = 24
N_KV_HEADS = 3
Q_PER_KV = 8
ATTN_WIDTH = N_Q_HEADS * HEAD_DIM
BLOCK = 128
SSM_GROUP = 16
SSM_WIDTH = D_MODEL - ATTN_WIDTH
N_SSM_GROUPS = SSM_WIDTH // SSM_GROUP
STATE = 64
EPS = 1e-5
NEG = -1e30
LOG2_E = math.log2(math.e)

LANES = 128
SSM_CHUNK = 16
CHUNK_COLS = SSM_CHUNK * SSM_GROUP
GROUPS_PER_BLOCK = LANES // SSM_GROUP
N_GROUP_BLOCKS = N_SSM_GROUPS // GROUPS_PER_BLOCK
SLAB = SSM_CHUNK * LANES
STATE_COLS = GROUPS_PER_BLOCK * STATE
KV_WIDTH = N_KV_HEADS * HEAD_DIM
PAIR = 2 * HEAD_DIM
Q_SCALE = LOG2_E * HEAD_DIM ** -0.5
PROJ_COLS = 512

VMEM_LIMIT = 56 * 1024 * 1024

ROW_TILE = 512
CHUNK_ROWS = ROW_TILE // SSM_CHUNK
ADA_COLS = 512
OUT_CHUNK = D_MODEL // (ROW_TILE // BLOCK)
SSM_TILE = 512
SCAN_TILE = 128


def _resident(shape):
    return pl.BlockSpec(shape, lambda *_: (0,) * len(shape), pipeline_mode=pl.Buffered(1))


def _ada_kernel(ct_ref, w_ref, b_ref, o_ref):
    ct = ct_ref[...]
    s = ct * jax.nn.sigmoid(ct)
    w = w_ref[...]
    rows = [jnp.sum(s[:, b:b + 1] * w, axis=0, keepdims=True) for b in range(ct.shape[1])]
    o_ref[...] = jnp.concatenate(rows, axis=0) + b_ref[...]


def _ada(c, w_ada, b_ada):
    bsz, d = c.shape
    n = w_ada.shape[1]
    return pl.pallas_call(
        _ada_kernel,
        out_shape=jax.ShapeDtypeStruct((bsz, n), F32),
        grid=(n // ADA_COLS,),
        in_specs=[pl.BlockSpec((d, bsz), lambda i: (0, 0)),
                  pl.BlockSpec((d, ADA_COLS), lambda i: (0, i)),
                  pl.BlockSpec((1, ADA_COLS), lambda i: (0, i))],
        out_specs=pl.BlockSpec((bsz, ADA_COLS), lambda i: (0, i)),
        compiler_params=pltpu.CompilerParams(dimension_semantics=("arbitrary",)),
        name="ada",
    )(c.T, w_ada, b_ada.reshape(1, n))


def _inproj_kernel(x_ref, sc_ref, sh_ref, g_ref, w_ref, b_ref,
                   q_ref, kv_ref, za_ref, zs_ref, u2_ref, h_scr, u_scr):
    x = x_ref[...]
    var = jnp.mean(x * x, axis=-1, keepdims=True)
    gain = g_ref[...] * (1.0 + sc_ref[0])
    h_scr[...] = (x * lax.rsqrt(var + EPS) * gain + sh_ref[0]).astype(BF16)

    def project(col, width):
        acc = jnp.dot(h_scr[...], w_ref[:, col:col + width], preferred_element_type=F32)
        return acc + b_ref[:, col:col + width]

    col_kv, col_za = ATTN_WIDTH, ATTN_WIDTH + 2 * KV_WIDTH
    col_u = col_za + ATTN_WIDTH
    col_zs = col_u + SSM_WIDTH
    for c0 in range(0, ATTN_WIDTH, PROJ_COLS):
        q_ref[:, c0:c0 + PROJ_COLS] = (project(c0, PROJ_COLS) * Q_SCALE).astype(BF16)
        za_ref[:, c0:c0 + PROJ_COLS] = project(col_za + c0, PROJ_COLS).astype(BF16)
    kv_ref[...] = project(col_kv, 2 * KV_WIDTH).astype(BF16)
    zs_ref[...] = project(col_zs, SSM_WIDTH).astype(BF16)
    u = project(col_u, SSM_WIDTH)
    for gb in range(N_GROUP_BLOCKS):
        u_scr[gb] = u[:, gb * LANES:(gb + 1) * LANES]
        for tok in range(SSM_CHUNK):
            c0 = gb * SLAB + tok * LANES
            u2_ref[:, c0:c0 + LANES] = (
                u_scr[gb, pl.ds(tok, CHUNK_ROWS, stride=SSM_CHUNK), :].astype(BF16))


def _inproj(x2, scale, shift, gain, w_cat, b_cat, rows_per_batch):
    t, d = x2.shape
    widths = (ATTN_WIDTH, 2 * KV_WIDTH, ATTN_WIDTH, SSM_WIDTH)
    n_all = w_cat.shape[1]
    tiles_per_batch = rows_per_batch // ROW_TILE
    mod_spec = pl.BlockSpec((1, 1, d), lambda i: (i // tiles_per_batch, 0, 0))
    u2_cols = N_GROUP_BLOCKS * SLAB
    return pl.pallas_call(
        _inproj_kernel,
        out_shape=[jax.ShapeDtypeStruct((t, w), BF16) for w in widths]
        + [jax.ShapeDtypeStruct((t // SSM_CHUNK, u2_cols), BF16)],
        grid=(t // ROW_TILE,),
        in_specs=[pl.BlockSpec((ROW_TILE, d), lambda i: (i, 0)),
                  mod_spec, mod_spec,
                  _resident((1, d)),
                  _resident((d, n_all)),
                  _resident((1, n_all))],
        out_specs=[pl.BlockSpec((ROW_TILE, w), lambda i: (i, 0)) for w in widths]
        + [pl.BlockSpec((CHUNK_ROWS, u2_cols), lambda i: (i, 0))],
        scratch_shapes=[pltpu.VMEM((ROW_TILE, d), BF16),
                        pltpu.VMEM((N_GROUP_BLOCKS, ROW_TILE, LANES), F32)],
        compiler_params=pltpu.CompilerParams(dimension_semantics=("arbitrary",),
                                             vmem_limit_bytes=VMEM_LIMIT),
        name="inproj",
    )(x2, scale, shift, gain, w_cat, b_cat)


def _ssm_prep_kernel(lr_ref, li_ref, ls_ref, br_ref, bi_ref, cr_ref, ci_ref, d_ref,
                     bd_ref, vr_ref, vi_ref, wr_ref, wi_ref, a16_ref, y_scr):
    lr, li = lr_ref[...], li_ref[...]
    step = jnp.exp(ls_ref[...])
    decay = jnp.exp(lr * step)
    ar, ai = decay * jnp.cos(li * step), decay * jnp.sin(li * step)
    den = lr * lr + li * li
    nr, ni = ar - 1.0, ai
    coef_re = (nr * lr + ni * li) / den
    coef_im = (ni * lr - nr * li) / den
    br, bi = br_ref[...], bi_ref[...]
    bbar_re = coef_re * br - coef_im * bi
    bbar_im = coef_re * bi + coef_im * br
    cr, ci = cr_ref[...], ci_ref[...]
    lo = lax.broadcasted_iota(jnp.int32, lr.shape, 2) < STATE
    pr, pi = jnp.ones_like(ar), jnp.zeros_like(ai)
    for tau in range(SSM_CHUNK + 1):
        er, ei = pr * cr - pi * ci, pr * ci + pi * cr
        if tau < SSM_CHUNK:
            y_scr[:, tau * LANES:(tau + 1) * LANES, :] = jnp.where(lo, er, ei)
            s = SSM_CHUNK - 1 - tau
            vr_ref[:, s * LANES:(s + 1) * LANES, :] = pr * bbar_re - pi * bbar_im
            vi_ref[:, s * LANES:(s + 1) * LANES, :] = pr * bbar_im + pi * bbar_re
        if tau >= 1:
            wr_ref[:, (tau - 1) * LANES:tau * LANES, :] = er
            wi_ref[:, (tau - 1) * LANES:tau * LANES, :] = -ei
        if tau == SSM_CHUNK:
            a16_ref[...] = jnp.where(lo, pr, pi)
        pr, pi = pr * ar - pi * ai, pr * ai + pi * ar
    x = jnp.where(lo, bbar_re, -bbar_im)
    row = lax.broadcasted_iota(jnp.int32, (LANES, SLAB), 0)
    col = lax.broadcasted_iota(jnp.int32, (LANES, SLAB), 1)
    same_group = row // SSM_GROUP == (col % LANES) // SSM_GROUP
    diagonal = (lax.broadcasted_iota(jnp.int32, (LANES, LANES), 0)
                == lax.broadcasted_iota(jnp.int32, (LANES, LANES), 1))
    for gb in range(N_GROUP_BLOCKS):
        k = lax.dot_general(x[gb], y_scr[gb], (((1,), (1,)), ((), ())),
                            precision=lax.Precision.HIGHEST, preferred_element_type=F32)
        k = jnp.where(same_group, k, 0.0)
        bd_ref[gb, :, LANES:] = k[:, LANES:]
        bd_ref[gb, :, :LANES] = k[:, :LANES] + jnp.where(diagonal, d_ref[gb], 0.0)


def _ssm_prep(lam_re, lam_im, log_step, b_re, b_im, c_re, c_im, d_skip):
    nb = N_GROUP_BLOCKS
    rows = lambda a: jnp.concatenate([a, a], axis=-1).reshape(nb, LANES, 2 * STATE)
    per_group = lambda a: rows(jnp.repeat(a[:, None, :], SSM_GROUP, axis=1))
    compact = jax.ShapeDtypeStruct((nb, SLAB, 2 * STATE), F32)
    return pl.pallas_call(
        _ssm_prep_kernel,
        out_shape=[jax.ShapeDtypeStruct((nb, LANES, SLAB), F32),
                   compact, compact,
                   compact, compact,
                   jax.ShapeDtypeStruct((nb, LANES, 2 * STATE), F32)],
        scratch_shapes=[pltpu.VMEM((nb, SLAB, 2 * STATE), F32)],
        compiler_params=pltpu.CompilerParams(vmem_limit_bytes=VMEM_LIMIT),
        name="ssm_prep",
    )(per_group(lam_re), per_group(lam_im),
      jnp.broadcast_to(jnp.repeat(log_step, SSM_GROUP).reshape(nb, LANES, 1), (nb, LANES, 2 * STATE)),
      rows(jnp.swapaxes(b_re, 1, 2)), rows(jnp.swapaxes(b_im, 1, 2)), rows(c_re), rows(c_im),
      jnp.broadcast_to(d_skip.reshape(nb, 1, LANES), (nb, LANES, LANES)))


def _ssm_expand_kernel(bd_ref, vr_ref, vi_ref, wr_ref, wi_ref, t_ref, v_ref, wt_ref):
    bd = bd_ref[0].astype(BF16)
    for s in range(SSM_CHUNK):
        rows = slice(s * LANES, (s + 1) * LANES)
        if s:
            t_ref[0, rows, :s * LANES] = jnp.zeros((LANES, s * LANES), BF16)
        t_ref[0, rows, s * LANES:] = bd[:, :(SSM_CHUNK - s) * LANES]
    shape = (SLAB, 2 * STATE)
    row_group = (lax.broadcasted_iota(jnp.int32, shape, 0) % LANES) // SSM_GROUP
    half = lax.broadcasted_iota(jnp.int32, shape, 1) // STATE
    for j in range(STATE_COLS // LANES):
        own = row_group == 2 * j + half
        for src, dst, c0 in ((vr_ref, v_ref, 0), (vi_ref, v_ref, STATE_COLS),
                             (wr_ref, wt_ref, 0), (wi_ref, wt_ref, STATE_COLS)):
            dst[0, :, c0 + j * LANES:c0 + (j + 1) * LANES] = (
                jnp.where(own, src[0], 0.0).astype(BF16))


def _chunk_matrices(prep):
    nb = N_GROUP_BLOCKS
    per_block = lambda shape: pl.BlockSpec((1,) + shape, lambda gb: (gb, 0, 0))
    return pl.pallas_call(
        _ssm_expand_kernel,
        out_shape=[jax.ShapeDtypeStruct((nb, SLAB, SLAB), BF16),
                   jax.ShapeDtypeStruct((nb, SLAB, 2 * STATE_COLS), BF16),
                   jax.ShapeDtypeStruct((nb, SLAB, 2 * STATE_COLS), BF16)],
        grid=(nb,),
        in_specs=[per_block((LANES, SLAB))] + [per_block((SLAB, 2 * STATE))] * 4,
        out_specs=[per_block((SLAB, SLAB)), per_block((SLAB, 2 * STATE_COLS)),
                   per_block((SLAB, 2 * STATE_COLS))],
        compiler_params=pltpu.CompilerParams(dimension_semantics=("arbitrary",),
                                             vmem_limit_bytes=VMEM_LIMIT),
        name="ssm_expand",
    )(*prep[:5])


def _ssm_state_kernel(u_ref, v_ref, sr_ref, si_ref):
    s = jnp.dot(u_ref[...], v_ref[0], preferred_element_type=F32)
    sr_ref[...] = s[:, :STATE_COLS]
    si_ref[...] = s[:, STATE_COLS:]


def _ssm_state(u2, v_blk):
    n = u2.shape[0]
    state = pl.BlockSpec((SSM_TILE, STATE_COLS), lambda gb, i: (i, gb))
    return pl.pallas_call(
        _ssm_state_kernel,
        out_shape=[jax.ShapeDtypeStruct((n, N_GROUP_BLOCKS * STATE_COLS), F32)] * 2,
        grid=(N_GROUP_BLOCKS, n // SSM_TILE),
        in_specs=[pl.BlockSpec((SSM_TILE, SLAB), lambda gb, i: (i, gb)),
                  pl.BlockSpec((1, SLAB, 2 * STATE_COLS), lambda gb, i: (gb, 0, 0))],
        out_specs=[state, state],
        compiler_params=pltpu.CompilerParams(dimension_semantics=("arbitrary", "arbitrary"),
                                             vmem_limit_bytes=VMEM_LIMIT),
        name="ssm_state",
    )(u2, v_blk)


def _ssm_scan_kernel(sr_ref, si_ref, ar_ref, ai_ref, hr_ref, hi_ref, cr_scr, ci_scr, s_scr, h_scr):
    @pl.when(pl.program_id(0) == 0)
    def _():
        cr_scr[...] = jnp.zeros_like(cr_scr)
        ci_scr[...] = jnp.zeros_like(ci_scr)

    ar, ai = ar_ref[...], ai_ref[...]
    nb = sr_ref.shape[0]
    tiles = ar.shape[0]
    for b in range(nb):
        for part, ref in enumerate((sr_ref, si_ref)):
            for r in range(tiles):
                s_scr[part, b, pl.ds(r, SCAN_TILE, stride=tiles), :] = ref[b, :, r * LANES:(r + 1) * LANES]

    def body(k, carry):
        rows = pl.ds(pl.multiple_of(k * tiles, tiles), tiles)
        out = []
        for b in range(nb):
            hr, hi = carry[2 * b], carry[2 * b + 1]
            h_scr[0, b, rows, :] = hr
            h_scr[1, b, rows, :] = hi
            out += [ar * hr - ai * hi + s_scr[0, b, rows, :], ar * hi + ai * hr + s_scr[1, b, rows, :]]
        return tuple(out)

    init = tuple(scr[b] for b in range(nb) for scr in (cr_scr, ci_scr))
    fin = lax.fori_loop(0, SCAN_TILE, body, init, unroll=8)
    for b in range(nb):
        cr_scr[b] = fin[2 * b]
        ci_scr[b] = fin[2 * b + 1]
        for part, ref in enumerate((hr_ref, hi_ref)):
            for r in range(tiles):
                ref[b, :, r * LANES:(r + 1) * LANES] = (
                    h_scr[part, b, pl.ds(r, SCAN_TILE, stride=tiles), :].astype(BF16))


def _ssm_scan(s_re, s_im, a_re, a_im):
    bsz, k, cols = s_re.shape
    rows, lanes = a_re.shape
    blk = pl.BlockSpec((bsz, SCAN_TILE, cols), lambda i: (0, i, 0))
    return pl.pallas_call(
        _ssm_scan_kernel,
        out_shape=[jax.ShapeDtypeStruct(s_re.shape, BF16)] * 2,
        grid=(k // SCAN_TILE,),
        in_specs=[blk, blk, _resident((rows, lanes)), _resident((rows, lanes))],
        out_specs=[blk, blk],
        scratch_shapes=[pltpu.VMEM((bsz, rows, lanes), F32)] * 2
        + [pltpu.VMEM((2, bsz, SCAN_TILE * rows, lanes), F32)] * 2,
        compiler_params=pltpu.CompilerParams(dimension_semantics=("arbitrary",)),
        name="ssm_scan",
    )(s_re, s_im, a_re, a_im)


def _ssm_out_kernel(u_ref, hr_ref, hi_ref, t_ref, wt_ref, y_ref):
    pair = 2 * LANES
    contract_last = (((1,), (1,)), ((), ()))
    for tb in range(SLAB // pair):
        k_rows = (tb + 1) * pair
        cols = slice(tb * pair, (tb + 1) * pair)
        y = jnp.dot(u_ref[:, :k_rows], t_ref[0, :k_rows, cols], preferred_element_type=F32)
        y += lax.dot_general(hr_ref[...], wt_ref[0, cols, :STATE_COLS], contract_last,
                             preferred_element_type=F32)
        y += lax.dot_general(hi_ref[...], wt_ref[0, cols, STATE_COLS:], contract_last,
                             preferred_element_type=F32)
        y_ref[:, cols] = y


def _ssm_out(u2, h_re, h_im, t_mat, wt_blk):
    n = u2.shape[0]
    state = pl.BlockSpec((SSM_TILE, STATE_COLS), lambda gb, i: (i, gb))
    slab = pl.BlockSpec((SSM_TILE, SLAB), lambda gb, i: (i, gb))
    return pl.pallas_call(
        _ssm_out_kernel,
        out_shape=jax.ShapeDtypeStruct(u2.shape, F32),
        grid=(N_GROUP_BLOCKS, n // SSM_TILE),
        in_specs=[slab, state, state,
                  pl.BlockSpec((1, SLAB, SLAB), lambda gb, i: (gb, 0, 0)),
                  pl.BlockSpec((1, SLAB, 2 * STATE_COLS), lambda gb, i: (gb, 0, 0))],
        out_specs=slab,
        compiler_params=pltpu.CompilerParams(dimension_semantics=("arbitrary", "arbitrary"),
                                             vmem_limit_bytes=VMEM_LIMIT),
        name="ssm_out",
    )(u2, h_re, h_im, t_mat, wt_blk)


def _s5(u2, prep, bsz):
    t_mat, v_blk, wt_blk = _chunk_matrices(prep)
    a16 = prep[5].reshape(N_SSM_GROUPS, SSM_GROUP, 2 * STATE)[:, 0]
    n = u2.shape[0]
    kc = n // bsz
    s_re, s_im = _ssm_state(u2, v_blk)
    tile_shape = (N_SSM_GROUPS * STATE // LANES, LANES)
    a_re = a16[:, :STATE].reshape(tile_shape)
    a_im = a16[:, STATE:].reshape(tile_shape)
    h_re, h_im = _ssm_scan(s_re.reshape(bsz, kc, -1), s_im.reshape(bsz, kc, -1), a_re, a_im)
    return _ssm_out(u2, h_re.reshape(n, -1), h_im.reshape(n, -1), t_mat, wt_blk)


def _swap_lane_halves(x):
    packed = pltpu.bitcast(x, jnp.uint32)
    return pltpu.bitcast(pltpu.roll(packed, HEAD_DIM, axis=1), BF16)


def _attn_out_kernel(sink_ref, q_ref, za_ref, kv_ref, kvp_ref, y2_ref, zs_ref,
                     x_ref, gate_ref, ag_ref, gw_ref, gb_ref, sg_ref, wo_ref, fg_ref,
                     o_ref, kv_scr, attn_scr, y_scr, mix_scr, out_scr,
                     *, n_tiles, tiles_per_batch):
    step = pl.program_id(0)
    cur_slot = step % 2
    prev_slot = 1 - cur_slot

    @pl.when(step == 0)
    def _():
        mix_scr[1] = jnp.zeros(mix_scr.shape[1:], BF16)

    first = jnp.minimum(step, n_tiles - 1) % tiles_per_batch == 0
    keep = jnp.where(first, 0.0, 1.0).astype(BF16)
    kv_scr[0:BLOCK, :] = kvp_ref[...] * keep
    kv_scr[BLOCK:, :] = kv_ref[...]

    qi = lax.broadcasted_iota(jnp.int32, (BLOCK, 2 * BLOCK), 0)
    kj = lax.broadcasted_iota(jnp.int32, (BLOCK, 2 * BLOCK), 1)
    in_band = (kj > qi) & (kj <= qi + BLOCK)
    bias_any = jnp.where(in_band, 0.0, NEG).astype(F32)
    bias_first = jnp.where(in_band & (kj >= BLOCK), 0.0, NEG).astype(F32)
    kv_half = lax.broadcasted_iota(jnp.int32, (2 * BLOCK, LANES), 1) // HEAD_DIM
    pairs = Q_PER_KV // 2

    def both_halves(r0, col):
        tile, half = col // LANES, (col % LANES) // HEAD_DIM
        band = kv_scr[pl.ds(r0, 2 * BLOCK), tile * LANES:(tile + 1) * LANES]
        own = jnp.where(kv_half == half, band, jnp.zeros_like(band))
        other = _swap_lane_halves(own)
        return (own, other) if half == 0 else (other, own)

    contract_last = (((1,), (1,)), ((), ()))
    for gb in range(N_GROUP_BLOCKS):
        for tok in range(SSM_CHUNK):
            c0 = gb * SLAB + tok * LANES
            y_scr[gb, pl.ds(tok, CHUNK_ROWS, stride=SSM_CHUNK), :] = y2_ref[:, c0:c0 + LANES]

    def block(j, carry):
        r0 = j * BLOCK
        rows = pl.ds(r0, BLOCK)
        out_scr[j] = jnp.dot(mix_scr[prev_slot], wo_ref[:, j * OUT_CHUNK:(j + 1) * OUT_CHUNK],
                             preferred_element_type=F32)
        bias = jnp.where(first & (j == 0), bias_first, bias_any)
        for h in range(N_KV_HEADS):
            k_par = both_halves(r0, h * HEAD_DIM)
            v_par = both_halves(r0, KV_WIDTH + h * HEAD_DIM)
            q0 = h * Q_PER_KV * HEAD_DIM
            qs = jnp.concatenate(
                [q_ref[pl.ds(r0, BLOCK), q0 + p * PAIR:q0 + (p + 1) * PAIR] for p in range(pairs)],
                axis=0)
            acc = [None] * pairs
            for par in range(2):
                s_all = lax.dot_general(qs, k_par[par], contract_last, preferred_element_type=F32)
                probs, inv = [], []
                for p in range(pairs):
                    s = s_all[p * BLOCK:(p + 1) * BLOCK] + bias
                    sink = sink_ref[h * Q_PER_KV + 2 * p + par] * LOG2_E
                    m = jnp.maximum(jnp.max(s, axis=-1, keepdims=True), sink)
                    e = jnp.exp2(s - m)
                    l = jnp.sum(e, axis=-1, keepdims=True) + jnp.exp2(sink - m)
                    inv.append(1.0 / l)
                    probs.append(e.astype(BF16))
                pv = jnp.dot(jnp.concatenate(probs, axis=0), v_par[par], preferred_element_type=F32)
                for p in range(pairs):
                    o = pv[p * BLOCK:(p + 1) * BLOCK] * inv[p]
                    acc[p] = o if par == 0 else acc[p] + o
            for p in range(pairs):
                attn_scr[:, q0 + p * PAIR:q0 + (p + 1) * PAIR] = acc[p]
        a = attn_scr[...]
        a = a * lax.rsqrt(jnp.mean(a * a, axis=-1, keepdims=True) + EPS) * ag_ref[...]
        za = za_ref[rows, :].astype(F32)
        mix_scr[cur_slot, rows, :ATTN_WIDTH] = (a * (za * jax.nn.sigmoid(za))).astype(BF16)
        y = jnp.concatenate([y_scr[gb, rows, :] for gb in range(N_GROUP_BLOCKS)], axis=1)
        y = 0.5 * y * (1.0 + lax.erf(y * math.sqrt(0.5)))
        glu = jnp.dot(y.astype(BF16), gw_ref[...], preferred_element_type=F32) + gb_ref[...]
        y = y * jax.nn.sigmoid(glu)
        y = y * lax.rsqrt(jnp.mean(y * y, axis=-1, keepdims=True) + EPS) * sg_ref[...]
        zs = zs_ref[rows, :].astype(F32)
        mix_scr[cur_slot, rows, ATTN_WIDTH:] = (y * (zs * jax.nn.sigmoid(zs))).astype(BF16)
        return carry

    for j in range(q_ref.shape[0] // BLOCK):
        block(j, 0)

    out = jnp.concatenate([out_scr[c] for c in range(out_scr.shape[0])], axis=1)
    res = x_ref[...] + gate_ref[0] * out
    o_ref[...] = res * lax.rsqrt(jnp.mean(res * res, axis=-1, keepdims=True) + EPS) * fg_ref[...]


def _attn_out(sinks, q, za, kv, y2, zs, x2, gate, attn_gain, glu_w, glu_b, ssm_gain, w_out,
              final_gain, rows_per_batch):
    t, d = x2.shape
    n_tiles = t // ROW_TILE
    tiles_per_batch = rows_per_batch // ROW_TILE
    blocks_per_tile = ROW_TILE // BLOCK
    cur = lambda i: jnp.minimum(i, n_tiles - 1)
    old = lambda i: jnp.maximum(i - 1, 0)
    row = lambda w: pl.BlockSpec((ROW_TILE, w), lambda i: (cur(i), 0))
    prev_block = pl.BlockSpec(
        (BLOCK, 2 * KV_WIDTH), lambda i: (jnp.maximum(cur(i) * blocks_per_tile - 1, 0), 0))
    out_chunks = d // OUT_CHUNK
    return pl.pallas_call(
        functools.partial(_attn_out_kernel, n_tiles=n_tiles, tiles_per_batch=tiles_per_batch),
        out_shape=jax.ShapeDtypeStruct((t, d), F32),
        grid=(n_tiles + 1,),
        in_specs=[pl.BlockSpec(memory_space=pltpu.SMEM),
                  row(ATTN_WIDTH), row(ATTN_WIDTH), row(2 * KV_WIDTH), prev_block,
                  pl.BlockSpec((CHUNK_ROWS, y2.shape[1]), lambda i: (cur(i), 0)),
                  row(SSM_WIDTH),
                  pl.BlockSpec((ROW_TILE, d), lambda i: (old(i), 0)),
                  pl.BlockSpec((1, 1, d), lambda i: (old(i) // tiles_per_batch, 0, 0)),
                  _resident((1, ATTN_WIDTH)),
                  _resident((SSM_WIDTH, SSM_WIDTH)), _resident((1, SSM_WIDTH)),
                  _resident((1, SSM_WIDTH)),
                  _resident((d, d)), _resident((1, d))],
        out_specs=pl.BlockSpec((ROW_TILE, d), lambda i: (old(i), 0)),
        scratch_shapes=[pltpu.VMEM((ROW_TILE + BLOCK, 2 * KV_WIDTH), BF16),
                        pltpu.VMEM((BLOCK, ATTN_WIDTH), F32),
                        pltpu.VMEM((N_GROUP_BLOCKS, ROW_TILE, LANES), F32),
                        pltpu.VMEM((2, ROW_TILE, d), BF16),
                        pltpu.VMEM((out_chunks, ROW_TILE, OUT_CHUNK), F32)],
        compiler_params=pltpu.CompilerParams(dimension_semantics=("arbitrary",),
                                             vmem_limit_bytes=VMEM_LIMIT),
        name="attn_out",
    )(sinks, q, za, kv, kv, y2, zs, x2, gate, attn_gain, glu_w, glu_b, ssm_gain, w_out, final_gain)


def kernel(x, c, w_ada, b_ada, norm_gain, w_in, b_in, attn_sinks, attn_out_gain, ssm_lambda_re,
           ssm_lambda_im, ssm_log_step, ssm_b_re, ssm_b_im, ssm_c_re, ssm_c_im, ssm_d, glu_w, glu_b,
           ssm_out_gain, w_out, final_gain):
    bsz, seq, d = x.shape
    assert w_ada.shape[0] == 1, "single-layer trunk only"
    x2 = x.reshape(bsz * seq, d)
    mod = _ada(c, w_ada[0], b_ada[0])
    shift, scale, gate = (m.reshape(bsz, 1, d) for m in jnp.split(mod, 3, axis=-1))
    q, kv, za, zs, u2 = _inproj(x2, scale, shift, norm_gain[0].reshape(1, d),
                                w_in[0].astype(BF16), b_in[0].reshape(1, -1), seq)
    prep = _ssm_prep(ssm_lambda_re[0], ssm_lambda_im[0], ssm_log_step[0], ssm_b_re[0],
                     ssm_b_im[0], ssm_c_re[0], ssm_c_im[0], ssm_d[0])
    y2 = _s5(u2, prep, bsz)
    out = _attn_out(attn_sinks[0], q, za, kv, y2, zs, x2, gate,
                    attn_out_gain[0].reshape(1, -1), glu_w[0].astype(BF16),
                    glu_b[0].reshape(1, -1), ssm_out_gain[0].reshape(1, -1),
                    w_out[0].astype(BF16), final_gain.reshape(1, d), seq)
    return out.reshape(bsz, seq, d)
```

```python
import functools
import math

import jax
import jax.numpy as jnp
from jax import lax
from jax.experimental import pallas as pl
from jax.experimental.pallas import tpu as pltpu

F32 = jnp.float32
BF16 = jnp.bfloat16

D_MODEL = 2048
HEAD_DIM = 64
N_Q_HEADS = 24
N_KV_HEADS = 3
Q_PER_KV = 8
ATTN_WIDTH = N_Q_HEADS * HEAD_DIM
BLOCK = 128
SSM_GROUP = 16
SSM_WIDTH = D_MODEL - ATTN_WIDTH
N_SSM_GROUPS = SSM_WIDTH // SSM_GROUP
STATE = 64
EPS = 1e-5
NEG = -1e30
LOG2_E = math.log2(math.e)

LANES = 128
SSM_CHUNK = 16
CHUNK_COLS = SSM_CHUNK * SSM_GROUP
GROUPS_PER_BLOCK = LANES // SSM_GROUP
N_GROUP_BLOCKS = N_SSM_GROUPS // GROUPS_PER_BLOCK
SLAB = SSM_CHUNK * LANES
STATE_COLS = GROUPS_PER_BLOCK * STATE
KV_WIDTH = N_KV_HEADS * HEAD_DIM
PAIR = 2 * HEAD_DIM
Q_SCALE = LOG2_E * HEAD_DIM ** -0.5
PROJ_COLS = 512

VMEM_LIMIT = 56 * 1024 * 1024

ROW_TILE = 512
CHUNK_ROWS = ROW_TILE // SSM_CHUNK
ADA_COLS = 512
OUT_CHUNK = D_MODEL // (ROW_TILE // BLOCK)
SSM_TILE = 512
SCAN_TILE = 128


def _resident(shape):
    return pl.BlockSpec(shape, lambda *_: (0,) * len(shape), pipeline_mode=pl.Buffered(1))


def _ada_kernel(ct_ref, w_ref, b_ref, o_ref):
    ct = ct_ref[...]
    s = ct * jax.nn.sigmoid(ct)
    w = w_ref[...]
    rows = [jnp.sum(s[:, b:b + 1] * w, axis=0, keepdims=True) for b in range(ct.shape[1])]
    o_ref[...] = jnp.concatenate(rows, axis=0) + b_ref[...]


def _ada(c, w_ada, b_ada):
    bsz, d = c.shape
    n = w_ada.shape[1]
    return pl.pallas_call(
        _ada_kernel,
        out_shape=jax.ShapeDtypeStruct((bsz, n), F32),
        grid=(n // ADA_COLS,),
        in_specs=[pl.BlockSpec((d, bsz), lambda i: (0, 0)),
                  pl.BlockSpec((d, ADA_COLS), lambda i: (0, i)),
                  pl.BlockSpec((1, ADA_COLS), lambda i: (0, i))],
        out_specs=pl.BlockSpec((bsz, ADA_COLS), lambda i: (0, i)),
        compiler_params=pltpu.CompilerParams(dimension_semantics=("arbitrary",)),
        name="ada",
    )(c.T, w_ada, b_ada.reshape(1, n))


def _inproj_kernel(x_ref, sc_ref, sh_ref, g_ref, w_ref, b_ref,
                   q_ref, kv_ref, za_ref, zs_ref, u2_ref, h_scr, u_scr):
    x = x_ref[...]
    var = jnp.mean(x * x, axis=-1, keepdims=True)
    gain = g_ref[...] * (1.0 + sc_ref[0])
    h_scr[...] = (x * lax.rsqrt(var + EPS) * gain + sh_ref[0]).astype(BF16)

    def project(col, width):
        acc = jnp.dot(h_scr[...], w_ref[:, col:col + width], preferred_element_type=F32)
        return acc + b_ref[:, col:col + width]

    col_kv, col_za = ATTN_WIDTH, ATTN_WIDTH + 2 * KV_WIDTH
    col_u = col_za + ATTN_WIDTH
    col_zs = col_u + SSM_WIDTH
    for c0 in range(0, ATTN_WIDTH, PROJ_COLS):
        q_ref[:, c0:c0 + PROJ_COLS] = (project(c0, PROJ_COLS) * Q_SCALE).astype(BF16)
        za_ref[:, c0:c0 + PROJ_COLS] = project(col_za + c0, PROJ_COLS).astype(BF16)
    kv_ref[...] = project(col_kv, 2 * KV_WIDTH).astype(BF16)
    zs_ref[...] = project(col_zs, SSM_WIDTH).astype(BF16)
    u = project(col_u, SSM_WIDTH)
    for gb in range(N_GROUP_BLOCKS):
        u_scr[gb] = u[:, gb * LANES:(gb + 1) * LANES]
        for tok in range(SSM_CHUNK):
            c0 = gb * SLAB + tok * LANES
            u2_ref[:, c0:c0 + LANES] = (
                u_scr[gb, pl.ds(tok, CHUNK_ROWS, stride=SSM_CHUNK), :].astype(BF16))


def _inproj(x2, scale, shift, gain, w_cat, b_cat, rows_per_batch):
    t, d = x2.shape
    widths = (ATTN_WIDTH, 2 * KV_WIDTH, ATTN_WIDTH, SSM_WIDTH)
    n_all = w_cat.shape[1]
    tiles_per_batch = rows_per_batch // ROW_TILE
    mod_spec = pl.BlockSpec((1, 1, d), lambda i: (i // tiles_per_batch, 0, 0))
    u2_cols = N_GROUP_BLOCKS * SLAB
    return pl.pallas_call(
        _inproj_kernel,
        out_shape=[jax.ShapeDtypeStruct((t, w), BF16) for w in widths]
        + [jax.ShapeDtypeStruct((t // SSM_CHUNK, u2_cols), BF16)],
        grid=(t // ROW_TILE,),
        in_specs=[pl.BlockSpec((ROW_TILE, d), lambda i: (i, 0)),
                  mod_spec, mod_spec,
                  _resident((1, d)),
                  _resident((d, n_all)),
                  _resident((1, n_all))],
        out_specs=[pl.BlockSpec((ROW_TILE, w), lambda i: (i, 0)) for w in widths]
        + [pl.BlockSpec((CHUNK_ROWS, u2_cols), lambda i: (i, 0))],
        scratch_shapes=[pltpu.VMEM((ROW_TILE, d), BF16),
                        pltpu.VMEM((N_GROUP_BLOCKS, ROW_TILE, LANES), F32)],
        compiler_params=pltpu.CompilerParams(dimension_semantics=("arbitrary",),
                                             vmem_limit_bytes=VMEM_LIMIT),
        name="inproj",
    )(x2, scale, shift, gain, w_cat, b_cat)


def _ssm_prep_kernel(lr_ref, li_ref, ls_ref, br_ref, bi_ref, cr_ref, ci_ref, d_ref,
                     bd_ref, vr_ref, vi_ref, wr_ref, wi_ref, a16_ref, y_scr):
    lr, li = lr_ref[...], li_ref[...]
    step = jnp.exp(ls_ref[...])
    decay = jnp.exp(lr * step)
    ar, ai = decay * jnp.cos(li * step), decay * jnp.sin(li * step)
    den = lr * lr + li * li
    nr, ni = ar - 1.0, ai
    coef_re = (nr * lr + ni * li) / den
    coef_im = (ni * lr - nr * li) / den
    br, bi = br_ref[...], bi_ref[...]
    bbar_re = coef_re * br - coef_im * bi
    bbar_im = coef_re * bi + coef_im * br
    cr, ci = cr_ref[...], ci_ref[...]
    lo = lax.broadcasted_iota(jnp.int32, lr.shape, 2) < STATE
    pr, pi = jnp.ones_like(ar), jnp.zeros_like(ai)
    for tau in range(SSM_CHUNK + 1):
        er, ei = pr * cr - pi * ci, pr * ci + pi * cr
        if tau < SSM_CHUNK:
            y_scr[:, tau * LANES:(tau + 1) * LANES, :] = jnp.where(lo, er, ei)
            s = SSM_CHUNK - 1 - tau
            vr_ref[:, s * LANES:(s + 1) * LANES, :] = pr * bbar_re - pi * bbar_im
            vi_ref[:, s * LANES:(s + 1) * LANES, :] = pr * bbar_im + pi * bbar_re
        if tau >= 1:
            wr_ref[:, (tau - 1) * LANES:tau * LANES, :] = er
            wi_ref[:, (tau - 1) * LANES:tau * LANES, :] = -ei
        if tau == SSM_CHUNK:
            a16_ref[...] = jnp.where(lo, pr, pi)
        pr, pi = pr * ar - pi * ai, pr * ai + pi * ar
    x = jnp.where(lo, bbar_re, -bbar_im)
    row = lax.broadcasted_iota(jnp.int32, (LANES, SLAB), 0)
    col = lax.broadcasted_iota(jnp.int32, (LANES, SLAB), 1)
    same_group = row // SSM_GROUP == (col % LANES) // SSM_GROUP
    diagonal = (lax.broadcasted_iota(jnp.int32, (LANES, LANES), 0)
                == lax.broadcasted_iota(jnp.int32, (LANES, LANES), 1))
    for gb in range(N_GROUP_BLOCKS):
        k = lax.dot_general(x[gb], y_scr[gb], (((1,), (1,)), ((), ())),
                            precision=lax.Precision.HIGHEST, preferred_element_type=F32)
        k = jnp.where(same_group, k, 0.0)
        bd_ref[gb, :, LANES:] = k[:, LANES:]
        bd_ref[gb, :, :LANES] = k[:, :LANES] + jnp.where(diagonal, d_ref[gb], 0.0)


def _ssm_prep(lam_re, lam_im, log_step, b_re, b_im, c_re, c_im, d_skip):
    nb = N_GROUP_BLOCKS
    rows = lambda a: jnp.concatenate([a, a], axis=-1).reshape(nb, LANES, 2 * STATE)
    per_group = lambda a: rows(jnp.repeat(a[:, None, :], SSM_GROUP, axis=1))
    compact = jax.ShapeDtypeStruct((nb, SLAB, 2 * STATE), F32)
    return pl.pallas_call(
        _ssm_prep_kernel,
        out_shape=[jax.ShapeDtypeStruct((nb, LANES, SLAB), F32),
                   compact, compact,
                   compact, compact,
                   jax.ShapeDtypeStruct((nb, LANES, 2 * STATE), F32)],
        scratch_shapes=[pltpu.VMEM((nb, SLAB, 2 * STATE), F32)],
        compiler_params=pltpu.CompilerParams(vmem_limit_bytes=VMEM_LIMIT),
        name="ssm_prep",
    )(per_group(lam_re), per_group(lam_im),
      jnp.broadcast_to(jnp.repeat(log_step, SSM_GROUP).reshape(nb, LANES, 1), (nb, LANES, 2 * STATE)),
      rows(jnp.swapaxes(b_re, 1, 2)), rows(jnp.swapaxes(b_im, 1, 2)), rows(c_re), rows(c_im),
      jnp.broadcast_to(d_skip.reshape(nb, 1, LANES), (nb, LANES, LANES)))


def _ssm_expand_kernel(bd_ref, vr_ref, vi_ref, wr_ref, wi_ref, t_ref, v_ref, wt_ref):
    bd = bd_ref[0].astype(BF16)
    for s in range(SSM_CHUNK):
        rows = slice(s * LANES, (s + 1) * LANES)
        if s:
            t_ref[0, rows, :s * LANES] = jnp.zeros((LANES, s * LANES), BF16)
        t_ref[0, rows, s * LANES:] = bd[:, :(SSM_CHUNK - s) * LANES]
    shape = (SLAB, 2 * STATE)
    row_group = (lax.broadcasted_iota(jnp.int32, shape, 0) % LANES) // SSM_GROUP
    half = lax.broadcasted_iota(jnp.int32, shape, 1) // STATE
    for j in range(STATE_COLS // LANES):
        own = row_group == 2 * j + half
        for src, dst, c0 in ((vr_ref, v_ref, 0), (vi_ref, v_ref, STATE_COLS),
                             (wr_ref, wt_ref, 0), (wi_ref, wt_ref, STATE_COLS)):
            dst[0, :, c0 + j * LANES:c0 + (j + 1) * LANES] = (
                jnp.where(own, src[0], 0.0).astype(BF16))


def _chunk_matrices(prep):
    nb = N_GROUP_BLOCKS
    per_block = lambda shape: pl.BlockSpec((1,) + shape, lambda gb: (gb, 0, 0))
    return pl.pallas_call(
        _ssm_expand_kernel,
        out_shape=[jax.ShapeDtypeStruct((nb, SLAB, SLAB), BF16),
                   jax.ShapeDtypeStruct((nb, SLAB, 2 * STATE_COLS), BF16),
                   jax.ShapeDtypeStruct((nb, SLAB, 2 * STATE_COLS), BF16)],
        grid=(nb,),
        in_specs=[per_block((LANES, SLAB))] + [per_block((SLAB, 2 * STATE))] * 4,
        out_specs=[per_block((SLAB, SLAB)), per_block((SLAB, 2 * STATE_COLS)),
                   per_block((SLAB, 2 * STATE_COLS))],
        compiler_params=pltpu.CompilerParams(dimension_semantics=("arbitrary",),
                                             vmem_limit_bytes=VMEM_LIMIT),
        name="ssm_expand",
    )(*prep[:5])


def _ssm_state_kernel(u_ref, v_ref, sr_ref, si_ref):
    s = jnp.dot(u_ref[...], v_ref[0], preferred_element_type=F32)
    sr_ref[...] = s[:, :STATE_COLS]
    si_ref[...] = s[:, STATE_COLS:]


def _ssm_state(u2, v_blk):
    n = u2.shape[0]
    state = pl.BlockSpec((SSM_TILE, STATE_COLS), lambda gb, i: (i, gb))
    return pl.pallas_call(
        _ssm_state_kernel,
        out_shape=[jax.ShapeDtypeStruct((n, N_GROUP_BLOCKS * STATE_COLS), F32)] * 2,
        grid=(N_GROUP_BLOCKS, n // SSM_TILE),
        in_specs=[pl.BlockSpec((SSM_TILE, SLAB), lambda gb, i: (i, gb)),
                  pl.BlockSpec((1, SLAB, 2 * STATE_COLS), lambda gb, i: (gb, 0, 0))],
        out_specs=[state, state],
        compiler_params=pltpu.CompilerParams(dimension_semantics=("arbitrary", "arbitrary"),
                                             vmem_limit_bytes=VMEM_LIMIT),
        name="ssm_state",
    )(u2, v_blk)


def _ssm_scan_kernel(sr_ref, si_ref, ar_ref, ai_ref, hr_ref, hi_ref, cr_scr, ci_scr, s_scr, h_scr):
    @pl.when(pl.program_id(0) == 0)
    def _():
        cr_scr[...] = jnp.zeros_like(cr_scr)
        ci_scr[...] = jnp.zeros_like(ci_scr)

    ar, ai = ar_ref[...], ai_ref[...]
    nb = sr_ref.shape[0]
    tiles = ar.shape[0]
    for b in range(nb):
        for part, ref in enumerate((sr_ref, si_ref)):
            for r in range(tiles):
                s_scr[part, b, pl.ds(r, SCAN_TILE, stride=tiles), :] = ref[b, :, r * LANES:(r + 1) * LANES]

    def body(k, carry):
        rows = pl.ds(pl.multiple_of(k * tiles, tiles), tiles)
        out = []
        for b in range(nb):
            hr, hi = carry[2 * b], carry[2 * b + 1]
            h_scr[0, b, rows, :] = hr
            h_scr[1, b, rows, :] = hi
            out += [ar * hr - ai * hi + s_scr[0, b, rows, :], ar * hi + ai * hr + s_scr[1, b, rows, :]]
        return tuple(out)

    init = tuple(scr[b] for b in range(nb) for scr in (cr_scr, ci_scr))
    fin = lax.fori_loop(0, SCAN_TILE, body, init, unroll=8)
    for b in range(nb):
        cr_scr[b] = fin[2 * b]
        ci_scr[b] = fin[2 * b + 1]
        for part, ref in enumerate((hr_ref, hi_ref)):
            for r in range(tiles):
                ref[b, :, r * LANES:(r + 1) * LANES] = (
                    h_scr[part, b, pl.ds(r, SCAN_TILE, stride=tiles), :].astype(BF16))


def _ssm_scan(s_re, s_im, a_re, a_im):
    bsz, k, cols = s_re.shape
    rows, lanes = a_re.shape
    blk = pl.BlockSpec((bsz, SCAN_TILE, cols), lambda i: (0, i, 0))
    return pl.pallas_call(
        _ssm_scan_kernel,
        out_shape=[jax.ShapeDtypeStruct(s_re.shape, BF16)] * 2,
        grid=(k // SCAN_TILE,),
        in_specs=[blk, blk, _resident((rows, lanes)), _resident((rows, lanes))],
        out_specs=[blk, blk],
        scratch_shapes=[pltpu.VMEM((bsz, rows, lanes), F32)] * 2
        + [pltpu.VMEM((2, bsz, SCAN_TILE * rows, lanes), F32)] * 2,
        compiler_params=pltpu.CompilerParams(dimension_semantics=("arbitrary",)),
        name="ssm_scan",
    )(s_re, s_im, a_re, a_im)


def _ssm_out_kernel(u_ref, hr_ref, hi_ref, t_ref, wt_ref, y_ref):
    pair = 2 * LANES
    contract_last = (((1,), (1,)), ((), ()))
    for tb in range(SLAB // pair):
        k_rows = (tb + 1) * pair
        cols = slice(tb * pair, (tb + 1) * pair)
        y = jnp.dot(u_ref[:, :k_rows], t_ref[0, :k_rows, cols], preferred_element_type=F32)
        y += lax.dot_general(hr_ref[...], wt_ref[0, cols, :STATE_COLS], contract_last,
                             preferred_element_type=F32)
        y += lax.dot_general(hi_ref[...], wt_ref[0, cols, STATE_COLS:], contract_last,
                             preferred_element_type=F32)
        y_ref[:, cols] = y


def _ssm_out(u2, h_re, h_im, t_mat, wt_blk):
    n = u2.shape[0]
    state = pl.BlockSpec((SSM_TILE, STATE_COLS), lambda gb, i: (i, gb))
    slab = pl.BlockSpec((SSM_TILE, SLAB), lambda gb, i: (i, gb))
    return pl.pallas_call(
        _ssm_out_kernel,
        out_shape=jax.ShapeDtypeStruct(u2.shape, F32),
        grid=(N_GROUP_BLOCKS, n // SSM_TILE),
        in_specs=[slab, state, state,
                  pl.BlockSpec((1, SLAB, SLAB), lambda gb, i: (gb, 0, 0)),
                  pl.BlockSpec((1, SLAB, 2 * STATE_COLS), lambda gb, i: (gb, 0, 0))],
        out_specs=slab,
        compiler_params=pltpu.CompilerParams(dimension_semantics=("arbitrary", "arbitrary"),
                                             vmem_limit_bytes=VMEM_LIMIT),
        name="ssm_out",
    )(u2, h_re, h_im, t_mat, wt_blk)


def _s5(u2, prep, bsz):
    t_mat, v_blk, wt_blk = _chunk_matrices(prep)
    a16 = prep[5].reshape(N_SSM_GROUPS, SSM_GROUP, 2 * STATE)[:, 0]
    n = u2.shape[0]
    kc = n // bsz
    s_re, s_im = _ssm_state(u2, v_blk)
    tile_shape = (N_SSM_GROUPS * STATE // LANES, LANES)
    a_re = a16[:, :STATE].reshape(tile_shape)
    a_im = a16[:, STATE:].reshape(tile_shape)
    h_re, h_im = _ssm_scan(s_re.reshape(bsz, kc, -1), s_im.reshape(bsz, kc, -1), a_re, a_im)
    return _ssm_out(u2, h_re.reshape(n, -1), h_im.reshape(n, -1), t_mat, wt_blk)


def _swap_lane_halves(x):
    packed = pltpu.bitcast(x, jnp.uint32)
    return pltpu.bitcast(pltpu.roll(packed, HEAD_DIM, axis=1), BF16)


def _attn_out_kernel(sink_ref, q_ref, za_ref, kv_ref, kvp_ref, y2_ref, zs_ref,
                     x_ref, gate_ref, ag_ref, gw_ref, gb_ref, sg_ref, wo_ref, fg_ref,
                     o_ref, kv_scr, attn_scr, y_scr, mix_scr, out_scr,
                     *, n_tiles, tiles_per_batch):
    step = pl.program_id(0)
    cur_slot = step % 2
    prev_slot = 1 - cur_slot

    @pl.when(step == 0)
    def _():
        mix_scr[1] = jnp.zeros(mix_scr.shape[1:], BF16)

    first = jnp.minimum(step, n_tiles - 1) % tiles_per_batch == 0
    keep = jnp.where(first, 0.0, 1.0).astype(BF16)
    kv_scr[0:BLOCK, :] = kvp_ref[...] * keep
    kv_scr[BLOCK:, :] = kv_ref[...]

    older = (lax.broadcasted_iota(jnp.int32, (BLOCK, BLOCK), 1)
             > lax.broadcasted_iota(jnp.int32, (BLOCK, BLOCK), 0))
    kv_half = lax.broadcasted_iota(jnp.int32, (2 * BLOCK, LANES), 1) // HEAD_DIM
    pairs = Q_PER_KV // 2

    def both_halves(r0, col):
        tile, half = col // LANES, (col % LANES) // HEAD_DIM
        band = kv_scr[pl.ds(r0, 2 * BLOCK), tile * LANES:(tile + 1) * LANES]
        own = jnp.where(kv_half == half, band, jnp.zeros_like(band))
        other = _swap_lane_halves(own)
        return (own, other) if half == 0 else (other, own)

    contract_last = (((1,), (1,)), ((), ()))
    for gb in range(N_GROUP_BLOCKS):
        for tok in range(SSM_CHUNK):
            c0 = gb * SLAB + tok * LANES
            y_scr[gb, pl.ds(tok, CHUNK_ROWS, stride=SSM_CHUNK), :] = y2_ref[:, c0:c0 + LANES]

    def block(j, carry):
        r0 = j * BLOCK
        rows = pl.ds(r0, BLOCK)
        out_scr[j] = jnp.dot(mix_scr[prev_slot], wo_ref[:, j * OUT_CHUNK:(j + 1) * OUT_CHUNK],
                             preferred_element_type=F32)
        no_past = jnp.where(first, NEG, 0.0) if j == 0 else None
        for h in range(N_KV_HEADS):
            k_par = both_halves(r0, h * HEAD_DIM)
            v_par = both_halves(r0, KV_WIDTH + h * HEAD_DIM)
            q0 = h * Q_PER_KV * HEAD_DIM
            qs = jnp.concatenate(
                [q_ref[pl.ds(r0, BLOCK), q0 + p * PAIR:q0 + (p + 1) * PAIR] for p in range(pairs)],
                axis=0)
            acc = [None] * pairs
            for par in range(2):
                s_all = lax.dot_general(qs, k_par[par], contract_last, preferred_element_type=F32)
                probs, inv = [], []
                for p in range(pairs):
                    past = s_all[p * BLOCK:(p + 1) * BLOCK, :BLOCK]
                    if no_past is not None:
                        past = past + no_past
                    s = jnp.where(older, past, s_all[p * BLOCK:(p + 1) * BLOCK, BLOCK:])
                    sink = sink_ref[h * Q_PER_KV + 2 * p + par] * LOG2_E
                    m = jnp.max(s, axis=-1, keepdims=True)
                    e = jnp.exp2(s - m)
                    l = jnp.sum(e, axis=-1, keepdims=True) + jnp.exp2(sink - m)
                    inv.append(1.0 / l)
                    e = e.astype(BF16)
                    zero = jnp.zeros_like(e)
                    probs.append(jnp.concatenate(
                        [jnp.where(older, e, zero), jnp.where(older, zero, e)], axis=1))
                pv = jnp.dot(jnp.concatenate(probs, axis=0), v_par[par], preferred_element_type=F32)
                for p in range(pairs):
                    o = pv[p * BLOCK:(p + 1) * BLOCK] * inv[p]
                    acc[p] = o if par == 0 else acc[p] + o
            for p in range(pairs):
                attn_scr[:, q0 + p * PAIR:q0 + (p + 1) * PAIR] = acc[p]
        a = attn_scr[...]
        a = a * lax.rsqrt(jnp.mean(a * a, axis=-1, keepdims=True) + EPS) * ag_ref[...]
        za = za_ref[rows, :]
        mix_scr[cur_slot, rows, :ATTN_WIDTH] = a.astype(BF16) * (za * jax.nn.sigmoid(za))
        y = jnp.concatenate([y_scr[gb, rows, :] for gb in range(N_GROUP_BLOCKS)], axis=1)
        y = 0.5 * y * (1.0 + lax.erf(y * math.sqrt(0.5)))
        glu = jnp.dot(y.astype(BF16), gw_ref[...], preferred_element_type=F32) + gb_ref[...]
        y = y * jax.nn.sigmoid(glu)
        y = y * lax.rsqrt(jnp.mean(y * y, axis=-1, keepdims=True) + EPS) * sg_ref[...]
        zs = zs_ref[rows, :]
        mix_scr[cur_slot, rows, ATTN_WIDTH:] = y.astype(BF16) * (zs * jax.nn.sigmoid(zs))
        return carry

    for j in range(q_ref.shape[0] // BLOCK):
        block(j, 0)

    out = jnp.concatenate([out_scr[c] for c in range(out_scr.shape[0])], axis=1)
    res = x_ref[...] + gate_ref[0] * out
    o_ref[...] = res * lax.rsqrt(jnp.mean(res * res, axis=-1, keepdims=True) + EPS) * fg_ref[...]


def _attn_out(sinks, q, za, kv, y2, zs, x2, gate, attn_gain, glu_w, glu_b, ssm_gain, w_out,
              final_gain, rows_per_batch):
    t, d = x2.shape
    n_tiles = t // ROW_TILE
    tiles_per_batch = rows_per_batch // ROW_TILE
    blocks_per_tile = ROW_TILE // BLOCK
    cur = lambda i: jnp.minimum(i, n_tiles - 1)
    old = lambda i: jnp.maximum(i - 1, 0)
    row = lambda w: pl.BlockSpec((ROW_TILE, w), lambda i: (cur(i), 0))
    prev_block = pl.BlockSpec(
        (BLOCK, 2 * KV_WIDTH), lambda i: (jnp.maximum(cur(i) * blocks_per_tile - 1, 0), 0))
    out_chunks = d // OUT_CHUNK
    return pl.pallas_call(
        functools.partial(_attn_out_kernel, n_tiles=n_tiles, tiles_per_batch=tiles_per_batch),
        out_shape=jax.ShapeDtypeStruct((t, d), F32),
        grid=(n_tiles + 1,),
        in_specs=[pl.BlockSpec(memory_space=pltpu.SMEM),
                  row(ATTN_WIDTH), row(ATTN_WIDTH), row(2 * KV_WIDTH), prev_block,
                  pl.BlockSpec((CHUNK_ROWS, y2.shape[1]), lambda i: (cur(i), 0)),
                  row(SSM_WIDTH),
                  pl.BlockSpec((ROW_TILE, d), lambda i: (old(i), 0)),
                  pl.BlockSpec((1, 1, d), lambda i: (old(i) // tiles_per_batch, 0, 0)),
                  _resident((1, ATTN_WIDTH)),
                  _resident((SSM_WIDTH, SSM_WIDTH)), _resident((1, SSM_WIDTH)),
                  _resident((1, SSM_WIDTH)),
                  _resident((d, d)), _resident((1, d))],
        out_specs=pl.BlockSpec((ROW_TILE, d), lambda i: (old(i), 0)),
        scratch_shapes=[pltpu.VMEM((ROW_TILE + BLOCK, 2 * KV_WIDTH), BF16),
                        pltpu.VMEM((BLOCK, ATTN_WIDTH), F32),
                        pltpu.VMEM((N_GROUP_BLOCKS, ROW_TILE, LANES), F32),
                        pltpu.VMEM((2, ROW_TILE, d), BF16),
                        pltpu.VMEM((out_chunks, ROW_TILE, OUT_CHUNK), F32)],
        compiler_params=pltpu.CompilerParams(dimension_semantics=("arbitrary",),
                                             vmem_limit_bytes=VMEM_LIMIT),
        name="attn_out",
    )(sinks, q, za, kv, kv, y2, zs, x2, gate, attn_gain, glu_w, glu_b, ssm_gain, w_out, final_gain)


def kernel(x, c, w_ada, b_ada, norm_gain, w_in, b_in, attn_sinks, attn_out_gain, ssm_lambda_re,
           ssm_lambda_im, ssm_log_step, ssm_b_re, ssm_b_im, ssm_c_re, ssm_c_im, ssm_d, glu_w, glu_b,
           ssm_out_gain, w_out, final_gain):
    bsz, seq, d = x.shape
    assert w_ada.shape[0] == 1, "single-layer trunk only"
    x2 = x.reshape(bsz * seq, d)
    mod = _ada(c, w_ada[0], b_ada[0])
    shift, scale, gate = (m.reshape(bsz, 1, d) for m in jnp.split(mod, 3, axis=-1))
    q, kv, za, zs, u2 = _inproj(x2, scale, shift, norm_gain[0].reshape(1, d),
                                w_in[0].astype(BF16), b_in[0].reshape(1, -1), seq)
    prep = _ssm_prep(ssm_lambda_re[0], ssm_lambda_im[0], ssm_log_step[0], ssm_b_re[0],
                     ssm_b_im[0], ssm_c_re[0], ssm_c_im[0], ssm_d[0])
    y2 = _s5(u2, prep, bsz)
    out = _attn_out(attn_sinks[0], q, za, kv, y2, zs, x2, gate,
                    attn_out_gain[0].reshape(1, -1), glu_w[0].astype(BF16),
                    glu_b[0].reshape(1, -1), ssm_out_gain[0].reshape(1, -1),
                    w_out[0].astype(BF16), final_gain.reshape(1, d), seq)
    return out.reshape(bsz, seq, d)
```

```python
import functools
import math

import jax
import jax.numpy as jnp
from jax import lax
from jax.experimental import pallas as pl
from jax.experimental.pallas import tpu as pltpu

F32 = jnp.float32
BF16 = jnp.bfloat16

D_MODEL = 2048
HEAD_DIM = 64
N_Q_HEADS = 24
N_KV_HEADS = 3
Q_PER_KV = 8
ATTN_WIDTH = N_Q_HEADS * HEAD_DIM
BLOCK = 128
SSM_GROUP = 16
SSM_WIDTH = D_MODEL - ATTN_WIDTH
N_SSM_GROUPS = SSM_WIDTH // SSM_GROUP
STATE = 64
EPS = 1e-5
NEG = -1e30
LOG2_E = math.log2(math.e)

LANES = 128
SSM_CHUNK = 16
CHUNK_COLS = SSM_CHUNK * SSM_GROUP
GROUPS_PER_BLOCK = LANES // SSM_GROUP
N_GROUP_BLOCKS = N_SSM_GROUPS // GROUPS_PER_BLOCK
SLAB = SSM_CHUNK * LANES
STATE_COLS = GROUPS_PER_BLOCK * STATE
KV_WIDTH = N_KV_HEADS * HEAD_DIM
PAIR = 2 * HEAD_DIM
Q_SCALE = LOG2_E * HEAD_DIM ** -0.5
PROJ_COLS = 512

VMEM_LIMIT = 56 * 1024 * 1024

ROW_TILE = 512
CHUNK_ROWS = ROW_TILE // SSM_CHUNK
ADA_COLS = 1024
OUT_CHUNK = D_MODEL // (ROW_TILE // BLOCK)
SSM_TILE = 512
SCAN_TILE = 128


def _resident(shape):
    return pl.BlockSpec(shape, lambda *_: (0,) * len(shape), pipeline_mode=pl.Buffered(1))


def _ada_kernel(ct_ref, w_ref, b_ref, o_ref):
    ct = ct_ref[...]
    s = ct * jax.nn.sigmoid(ct)
    w = w_ref[...]
    rows = [jnp.sum(s[:, b:b + 1] * w, axis=0, keepdims=True) for b in range(ct.shape[1])]
    o_ref[...] = jnp.concatenate(rows, axis=0) + b_ref[...]


def _ada(c, w_ada, b_ada):
    bsz, d = c.shape
    n = w_ada.shape[1]
    return pl.pallas_call(
        _ada_kernel,
        out_shape=jax.ShapeDtypeStruct((bsz, n), F32),
        grid=(n // ADA_COLS,),
        in_specs=[pl.BlockSpec((d, bsz), lambda i: (0, 0)),
                  pl.BlockSpec((d, ADA_COLS), lambda i: (0, i)),
                  pl.BlockSpec((1, ADA_COLS), lambda i: (0, i))],
        out_specs=pl.BlockSpec((bsz, ADA_COLS), lambda i: (0, i)),
        compiler_params=pltpu.CompilerParams(dimension_semantics=("arbitrary",)),
        name="ada",
    )(c.T, w_ada, b_ada.reshape(1, n))


def _inproj_kernel(x_ref, sc_ref, sh_ref, g_ref, w_ref, b_ref,
                   q_ref, kv_ref, za_ref, zs_ref, u2_ref, h_scr, u_scr):
    x = x_ref[...]
    var = jnp.mean(x * x, axis=-1, keepdims=True)
    gain = g_ref[...] * (1.0 + sc_ref[0])
    h_scr[...] = (x * lax.rsqrt(var + EPS) * gain + sh_ref[0]).astype(BF16)

    def project(col, width):
        acc = jnp.dot(h_scr[...], w_ref[:, col:col + width], preferred_element_type=F32)
        return acc + b_ref[:, col:col + width]

    col_kv, col_za = ATTN_WIDTH, ATTN_WIDTH + 2 * KV_WIDTH
    col_u = col_za + ATTN_WIDTH
    col_zs = col_u + SSM_WIDTH
    for c0 in range(0, ATTN_WIDTH, PROJ_COLS):
        q_ref[:, c0:c0 + PROJ_COLS] = (project(c0, PROJ_COLS) * Q_SCALE).astype(BF16)
        za_ref[:, c0:c0 + PROJ_COLS] = project(col_za + c0, PROJ_COLS).astype(BF16)
    kv_ref[...] = project(col_kv, 2 * KV_WIDTH).astype(BF16)
    zs_ref[...] = project(col_zs, SSM_WIDTH).astype(BF16)
    u = project(col_u, SSM_WIDTH)
    for gb in range(N_GROUP_BLOCKS):
        u_scr[gb] = u[:, gb * LANES:(gb + 1) * LANES]
        for tok in range(SSM_CHUNK):
            c0 = gb * SLAB + tok * LANES
            u2_ref[:, c0:c0 + LANES] = (
                u_scr[gb, pl.ds(tok, CHUNK_ROWS, stride=SSM_CHUNK), :].astype(BF16))


def _inproj(x2, scale, shift, gain, w_cat, b_cat, rows_per_batch):
    t, d = x2.shape
    widths = (ATTN_WIDTH, 2 * KV_WIDTH, ATTN_WIDTH, SSM_WIDTH)
    n_all = w_cat.shape[1]
    tiles_per_batch = rows_per_batch // ROW_TILE
    mod_spec = pl.BlockSpec((1, 1, d), lambda i: (i // tiles_per_batch, 0, 0))
    u2_cols = N_GROUP_BLOCKS * SLAB
    return pl.pallas_call(
        _inproj_kernel,
        out_shape=[jax.ShapeDtypeStruct((t, w), BF16) for w in widths]
        + [jax.ShapeDtypeStruct((t // SSM_CHUNK, u2_cols), BF16)],
        grid=(t // ROW_TILE,),
        in_specs=[pl.BlockSpec((ROW_TILE, d), lambda i: (i, 0)),
                  mod_spec, mod_spec,
                  _resident((1, d)),
                  _resident((d, n_all)),
                  _resident((1, n_all))],
        out_specs=[pl.BlockSpec((ROW_TILE, w), lambda i: (i, 0)) for w in widths]
        + [pl.BlockSpec((CHUNK_ROWS, u2_cols), lambda i: (i, 0))],
        scratch_shapes=[pltpu.VMEM((ROW_TILE, d), BF16),
                        pltpu.VMEM((N_GROUP_BLOCKS, ROW_TILE, LANES), F32)],
        compiler_params=pltpu.CompilerParams(dimension_semantics=("arbitrary",),
                                             vmem_limit_bytes=VMEM_LIMIT),
        name="inproj",
    )(x2, scale, shift, gain, w_cat, b_cat)


def _ssm_prep_kernel(lr_ref, li_ref, ls_ref, br_ref, bi_ref, cr_ref, ci_ref, d_ref,
                     bd_ref, vr_ref, vi_ref, wr_ref, wi_ref, a16_ref, y_scr):
    lr, li = lr_ref[...], li_ref[...]
    step = jnp.exp(ls_ref[...])
    decay = jnp.exp(lr * step)
    ar, ai = decay * jnp.cos(li * step), decay * jnp.sin(li * step)
    den = lr * lr + li * li
    nr, ni = ar - 1.0, ai
    coef_re = (nr * lr + ni * li) / den
    coef_im = (ni * lr - nr * li) / den
    br, bi = br_ref[...], bi_ref[...]
    bbar_re = coef_re * br - coef_im * bi
    bbar_im = coef_re * bi + coef_im * br
    cr, ci = cr_ref[...], ci_ref[...]
    lo = lax.broadcasted_iota(jnp.int32, lr.shape, 2) < STATE
    pr, pi = jnp.ones_like(ar), jnp.zeros_like(ai)
    for tau in range(SSM_CHUNK + 1):
        er, ei = pr * cr - pi * ci, pr * ci + pi * cr
        if tau < SSM_CHUNK:
            y_scr[:, tau * LANES:(tau + 1) * LANES, :] = jnp.where(lo, er, ei)
            s = SSM_CHUNK - 1 - tau
            vr_ref[:, s * LANES:(s + 1) * LANES, :] = pr * bbar_re - pi * bbar_im
            vi_ref[:, s * LANES:(s + 1) * LANES, :] = pr * bbar_im + pi * bbar_re
        if tau >= 1:
            wr_ref[:, (tau - 1) * LANES:tau * LANES, :] = er
            wi_ref[:, (tau - 1) * LANES:tau * LANES, :] = -ei
        if tau == SSM_CHUNK:
            a16_ref[...] = jnp.where(lo, pr, pi)
        pr, pi = pr * ar - pi * ai, pr * ai + pi * ar
    x = jnp.where(lo, bbar_re, -bbar_im)
    row = lax.broadcasted_iota(jnp.int32, (LANES, SLAB), 0)
    col = lax.broadcasted_iota(jnp.int32, (LANES, SLAB), 1)
    same_group = row // SSM_GROUP == (col % LANES) // SSM_GROUP
    diagonal = (lax.broadcasted_iota(jnp.int32, (LANES, LANES), 0)
                == lax.broadcasted_iota(jnp.int32, (LANES, LANES), 1))
    for gb in range(N_GROUP_BLOCKS):
        k = lax.dot_general(x[gb], y_scr[gb], (((1,), (1,)), ((), ())),
                            precision=lax.Precision.HIGHEST, preferred_element_type=F32)
        k = jnp.where(same_group, k, 0.0)
        bd_ref[gb, :, LANES:] = k[:, LANES:]
        bd_ref[gb, :, :LANES] = k[:, :LANES] + jnp.where(diagonal, d_ref[gb], 0.0)


def _ssm_prep(lam_re, lam_im, log_step, b_re, b_im, c_re, c_im, d_skip):
    nb = N_GROUP_BLOCKS
    rows = lambda a: jnp.concatenate([a, a], axis=-1).reshape(nb, LANES, 2 * STATE)
    per_group = lambda a: rows(jnp.repeat(a[:, None, :], SSM_GROUP, axis=1))
    compact = jax.ShapeDtypeStruct((nb, SLAB, 2 * STATE), F32)
    return pl.pallas_call(
        _ssm_prep_kernel,
        out_shape=[jax.ShapeDtypeStruct((nb, LANES, SLAB), F32),
                   compact, compact,
                   compact, compact,
                   jax.ShapeDtypeStruct((nb, LANES, 2 * STATE), F32)],
        scratch_shapes=[pltpu.VMEM((nb, SLAB, 2 * STATE), F32)],
        compiler_params=pltpu.CompilerParams(vmem_limit_bytes=VMEM_LIMIT),
        name="ssm_prep",
    )(per_group(lam_re), per_group(lam_im),
      jnp.broadcast_to(jnp.repeat(log_step, SSM_GROUP).reshape(nb, LANES, 1), (nb, LANES, 2 * STATE)),
      rows(jnp.swapaxes(b_re, 1, 2)), rows(jnp.swapaxes(b_im, 1, 2)), rows(c_re), rows(c_im),
      jnp.broadcast_to(d_skip.reshape(nb, 1, LANES), (nb, LANES, LANES)))


def _expand_block_diagonal(re_ref, im_ref, dst):
    shape = (SLAB, 2 * STATE)
    row_group = (lax.broadcasted_iota(jnp.int32, shape, 0) % LANES) // SSM_GROUP
    half = lax.broadcasted_iota(jnp.int32, shape, 1) // STATE
    for j in range(STATE_COLS // LANES):
        own = row_group == 2 * j + half
        for src, c0 in ((re_ref, 0), (im_ref, STATE_COLS)):
            dst[:, c0 + j * LANES:c0 + (j + 1) * LANES] = jnp.where(own, src[0], 0.0).astype(BF16)


def _ssm_state_kernel(u_ref, vr_ref, vi_ref, sr_ref, si_ref, v_scr):
    @pl.when(pl.program_id(1) == 0)
    def _():
        _expand_block_diagonal(vr_ref, vi_ref, v_scr)

    s = jnp.dot(u_ref[...], v_scr[...], preferred_element_type=F32)
    sr_ref[...] = s[:, :STATE_COLS]
    si_ref[...] = s[:, STATE_COLS:]


def _ssm_state(u2, v_re, v_im):
    n = u2.shape[0]
    state = pl.BlockSpec((SSM_TILE, STATE_COLS), lambda gb, i: (i, gb))
    compact = pl.BlockSpec((1, SLAB, 2 * STATE), lambda gb, i: (gb, 0, 0))
    return pl.pallas_call(
        _ssm_state_kernel,
        out_shape=[jax.ShapeDtypeStruct((n, N_GROUP_BLOCKS * STATE_COLS), F32)] * 2,
        grid=(N_GROUP_BLOCKS, n // SSM_TILE),
        in_specs=[pl.BlockSpec((SSM_TILE, SLAB), lambda gb, i: (i, gb)), compact, compact],
        out_specs=[state, state],
        scratch_shapes=[pltpu.VMEM((SLAB, 2 * STATE_COLS), BF16)],
        compiler_params=pltpu.CompilerParams(dimension_semantics=("arbitrary", "arbitrary"),
                                             vmem_limit_bytes=VMEM_LIMIT),
        name="ssm_state",
    )(u2, v_re, v_im)


def _ssm_scan_kernel(sr_ref, si_ref, ar_ref, ai_ref, hr_ref, hi_ref, cr_scr, ci_scr, s_scr, h_scr):
    @pl.when(pl.program_id(0) == 0)
    def _():
        cr_scr[...] = jnp.zeros_like(cr_scr)
        ci_scr[...] = jnp.zeros_like(ci_scr)

    ar, ai = ar_ref[...], ai_ref[...]
    nb = sr_ref.shape[0]
    tiles = ar.shape[0]
    for b in range(nb):
        for part, ref in enumerate((sr_ref, si_ref)):
            for r in range(tiles):
                s_scr[part, b, pl.ds(r, SCAN_TILE, stride=tiles), :] = ref[b, :, r * LANES:(r + 1) * LANES]

    def body(k, carry):
        rows = pl.ds(pl.multiple_of(k * tiles, tiles), tiles)
        out = []
        for b in range(nb):
            hr, hi = carry[2 * b], carry[2 * b + 1]
            h_scr[0, b, rows, :] = hr
            h_scr[1, b, rows, :] = hi
            out += [ar * hr - ai * hi + s_scr[0, b, rows, :], ar * hi + ai * hr + s_scr[1, b, rows, :]]
        return tuple(out)

    init = tuple(scr[b] for b in range(nb) for scr in (cr_scr, ci_scr))
    fin = lax.fori_loop(0, SCAN_TILE, body, init, unroll=8)
    for b in range(nb):
        cr_scr[b] = fin[2 * b]
        ci_scr[b] = fin[2 * b + 1]
        for part, ref in enumerate((hr_ref, hi_ref)):
            for r in range(tiles):
                ref[b, :, r * LANES:(r + 1) * LANES] = (
                    h_scr[part, b, pl.ds(r, SCAN_TILE, stride=tiles), :].astype(BF16))


def _ssm_scan(s_re, s_im, a_re, a_im):
    bsz, k, cols = s_re.shape
    rows, lanes = a_re.shape
    blk = pl.BlockSpec((bsz, SCAN_TILE, cols), lambda i: (0, i, 0))
    return pl.pallas_call(
        _ssm_scan_kernel,
        out_shape=[jax.ShapeDtypeStruct(s_re.shape, BF16)] * 2,
        grid=(k // SCAN_TILE,),
        in_specs=[blk, blk, _resident((rows, lanes)), _resident((rows, lanes))],
        out_specs=[blk, blk],
        scratch_shapes=[pltpu.VMEM((bsz, rows, lanes), F32)] * 2
        + [pltpu.VMEM((2, bsz, SCAN_TILE * rows, lanes), F32)] * 2,
        compiler_params=pltpu.CompilerParams(dimension_semantics=("arbitrary",)),
        name="ssm_scan",
    )(s_re, s_im, a_re, a_im)


def _ssm_out_kernel(u_ref, hr_ref, hi_ref, bd_ref, wr_ref, wi_ref, y_ref, t_scr, wt_scr):
    @pl.when(pl.program_id(1) == 0)
    def _():
        bd = bd_ref[0].astype(BF16)
        for s in range(SSM_CHUNK):
            rows = slice(s * LANES, (s + 1) * LANES)
            if s:
                t_scr[rows, :s * LANES] = jnp.zeros((LANES, s * LANES), BF16)
            t_scr[rows, s * LANES:] = bd[:, :(SSM_CHUNK - s) * LANES]
        _expand_block_diagonal(wr_ref, wi_ref, wt_scr)

    pair = 2 * LANES
    contract_last = (((1,), (1,)), ((), ()))
    for tb in range(SLAB // pair):
        k_rows = (tb + 1) * pair
        cols = slice(tb * pair, (tb + 1) * pair)
        y = jnp.dot(u_ref[:, :k_rows], t_scr[:k_rows, cols], preferred_element_type=F32)
        y += lax.dot_general(hr_ref[...], wt_scr[cols, :STATE_COLS], contract_last,
                             preferred_element_type=F32)
        y += lax.dot_general(hi_ref[...], wt_scr[cols, STATE_COLS:], contract_last,
                             preferred_element_type=F32)
        y_ref[:, cols] = y


def _ssm_out(u2, h_re, h_im, bd, w_re, w_im):
    n = u2.shape[0]
    state = pl.BlockSpec((SSM_TILE, STATE_COLS), lambda gb, i: (i, gb))
    slab = pl.BlockSpec((SSM_TILE, SLAB), lambda gb, i: (i, gb))
    compact = pl.BlockSpec((1, SLAB, 2 * STATE), lambda gb, i: (gb, 0, 0))
    return pl.pallas_call(
        _ssm_out_kernel,
        out_shape=jax.ShapeDtypeStruct(u2.shape, F32),
        grid=(N_GROUP_BLOCKS, n // SSM_TILE),
        in_specs=[slab, state, state,
                  pl.BlockSpec((1, LANES, SLAB), lambda gb, i: (gb, 0, 0)), compact, compact],
        out_specs=slab,
        scratch_shapes=[pltpu.VMEM((SLAB, SLAB), BF16), pltpu.VMEM((SLAB, 2 * STATE_COLS), BF16)],
        compiler_params=pltpu.CompilerParams(dimension_semantics=("arbitrary", "arbitrary"),
                                             vmem_limit_bytes=VMEM_LIMIT),
        name="ssm_out",
    )(u2, h_re, h_im, bd, w_re, w_im)


def _s5(u2, prep, bsz):
    bd, v_re, v_im, w_re, w_im = prep[:5]
    a16 = prep[5].reshape(N_SSM_GROUPS, SSM_GROUP, 2 * STATE)[:, 0]
    n = u2.shape[0]
    kc = n // bsz
    s_re, s_im = _ssm_state(u2, v_re, v_im)
    tile_shape = (N_SSM_GROUPS * STATE // LANES, LANES)
    a_re = a16[:, :STATE].reshape(tile_shape)
    a_im = a16[:, STATE:].reshape(tile_shape)
    h_re, h_im = _ssm_scan(s_re.reshape(bsz, kc, -1), s_im.reshape(bsz, kc, -1), a_re, a_im)
    return _ssm_out(u2, h_re.reshape(n, -1), h_im.reshape(n, -1), bd, w_re, w_im)


def _swap_lane_halves(x):
    packed = pltpu.bitcast(x, jnp.uint32)
    return pltpu.bitcast(pltpu.roll(packed, HEAD_DIM, axis=1), BF16)


def _attn_out_kernel(sink_ref, q_ref, za_ref, kv_ref, kvp_ref, y2_ref, zs_ref,
                     x_ref, gate_ref, ag_ref, gw_ref, gb_ref, sg_ref, wo_ref, fg_ref,
                     o_ref, kv_scr, attn_scr, y_scr, mix_scr, out_scr,
                     *, n_tiles, tiles_per_batch):
    step = pl.program_id(0)
    cur_slot = step % 2
    prev_slot = 1 - cur_slot

    @pl.when(step == 0)
    def _():
        mix_scr[1] = jnp.zeros(mix_scr.shape[1:], BF16)

    first = jnp.minimum(step, n_tiles - 1) % tiles_per_batch == 0
    keep = jnp.where(first, 0.0, 1.0).astype(BF16)
    kv_scr[0:BLOCK, :] = kvp_ref[...] * keep
    kv_scr[BLOCK:, :] = kv_ref[...]

    older = (lax.broadcasted_iota(jnp.int32, (BLOCK, BLOCK), 1)
             > lax.broadcasted_iota(jnp.int32, (BLOCK, BLOCK), 0))
    kv_half = lax.broadcasted_iota(jnp.int32, (2 * BLOCK, LANES), 1) // HEAD_DIM
    pairs = Q_PER_KV // 2

    def both_halves(r0, col):
        tile, half = col // LANES, (col % LANES) // HEAD_DIM
        band = kv_scr[pl.ds(r0, 2 * BLOCK), tile * LANES:(tile + 1) * LANES]
        own = jnp.where(kv_half == half, band, jnp.zeros_like(band))
        other = _swap_lane_halves(own)
        return (own, other) if half == 0 else (other, own)

    contract_last = (((1,), (1,)), ((), ()))
    for gb in range(N_GROUP_BLOCKS):
        for tok in range(SSM_CHUNK):
            c0 = gb * SLAB + tok * LANES
            y_scr[gb, pl.ds(tok, CHUNK_ROWS, stride=SSM_CHUNK), :] = y2_ref[:, c0:c0 + LANES]

    def block(j, carry):
        r0 = j * BLOCK
        rows = pl.ds(r0, BLOCK)
        out_scr[j] = jnp.dot(mix_scr[prev_slot], wo_ref[:, j * OUT_CHUNK:(j + 1) * OUT_CHUNK],
                             preferred_element_type=F32)
        no_past = jnp.where(first, NEG, 0.0) if j == 0 else None
        for h in range(N_KV_HEADS):
            k_par = both_halves(r0, h * HEAD_DIM)
            v_par = both_halves(r0, KV_WIDTH + h * HEAD_DIM)
            q0 = h * Q_PER_KV * HEAD_DIM
            qs = jnp.concatenate(
                [q_ref[pl.ds(r0, BLOCK), q0 + p * PAIR:q0 + (p + 1) * PAIR] for p in range(pairs)],
                axis=0)
            acc = [None] * pairs
            for par in range(2):
                s_all = lax.dot_general(qs, k_par[par], contract_last, preferred_element_type=F32)
                probs, inv = [], []
                for p in range(pairs):
                    past = s_all[p * BLOCK:(p + 1) * BLOCK, :BLOCK]
                    if no_past is not None:
                        past = past + no_past
                    s = jnp.where(older, past, s_all[p * BLOCK:(p + 1) * BLOCK, BLOCK:])
                    sink = sink_ref[h * Q_PER_KV + 2 * p + par] * LOG2_E
                    m = jnp.max(s, axis=-1, keepdims=True)
                    e = jnp.exp2(s - m)
                    l = jnp.sum(e, axis=-1, keepdims=True) + jnp.exp2(sink - m)
                    inv.append(1.0 / l)
                    e = e.astype(BF16)
                    zero = jnp.zeros_like(e)
                    probs.append(jnp.concatenate(
                        [jnp.where(older, e, zero), jnp.where(older, zero, e)], axis=1))
                pv = jnp.dot(jnp.concatenate(probs, axis=0), v_par[par], preferred_element_type=F32)
                for p in range(pairs):
                    o = pv[p * BLOCK:(p + 1) * BLOCK] * inv[p]
                    acc[p] = o if par == 0 else acc[p] + o
            for p in range(pairs):
                attn_scr[:, q0 + p * PAIR:q0 + (p + 1) * PAIR] = acc[p]
        a = attn_scr[...]
        a = a * lax.rsqrt(jnp.mean(a * a, axis=-1, keepdims=True) + EPS) * ag_ref[...]
        za = za_ref[rows, :]
        mix_scr[cur_slot, rows, :ATTN_WIDTH] = a.astype(BF16) * (za * jax.nn.sigmoid(za))
        y = jnp.concatenate([y_scr[gb, rows, :] for gb in range(N_GROUP_BLOCKS)], axis=1)
        y = 0.5 * y * (1.0 + lax.erf(y * math.sqrt(0.5)))
        glu = jnp.dot(y.astype(BF16), gw_ref[...], preferred_element_type=F32) + gb_ref[...]
        y = y * jax.nn.sigmoid(glu)
        y = y * lax.rsqrt(jnp.mean(y * y, axis=-1, keepdims=True) + EPS) * sg_ref[...]
        zs = zs_ref[rows, :]
        mix_scr[cur_slot, rows, ATTN_WIDTH:] = y.astype(BF16) * (zs * jax.nn.sigmoid(zs))
        return carry

    for j in range(q_ref.shape[0] // BLOCK):
        block(j, 0)

    out = jnp.concatenate([out_scr[c] for c in range(out_scr.shape[0])], axis=1)
    res = x_ref[...] + gate_ref[0] * out
    o_ref[...] = res * lax.rsqrt(jnp.mean(res * res, axis=-1, keepdims=True) + EPS) * fg_ref[...]


def _attn_out(sinks, q, za, kv, y2, zs, x2, gate, attn_gain, glu_w, glu_b, ssm_gain, w_out,
              final_gain, rows_per_batch):
    t, d = x2.shape
    n_tiles = t // ROW_TILE
    tiles_per_batch = rows_per_batch // ROW_TILE
    blocks_per_tile = ROW_TILE // BLOCK
    cur = lambda i: jnp.minimum(i, n_tiles - 1)
    old = lambda i: jnp.maximum(i - 1, 0)
    row = lambda w: pl.BlockSpec((ROW_TILE, w), lambda i: (cur(i), 0))
    prev_block = pl.BlockSpec(
        (BLOCK, 2 * KV_WIDTH), lambda i: (jnp.maximum(cur(i) * blocks_per_tile - 1, 0), 0))
    out_chunks = d // OUT_CHUNK
    return pl.pallas_call(
        functools.partial(_attn_out_kernel, n_tiles=n_tiles, tiles_per_batch=tiles_per_batch),
        out_shape=jax.ShapeDtypeStruct((t, d), F32),
        grid=(n_tiles + 1,),
        in_specs=[pl.BlockSpec(memory_space=pltpu.SMEM),
                  row(ATTN_WIDTH), row(ATTN_WIDTH), row(2 * KV_WIDTH), prev_block,
                  pl.BlockSpec((CHUNK_ROWS, y2.shape[1]), lambda i: (cur(i), 0)),
                  row(SSM_WIDTH),
                  pl.BlockSpec((ROW_TILE, d), lambda i: (old(i), 0)),
                  pl.BlockSpec((1, 1, d), lambda i: (old(i) // tiles_per_batch, 0, 0)),
                  _resident((1, ATTN_WIDTH)),
                  _resident((SSM_WIDTH, SSM_WIDTH)), _resident((1, SSM_WIDTH)),
                  _resident((1, SSM_WIDTH)),
                  _resident((d, d)), _resident((1, d))],
        out_specs=pl.BlockSpec((ROW_TILE, d), lambda i: (old(i), 0)),
        scratch_shapes=[pltpu.VMEM((ROW_TILE + BLOCK, 2 * KV_WIDTH), BF16),
                        pltpu.VMEM((BLOCK, ATTN_WIDTH), F32),
                        pltpu.VMEM((N_GROUP_BLOCKS, ROW_TILE, LANES), F32),
                        pltpu.VMEM((2, ROW_TILE, d), BF16),
                        pltpu.VMEM((out_chunks, ROW_TILE, OUT_CHUNK), F32)],
        compiler_params=pltpu.CompilerParams(dimension_semantics=("arbitrary",),
                                             vmem_limit_bytes=VMEM_LIMIT),
        name="attn_out",
    )(sinks, q, za, kv, kv, y2, zs, x2, gate, attn_gain, glu_w, glu_b, ssm_gain, w_out, final_gain)


def kernel(x, c, w_ada, b_ada, norm_gain, w_in, b_in, attn_sinks, attn_out_gain, ssm_lambda_re,
           ssm_lambda_im, ssm_log_step, ssm_b_re, ssm_b_im, ssm_c_re, ssm_c_im, ssm_d, glu_w, glu_b,
           ssm_out_gain, w_out, final_gain):
    bsz, seq, d = x.shape
    assert w_ada.shape[0] == 1, "single-layer trunk only"
    x2 = x.reshape(bsz * seq, d)
    mod = _ada(c, w_ada[0], b_ada[0])
    shift, scale, gate = (m.reshape(bsz, 1, d) for m in jnp.split(mod, 3, axis=-1))
    q, kv, za, zs, u2 = _inproj(x2, scale, shift, norm_gain[0].reshape(1, d),
                                w_in[0].astype(BF16), b_in[0].reshape(1, -1), seq)
    prep = _ssm_prep(ssm_lambda_re[0], ssm_lambda_im[0], ssm_log_step[0], ssm_b_re[0],
                     ssm_b_im[0], ssm_c_re[0], ssm_c_im[0], ssm_d[0])
    y2 = _s5(u2, prep, bsz)
    out = _attn_out(attn_sinks[0], q, za, kv, y2, zs, x2, gate,
                    attn_out_gain[0].reshape(1, -1), glu_w[0].astype(BF16),
                    glu_b[0].reshape(1, -1), ssm_out_gain[0].reshape(1, -1),
                    w_out[0].astype(BF16), final_gain.reshape(1, d), seq)
    return out.reshape(bsz, seq, d)
```

```python
import functools
import math

import jax
import jax.numpy as jnp
from jax import lax
from jax.experimental import pallas as pl
from jax.experimental.pallas import tpu as pltpu

F32 = jnp.float32
BF16 = jnp.bfloat16

D_MODEL = 2048
HEAD_DIM = 64
N_Q_HEADS = 24
N_KV_HEADS = 3
Q_PER_KV = 8
ATTN_WIDTH = N_Q_HEADS * HEAD_DIM
BLOCK = 128
SSM_GROUP = 16
SSM_WIDTH = D_MODEL - ATTN_WIDTH
N_SSM_GROUPS = SSM_WIDTH // SSM_GROUP
STATE = 64
EPS = 1e-5
NEG = -1e30
LOG2_E = math.log2(math.e)

LANES = 128
SSM_CHUNK = 16
CHUNK_COLS = SSM_CHUNK * SSM_GROUP
GROUPS_PER_BLOCK = LANES // SSM_GROUP
N_GROUP_BLOCKS = N_SSM_GROUPS // GROUPS_PER_BLOCK
SLAB = SSM_CHUNK * LANES
STATE_COLS = GROUPS_PER_BLOCK * STATE
KV_WIDTH = N_KV_HEADS * HEAD_DIM
PAIR = 2 * HEAD_DIM
Q_SCALE = LOG2_E * HEAD_DIM ** -0.5
PROJ_COLS = 512

VMEM_LIMIT = 56 * 1024 * 1024

ROW_TILE = 512
CHUNK_ROWS = ROW_TILE // SSM_CHUNK
ADA_COLS = 1024
OUT_CHUNK = D_MODEL // (ROW_TILE // BLOCK)
SSM_TILE = 1024
SCAN_TILE = 128
SCAN_PITCH = 24


def _resident(shape):
    return pl.BlockSpec(shape, lambda *_: (0,) * len(shape), pipeline_mode=pl.Buffered(1))


def _ada_kernel(ct_ref, w_ref, b_ref, o_ref):
    ct = ct_ref[...]
    s = ct * jax.nn.sigmoid(ct)
    w = w_ref[...]
    rows = [jnp.sum(s[:, b:b + 1] * w, axis=0, keepdims=True) for b in range(ct.shape[1])]
    o_ref[...] = jnp.concatenate(rows, axis=0) + b_ref[...]


def _ada(c, w_ada, b_ada):
    bsz, d = c.shape
    n = w_ada.shape[1]
    return pl.pallas_call(
        _ada_kernel,
        out_shape=jax.ShapeDtypeStruct((bsz, n), F32),
        grid=(n // ADA_COLS,),
        in_specs=[pl.BlockSpec((d, bsz), lambda i: (0, 0)),
                  pl.BlockSpec((d, ADA_COLS), lambda i: (0, i)),
                  pl.BlockSpec((1, ADA_COLS), lambda i: (0, i))],
        out_specs=pl.BlockSpec((bsz, ADA_COLS), lambda i: (0, i)),
        compiler_params=pltpu.CompilerParams(dimension_semantics=("arbitrary",)),
        name="ada",
    )(c.T, w_ada, b_ada.reshape(1, n))


def _inproj_kernel(x_ref, sc_ref, sh_ref, g_ref, w_ref, b_ref,
                   q_ref, kv_ref, za_ref, zs_ref, u2_ref, h_scr, u_scr):
    x = x_ref[...]
    var = jnp.mean(x * x, axis=-1, keepdims=True)
    gain = g_ref[...] * (1.0 + sc_ref[0])
    h_scr[...] = (x * lax.rsqrt(var + EPS) * gain + sh_ref[0]).astype(BF16)

    def project(col, width):
        acc = jnp.dot(h_scr[...], w_ref[:, col:col + width], preferred_element_type=F32)
        return acc + b_ref[:, col:col + width]

    col_kv, col_za = ATTN_WIDTH, ATTN_WIDTH + 2 * KV_WIDTH
    col_u = col_za + ATTN_WIDTH
    col_zs = col_u + SSM_WIDTH
    for c0 in range(0, ATTN_WIDTH, PROJ_COLS):
        q_ref[:, c0:c0 + PROJ_COLS] = (project(c0, PROJ_COLS) * Q_SCALE).astype(BF16)
        za_ref[:, c0:c0 + PROJ_COLS] = project(col_za + c0, PROJ_COLS).astype(BF16)
    kv_ref[...] = project(col_kv, 2 * KV_WIDTH).astype(BF16)
    zs_ref[...] = project(col_zs, SSM_WIDTH).astype(BF16)
    u = project(col_u, SSM_WIDTH)
    for gb in range(N_GROUP_BLOCKS):
        u_scr[gb] = u[:, gb * LANES:(gb + 1) * LANES]
        for tok in range(SSM_CHUNK):
            c0 = gb * SLAB + tok * LANES
            u2_ref[:, c0:c0 + LANES] = (
                u_scr[gb, pl.ds(tok, CHUNK_ROWS, stride=SSM_CHUNK), :].astype(BF16))


def _inproj(x2, scale, shift, gain, w_cat, b_cat, rows_per_batch):
    t, d = x2.shape
    widths = (ATTN_WIDTH, 2 * KV_WIDTH, ATTN_WIDTH, SSM_WIDTH)
    n_all = w_cat.shape[1]
    tiles_per_batch = rows_per_batch // ROW_TILE
    mod_spec = pl.BlockSpec((1, 1, d), lambda i: (i // tiles_per_batch, 0, 0))
    u2_cols = N_GROUP_BLOCKS * SLAB
    return pl.pallas_call(
        _inproj_kernel,
        out_shape=[jax.ShapeDtypeStruct((t, w), BF16) for w in widths]
        + [jax.ShapeDtypeStruct((t // SSM_CHUNK, u2_cols), BF16)],
        grid=(t // ROW_TILE,),
        in_specs=[pl.BlockSpec((ROW_TILE, d), lambda i: (i, 0)),
                  mod_spec, mod_spec,
                  _resident((1, d)),
                  _resident((d, n_all)),
                  _resident((1, n_all))],
        out_specs=[pl.BlockSpec((ROW_TILE, w), lambda i: (i, 0)) for w in widths]
        + [pl.BlockSpec((CHUNK_ROWS, u2_cols), lambda i: (i, 0))],
        scratch_shapes=[pltpu.VMEM((ROW_TILE, d), BF16),
                        pltpu.VMEM((N_GROUP_BLOCKS, ROW_TILE, LANES), F32)],
        compiler_params=pltpu.CompilerParams(dimension_semantics=("arbitrary",),
                                             vmem_limit_bytes=VMEM_LIMIT),
        name="inproj",
    )(x2, scale, shift, gain, w_cat, b_cat)


def _ssm_prep_kernel(lr_ref, li_ref, ls_ref, br_ref, bi_ref, cr_ref, ci_ref, d_ref,
                     bd_ref, vr_ref, vi_ref, wr_ref, wi_ref, a16_ref, y_scr):
    lr, li = lr_ref[...], li_ref[...]
    step = jnp.exp(ls_ref[...])
    decay = jnp.exp(lr * step)
    ar, ai = decay * jnp.cos(li * step), decay * jnp.sin(li * step)
    den = lr * lr + li * li
    nr, ni = ar - 1.0, ai
    coef_re = (nr * lr + ni * li) / den
    coef_im = (ni * lr - nr * li) / den
    br, bi = br_ref[...], bi_ref[...]
    bbar_re = coef_re * br - coef_im * bi
    bbar_im = coef_re * bi + coef_im * br
    cr, ci = cr_ref[...], ci_ref[...]
    lo = lax.broadcasted_iota(jnp.int32, lr.shape, 2) < STATE
    pr, pi = jnp.ones_like(ar), jnp.zeros_like(ai)
    for tau in range(SSM_CHUNK + 1):
        er, ei = pr * cr - pi * ci, pr * ci + pi * cr
        if tau < SSM_CHUNK:
            y_scr[:, tau * LANES:(tau + 1) * LANES, :] = jnp.where(lo, er, ei)
            s = SSM_CHUNK - 1 - tau
            vr_ref[:, s * LANES:(s + 1) * LANES, :] = pr * bbar_re - pi * bbar_im
            vi_ref[:, s * LANES:(s + 1) * LANES, :] = pr * bbar_im + pi * bbar_re
        if tau >= 1:
            wr_ref[:, (tau - 1) * LANES:tau * LANES, :] = er
            wi_ref[:, (tau - 1) * LANES:tau * LANES, :] = -ei
        if tau == SSM_CHUNK:
            a16_ref[...] = jnp.where(lo, pr, pi)
        pr, pi = pr * ar - pi * ai, pr * ai + pi * ar
    x = jnp.where(lo, bbar_re, -bbar_im)
    row = lax.broadcasted_iota(jnp.int32, (LANES, SLAB), 0)
    col = lax.broadcasted_iota(jnp.int32, (LANES, SLAB), 1)
    same_group = row // SSM_GROUP == (col % LANES) // SSM_GROUP
    diagonal = (lax.broadcasted_iota(jnp.int32, (LANES, LANES), 0)
                == lax.broadcasted_iota(jnp.int32, (LANES, LANES), 1))
    for gb in range(N_GROUP_BLOCKS):
        k = lax.dot_general(x[gb], y_scr[gb], (((1,), (1,)), ((), ())),
                            precision=lax.Precision.HIGHEST, preferred_element_type=F32)
        k = jnp.where(same_group, k, 0.0)
        bd_ref[gb, :, LANES:] = k[:, LANES:]
        bd_ref[gb, :, :LANES] = k[:, :LANES] + jnp.where(diagonal, d_ref[gb], 0.0)


def _ssm_prep(lam_re, lam_im, log_step, b_re, b_im, c_re, c_im, d_skip):
    nb = N_GROUP_BLOCKS
    rows = lambda a: jnp.concatenate([a, a], axis=-1).reshape(nb, LANES, 2 * STATE)
    per_group = lambda a: rows(jnp.repeat(a[:, None, :], SSM_GROUP, axis=1))
    compact = jax.ShapeDtypeStruct((nb, SLAB, 2 * STATE), F32)
    return pl.pallas_call(
        _ssm_prep_kernel,
        out_shape=[jax.ShapeDtypeStruct((nb, LANES, SLAB), F32),
                   compact, compact,
                   compact, compact,
                   jax.ShapeDtypeStruct((nb, LANES, 2 * STATE), F32)],
        scratch_shapes=[pltpu.VMEM((nb, SLAB, 2 * STATE), F32)],
        compiler_params=pltpu.CompilerParams(vmem_limit_bytes=VMEM_LIMIT),
        name="ssm_prep",
    )(per_group(lam_re), per_group(lam_im),
      jnp.broadcast_to(jnp.repeat(log_step, SSM_GROUP).reshape(nb, LANES, 1), (nb, LANES, 2 * STATE)),
      rows(jnp.swapaxes(b_re, 1, 2)), rows(jnp.swapaxes(b_im, 1, 2)), rows(c_re), rows(c_im),
      jnp.broadcast_to(d_skip.reshape(nb, 1, LANES), (nb, LANES, LANES)))


def _expand_block_diagonal(re_ref, im_ref, dst):
    shape = (SLAB, 2 * STATE)
    row_group = (lax.broadcasted_iota(jnp.int32, shape, 0) % LANES) // SSM_GROUP
    half = lax.broadcasted_iota(jnp.int32, shape, 1) // STATE
    for j in range(STATE_COLS // LANES):
        own = row_group == 2 * j + half
        for src, c0 in ((re_ref, 0), (im_ref, STATE_COLS)):
            dst[:, c0 + j * LANES:c0 + (j + 1) * LANES] = jnp.where(own, src[0], 0.0).astype(BF16)


def _ssm_state_kernel(u_ref, vr_ref, vi_ref, sr_ref, si_ref, v_scr):
    @pl.when(pl.program_id(1) == 0)
    def _():
        _expand_block_diagonal(vr_ref, vi_ref, v_scr)

    s = jnp.dot(u_ref[...], v_scr[...], preferred_element_type=F32)
    sr_ref[...] = s[:, :STATE_COLS]
    si_ref[...] = s[:, STATE_COLS:]


def _ssm_state(u2, v_re, v_im):
    n = u2.shape[0]
    state = pl.BlockSpec((SSM_TILE, STATE_COLS), lambda gb, i: (i, gb))
    compact = pl.BlockSpec((1, SLAB, 2 * STATE), lambda gb, i: (gb, 0, 0))
    return pl.pallas_call(
        _ssm_state_kernel,
        out_shape=[jax.ShapeDtypeStruct((n, N_GROUP_BLOCKS * STATE_COLS), F32)] * 2,
        grid=(N_GROUP_BLOCKS, n // SSM_TILE),
        in_specs=[pl.BlockSpec((SSM_TILE, SLAB), lambda gb, i: (i, gb)), compact, compact],
        out_specs=[state, state],
        scratch_shapes=[pltpu.VMEM((SLAB, 2 * STATE_COLS), BF16)],
        compiler_params=pltpu.CompilerParams(dimension_semantics=("arbitrary", "arbitrary"),
                                             vmem_limit_bytes=VMEM_LIMIT),
        name="ssm_state",
    )(u2, v_re, v_im)


def _ssm_scan_kernel(sr_ref, si_ref, ar_ref, ai_ref, hr_ref, hi_ref, cr_scr, ci_scr, s_scr, h_scr):
    @pl.when(pl.program_id(0) == 0)
    def _():
        cr_scr[...] = jnp.zeros_like(cr_scr)
        ci_scr[...] = jnp.zeros_like(ci_scr)

    ar, ai = ar_ref[...], ai_ref[...]
    nb = sr_ref.shape[0]
    tiles = ar.shape[0]
    for b in range(nb):
        for part, ref in enumerate((sr_ref, si_ref)):
            for r in range(tiles):
                s_scr[part, b, pl.ds(r, SCAN_TILE, stride=SCAN_PITCH), :] = (
                    ref[b, :, r * LANES:(r + 1) * LANES])

    def body(k, carry):
        rows = pl.ds(pl.multiple_of(k * SCAN_PITCH, 8), tiles)
        out = []
        for b in range(nb):
            hr, hi = carry[2 * b], carry[2 * b + 1]
            h_scr[0, b, rows, :] = hr
            h_scr[1, b, rows, :] = hi
            out += [ar * hr - ai * hi + s_scr[0, b, rows, :], ar * hi + ai * hr + s_scr[1, b, rows, :]]
        return tuple(out)

    init = tuple(scr[b] for b in range(nb) for scr in (cr_scr, ci_scr))
    fin = lax.fori_loop(0, SCAN_TILE, body, init, unroll=8)
    for b in range(nb):
        cr_scr[b] = fin[2 * b]
        ci_scr[b] = fin[2 * b + 1]
        for part, ref in enumerate((hr_ref, hi_ref)):
            for r in range(tiles):
                ref[b, :, r * LANES:(r + 1) * LANES] = (
                    h_scr[part, b, pl.ds(r, SCAN_TILE, stride=SCAN_PITCH), :].astype(BF16))


def _ssm_scan(s_re, s_im, a_re, a_im):
    bsz, k, cols = s_re.shape
    rows, lanes = a_re.shape
    blk = pl.BlockSpec((bsz, SCAN_TILE, cols), lambda i: (0, i, 0))
    return pl.pallas_call(
        _ssm_scan_kernel,
        out_shape=[jax.ShapeDtypeStruct(s_re.shape, BF16)] * 2,
        grid=(k // SCAN_TILE,),
        in_specs=[blk, blk, _resident((rows, lanes)), _resident((rows, lanes))],
        out_specs=[blk, blk],
        scratch_shapes=[pltpu.VMEM((bsz, rows, lanes), F32)] * 2
        + [pltpu.VMEM((2, bsz, SCAN_TILE * SCAN_PITCH, lanes), F32)] * 2,
        compiler_params=pltpu.CompilerParams(dimension_semantics=("arbitrary",)),
        name="ssm_scan",
    )(s_re, s_im, a_re, a_im)


def _ssm_out_kernel(u_ref, hr_ref, hi_ref, bd_ref, wr_ref, wi_ref, y_ref, t_scr, wt_scr):
    @pl.when(pl.program_id(1) == 0)
    def _():
        bd = bd_ref[0].astype(BF16)
        for s in range(SSM_CHUNK):
            rows = slice(s * LANES, (s + 1) * LANES)
            if s:
                t_scr[rows, :s * LANES] = jnp.zeros((LANES, s * LANES), BF16)
            t_scr[rows, s * LANES:] = bd[:, :(SSM_CHUNK - s) * LANES]
        _expand_block_diagonal(wr_ref, wi_ref, wt_scr)

    pair = 2 * LANES
    contract_last = (((1,), (1,)), ((), ()))
    for tb in range(SLAB // pair):
        k_rows = (tb + 1) * pair
        cols = slice(tb * pair, (tb + 1) * pair)
        y = jnp.dot(u_ref[:, :k_rows], t_scr[:k_rows, cols], preferred_element_type=F32)
        y += lax.dot_general(hr_ref[...], wt_scr[cols, :STATE_COLS], contract_last,
                             preferred_element_type=F32)
        y += lax.dot_general(hi_ref[...], wt_scr[cols, STATE_COLS:], contract_last,
                             preferred_element_type=F32)
        y_ref[:, cols] = y


def _ssm_out(u2, h_re, h_im, bd, w_re, w_im):
    n = u2.shape[0]
    state = pl.BlockSpec((SSM_TILE, STATE_COLS), lambda gb, i: (i, gb))
    slab = pl.BlockSpec((SSM_TILE, SLAB), lambda gb, i: (i, gb))
    compact = pl.BlockSpec((1, SLAB, 2 * STATE), lambda gb, i: (gb, 0, 0))
    return pl.pallas_call(
        _ssm_out_kernel,
        out_shape=jax.ShapeDtypeStruct(u2.shape, F32),
        grid=(N_GROUP_BLOCKS, n // SSM_TILE),
        in_specs=[slab, state, state,
                  pl.BlockSpec((1, LANES, SLAB), lambda gb, i: (gb, 0, 0)), compact, compact],
        out_specs=slab,
        scratch_shapes=[pltpu.VMEM((SLAB, SLAB), BF16), pltpu.VMEM((SLAB, 2 * STATE_COLS), BF16)],
        compiler_params=pltpu.CompilerParams(dimension_semantics=("arbitrary", "arbitrary"),
                                             vmem_limit_bytes=VMEM_LIMIT),
        name="ssm_out",
    )(u2, h_re, h_im, bd, w_re, w_im)


def _s5(u2, prep, bsz):
    bd, v_re, v_im, w_re, w_im = prep[:5]
    a16 = prep[5].reshape(N_SSM_GROUPS, SSM_GROUP, 2 * STATE)[:, 0]
    n = u2.shape[0]
    kc = n // bsz
    s_re, s_im = _ssm_state(u2, v_re, v_im)
    tile_shape = (N_SSM_GROUPS * STATE // LANES, LANES)
    a_re = a16[:, :STATE].reshape(tile_shape)
    a_im = a16[:, STATE:].reshape(tile_shape)
    h_re, h_im = _ssm_scan(s_re.reshape(bsz, kc, -1), s_im.reshape(bsz, kc, -1), a_re, a_im)
    return _ssm_out(u2, h_re.reshape(n, -1), h_im.reshape(n, -1), bd, w_re, w_im)


def _swap_lane_halves(x):
    packed = pltpu.bitcast(x, jnp.uint32)
    return pltpu.bitcast(pltpu.roll(packed, HEAD_DIM, axis=1), BF16)


def _attn_out_kernel(sink_ref, q_ref, za_ref, kv_ref, kvp_ref, y2_ref, zs_ref,
                     x_ref, gate_ref, ag_ref, gw_ref, gb_ref, sg_ref, wo_ref, fg_ref,
                     o_ref, kv_scr, attn_scr, y_scr, mix_scr, out_scr,
                     *, n_tiles, tiles_per_batch):
    step = pl.program_id(0)
    cur_slot = step % 2
    prev_slot = 1 - cur_slot

    @pl.when(step == 0)
    def _():
        mix_scr[1] = jnp.zeros(mix_scr.shape[1:], BF16)

    first = jnp.minimum(step, n_tiles - 1) % tiles_per_batch == 0
    keep = jnp.where(first, 0.0, 1.0).astype(BF16)
    kv_scr[0:BLOCK, :] = kvp_ref[...] * keep
    kv_scr[BLOCK:, :] = kv_ref[...]

    older = (lax.broadcasted_iota(jnp.int32, (BLOCK, BLOCK), 1)
             > lax.broadcasted_iota(jnp.int32, (BLOCK, BLOCK), 0))
    kv_half = lax.broadcasted_iota(jnp.int32, (2 * BLOCK, LANES), 1) // HEAD_DIM
    pairs = Q_PER_KV // 2

    def both_halves(r0, col):
        tile, half = col // LANES, (col % LANES) // HEAD_DIM
        band = kv_scr[pl.ds(r0, 2 * BLOCK), tile * LANES:(tile + 1) * LANES]
        own = jnp.where(kv_half == half, band, jnp.zeros_like(band))
        other = _swap_lane_halves(own)
        return (own, other) if half == 0 else (other, own)

    contract_last = (((1,), (1,)), ((), ()))
    for gb in range(N_GROUP_BLOCKS):
        for tok in range(SSM_CHUNK):
            c0 = gb * SLAB + tok * LANES
            y_scr[gb, pl.ds(tok, CHUNK_ROWS, stride=SSM_CHUNK), :] = y2_ref[:, c0:c0 + LANES]

    def block(j, carry):
        r0 = j * BLOCK
        rows = pl.ds(r0, BLOCK)
        out_scr[j] = jnp.dot(mix_scr[prev_slot], wo_ref[:, j * OUT_CHUNK:(j + 1) * OUT_CHUNK],
                             preferred_element_type=F32)
        no_past = jnp.where(first, NEG, 0.0) if j == 0 else None
        for h in range(N_KV_HEADS):
            k_par = both_halves(r0, h * HEAD_DIM)
            v_par = both_halves(r0, KV_WIDTH + h * HEAD_DIM)
            q0 = h * Q_PER_KV * HEAD_DIM
            qs = jnp.concatenate(
                [q_ref[pl.ds(r0, BLOCK), q0 + p * PAIR:q0 + (p + 1) * PAIR] for p in range(pairs)],
                axis=0)
            acc = [None] * pairs
            for par in range(2):
                s_all = lax.dot_general(qs, k_par[par], contract_last, preferred_element_type=F32)
                probs, inv = [], []
                for p in range(pairs):
                    past = s_all[p * BLOCK:(p + 1) * BLOCK, :BLOCK]
                    if no_past is not None:
                        past = past + no_past
                    s = jnp.where(older, past, s_all[p * BLOCK:(p + 1) * BLOCK, BLOCK:])
                    sink = sink_ref[h * Q_PER_KV + 2 * p + par] * LOG2_E
                    m = jnp.max(s, axis=-1, keepdims=True)
                    e = jnp.exp2(s - m)
                    l = jnp.sum(e, axis=-1, keepdims=True) + jnp.exp2(sink - m)
                    inv.append(1.0 / l)
                    e = e.astype(BF16)
                    zero = jnp.zeros_like(e)
                    probs.append(jnp.concatenate(
                        [jnp.where(older, e, zero), jnp.where(older, zero, e)], axis=1))
                pv = jnp.dot(jnp.concatenate(probs, axis=0), v_par[par], preferred_element_type=F32)
                for p in range(pairs):
                    o = pv[p * BLOCK:(p + 1) * BLOCK] * inv[p]
                    acc[p] = o if par == 0 else acc[p] + o
            for p in range(pairs):
                attn_scr[:, q0 + p * PAIR:q0 + (p + 1) * PAIR] = acc[p]
        a = attn_scr[...]
        a = a * lax.rsqrt(jnp.mean(a * a, axis=-1, keepdims=True) + EPS) * ag_ref[...]
        za = za_ref[rows, :]
        mix_scr[cur_slot, rows, :ATTN_WIDTH] = a.astype(BF16) * (za * jax.nn.sigmoid(za))
        y = jnp.concatenate([y_scr[gb, rows, :] for gb in range(N_GROUP_BLOCKS)], axis=1)
        y = 0.5 * y * (1.0 + lax.erf(y * math.sqrt(0.5)))
        glu = jnp.dot(y.astype(BF16), gw_ref[...], preferred_element_type=F32) + gb_ref[...]
        y = y * jax.nn.sigmoid(glu)
        y = y * lax.rsqrt(jnp.mean(y * y, axis=-1, keepdims=True) + EPS) * sg_ref[...]
        zs = zs_ref[rows, :]
        mix_scr[cur_slot, rows, ATTN_WIDTH:] = y.astype(BF16) * (zs * jax.nn.sigmoid(zs))
        return carry

    for j in range(q_ref.shape[0] // BLOCK):
        block(j, 0)

    out = jnp.concatenate([out_scr[c] for c in range(out_scr.shape[0])], axis=1)
    res = x_ref[...] + gate_ref[0] * out
    o_ref[...] = res * lax.rsqrt(jnp.mean(res * res, axis=-1, keepdims=True) + EPS) * fg_ref[...]


def _attn_out(sinks, q, za, kv, y2, zs, x2, gate, attn_gain, glu_w, glu_b, ssm_gain, w_out,
              final_gain, rows_per_batch):
    t, d = x2.shape
    n_tiles = t // ROW_TILE
    tiles_per_batch = rows_per_batch // ROW_TILE
    blocks_per_tile = ROW_TILE // BLOCK
    cur = lambda i: jnp.minimum(i, n_tiles - 1)
    old = lambda i: jnp.maximum(i - 1, 0)
    row = lambda w: pl.BlockSpec((ROW_TILE, w), lambda i: (cur(i), 0))
    prev_block = pl.BlockSpec(
        (BLOCK, 2 * KV_WIDTH), lambda i: (jnp.maximum(cur(i) * blocks_per_tile - 1, 0), 0))
    out_chunks = d // OUT_CHUNK
    return pl.pallas_call(
        functools.partial(_attn_out_kernel, n_tiles=n_tiles, tiles_per_batch=tiles_per_batch),
        out_shape=jax.ShapeDtypeStruct((t, d), F32),
        grid=(n_tiles + 1,),
        in_specs=[pl.BlockSpec(memory_space=pltpu.SMEM),
                  row(ATTN_WIDTH), row(ATTN_WIDTH), row(2 * KV_WIDTH), prev_block,
                  pl.BlockSpec((CHUNK_ROWS, y2.shape[1]), lambda i: (cur(i), 0)),
                  row(SSM_WIDTH),
                  pl.BlockSpec((ROW_TILE, d), lambda i: (old(i), 0)),
                  pl.BlockSpec((1, 1, d), lambda i: (old(i) // tiles_per_batch, 0, 0)),
                  _resident((1, ATTN_WIDTH)),
                  _resident((SSM_WIDTH, SSM_WIDTH)), _resident((1, SSM_WIDTH)),
                  _resident((1, SSM_WIDTH)),
                  _resident((d, d)), _resident((1, d))],
        out_specs=pl.BlockSpec((ROW_TILE, d), lambda i: (old(i), 0)),
        scratch_shapes=[pltpu.VMEM((ROW_TILE + BLOCK, 2 * KV_WIDTH), BF16),
                        pltpu.VMEM((BLOCK, ATTN_WIDTH), F32),
                        pltpu.VMEM((N_GROUP_BLOCKS, ROW_TILE, LANES), F32),
                        pltpu.VMEM((2, ROW_TILE, d), BF16),
                        pltpu.VMEM((out_chunks, ROW_TILE, OUT_CHUNK), F32)],
        compiler_params=pltpu.CompilerParams(dimension_semantics=("arbitrary",),
                                             vmem_limit_bytes=VMEM_LIMIT),
        name="attn_out",
    )(sinks, q, za, kv, kv, y2, zs, x2, gate, attn_gain, glu_w, glu_b, ssm_gain, w_out, final_gain)


def kernel(x, c, w_ada, b_ada, norm_gain, w_in, b_in, attn_sinks, attn_out_gain, ssm_lambda_re,
           ssm_lambda_im, ssm_log_step, ssm_b_re, ssm_b_im, ssm_c_re, ssm_c_im, ssm_d, glu_w, glu_b,
           ssm_out_gain, w_out, final_gain):
    bsz, seq, d = x.shape
    assert w_ada.shape[0] == 1, "single-layer trunk only"
    x2 = x.reshape(bsz * seq, d)
    mod = _ada(c, w_ada[0], b_ada[0])
    shift, scale, gate = (m.reshape(bsz, 1, d) for m in jnp.split(mod, 3, axis=-1))
    q, kv, za, zs, u2 = _inproj(x2, scale, shift, norm_gain[0].reshape(1, d),
                                w_in[0].astype(BF16), b_in[0].reshape(1, -1), seq)
    prep = _ssm_prep(ssm_lambda_re[0], ssm_lambda_im[0], ssm_log_step[0], ssm_b_re[0],
                     ssm_b_im[0], ssm_c_re[0], ssm_c_im[0], ssm_d[0])
    y2 = _s5(u2, prep, bsz)
    out = _attn_out(attn_sinks[0], q, za, kv, y2, zs, x2, gate,
                    attn_out_gain[0].reshape(1, -1), glu_w[0].astype(BF16),
                    glu_b[0].reshape(1, -1), ssm_out_gain[0].reshape(1, -1),
                    w_out[0].astype(BF16), final_gain.reshape(1, d), seq)
    return out.reshape(bsz, seq, d)
```

```python
import functools
import math

import jax
import jax.numpy as jnp
from jax import lax
from jax.experimental import pallas as pl
from jax.experimental.pallas import tpu as pltpu

F32 = jnp.float32
BF16 = jnp.bfloat16

D_MODEL = 2048
HEAD_DIM = 64
N_Q_HEADS = 24
N_KV_HEADS = 3
Q_PER_KV = 8
ATTN_WIDTH = N_Q_HEADS * HEAD_DIM
BLOCK = 128
SSM_GROUP = 16
SSM_WIDTH = D_MODEL - ATTN_WIDTH
N_SSM_GROUPS = SSM_WIDTH // SSM_GROUP
STATE = 64
EPS = 1e-5
NEG = -1e30
LOG2_E = math.log2(math.e)

LANES = 128
SSM_CHUNK = 16
CHUNK_COLS = SSM_CHUNK * SSM_GROUP
GROUPS_PER_BLOCK = LANES // SSM_GROUP
N_GROUP_BLOCKS = N_SSM_GROUPS // GROUPS_PER_BLOCK
SLAB = SSM_CHUNK * LANES
STATE_COLS = GROUPS_PER_BLOCK * STATE
KV_WIDTH = N_KV_HEADS * HEAD_DIM
PAIR = 2 * HEAD_DIM
Q_SCALE = LOG2_E * HEAD_DIM ** -0.5
PROJ_COLS = 512

VMEM_LIMIT = 56 * 1024 * 1024

ROW_TILE = 512
CHUNK_ROWS = ROW_TILE // SSM_CHUNK
ADA_COLS = 1024
OUT_CHUNK = D_MODEL // (ROW_TILE // BLOCK)
SSM_TILE = 1024
CAST_STEPS = 16
SCAN_TILE = 128
SCAN_PITCH = 24


def _resident(shape):
    return pl.BlockSpec(shape, lambda *_: (0,) * len(shape), pipeline_mode=pl.Buffered(1))


def _ada_kernel(ct_ref, w_ref, b_ref, o_ref):
    ct = ct_ref[...]
    s = ct * jax.nn.sigmoid(ct)
    w = w_ref[...]
    rows = [jnp.sum(s[:, b:b + 1] * w, axis=0, keepdims=True) for b in range(ct.shape[1])]
    o_ref[...] = jnp.concatenate(rows, axis=0) + b_ref[...]


def _ada(c, w_ada, b_ada):
    bsz, d = c.shape
    n = w_ada.shape[1]
    return pl.pallas_call(
        _ada_kernel,
        out_shape=jax.ShapeDtypeStruct((bsz, n), F32),
        grid=(n // ADA_COLS,),
        in_specs=[pl.BlockSpec((d, bsz), lambda i: (0, 0)),
                  pl.BlockSpec((d, ADA_COLS), lambda i: (0, i)),
                  pl.BlockSpec((1, ADA_COLS), lambda i: (0, i))],
        out_specs=pl.BlockSpec((bsz, ADA_COLS), lambda i: (0, i)),
        compiler_params=pltpu.CompilerParams(dimension_semantics=("arbitrary",)),
        name="ada",
    )(c.T, w_ada, b_ada.reshape(1, n))


def _inproj_kernel(x_ref, sc_ref, sh_ref, g_ref, w_ref, b_ref, wo_ref,
                   q_ref, kv_ref, za_ref, zs_ref, u2_ref, wo16_ref, h_scr, u_scr):
    wo16_ref[...] = wo_ref[...].astype(BF16)
    x = x_ref[...]
    var = jnp.mean(x * x, axis=-1, keepdims=True)
    gain = g_ref[...] * (1.0 + sc_ref[0])
    h_scr[...] = (x * lax.rsqrt(var + EPS) * gain + sh_ref[0]).astype(BF16)

    def project(col, width):
        acc = jnp.dot(h_scr[...], w_ref[:, col:col + width], preferred_element_type=F32)
        return acc + b_ref[:, col:col + width]

    col_kv, col_za = ATTN_WIDTH, ATTN_WIDTH + 2 * KV_WIDTH
    col_u = col_za + ATTN_WIDTH
    col_zs = col_u + SSM_WIDTH
    for c0 in range(0, ATTN_WIDTH, PROJ_COLS):
        q_ref[:, c0:c0 + PROJ_COLS] = (project(c0, PROJ_COLS) * Q_SCALE).astype(BF16)
        za_ref[:, c0:c0 + PROJ_COLS] = project(col_za + c0, PROJ_COLS).astype(BF16)
    kv_ref[...] = project(col_kv, 2 * KV_WIDTH).astype(BF16)
    zs_ref[...] = project(col_zs, SSM_WIDTH).astype(BF16)
    u = project(col_u, SSM_WIDTH)
    for gb in range(N_GROUP_BLOCKS):
        u_scr[gb] = u[:, gb * LANES:(gb + 1) * LANES]
        for tok in range(SSM_CHUNK):
            c0 = gb * SLAB + tok * LANES
            u2_ref[:, c0:c0 + LANES] = (
                u_scr[gb, pl.ds(tok, CHUNK_ROWS, stride=SSM_CHUNK), :].astype(BF16))


def _inproj(x2, scale, shift, gain, w_cat, b_cat, w_out, rows_per_batch):
    t, d = x2.shape
    widths = (ATTN_WIDTH, 2 * KV_WIDTH, ATTN_WIDTH, SSM_WIDTH)
    n_all = w_cat.shape[1]
    n_tiles = t // ROW_TILE
    tiles_per_batch = rows_per_batch // ROW_TILE
    mod_spec = pl.BlockSpec((1, 1, d), lambda i: (i // tiles_per_batch, 0, 0))
    u2_cols = N_GROUP_BLOCKS * SLAB
    wo_spec = pl.BlockSpec((w_out.shape[0] // n_tiles, w_out.shape[1]), lambda i: (i, 0))
    return pl.pallas_call(
        _inproj_kernel,
        out_shape=[jax.ShapeDtypeStruct((t, w), BF16) for w in widths]
        + [jax.ShapeDtypeStruct((t // SSM_CHUNK, u2_cols), BF16),
           jax.ShapeDtypeStruct(w_out.shape, BF16)],
        grid=(n_tiles,),
        in_specs=[pl.BlockSpec((ROW_TILE, d), lambda i: (i, 0)),
                  mod_spec, mod_spec,
                  _resident((1, d)),
                  _resident((d, n_all)),
                  _resident((1, n_all)),
                  wo_spec],
        out_specs=[pl.BlockSpec((ROW_TILE, w), lambda i: (i, 0)) for w in widths]
        + [pl.BlockSpec((CHUNK_ROWS, u2_cols), lambda i: (i, 0)), wo_spec],
        scratch_shapes=[pltpu.VMEM((ROW_TILE, d), BF16),
                        pltpu.VMEM((N_GROUP_BLOCKS, ROW_TILE, LANES), F32)],
        compiler_params=pltpu.CompilerParams(dimension_semantics=("arbitrary",),
                                             vmem_limit_bytes=VMEM_LIMIT),
        name="inproj",
    )(x2, scale, shift, gain, w_cat, b_cat, w_out)


def _ssm_prep_kernel(win_ref, *refs):
    win16_ref, factor_refs = refs[8], refs[:8] + refs[9:]
    win16_ref[...] = win_ref[...].astype(BF16)

    @pl.when(pl.program_id(0) == 0)
    def _():
        _ssm_factors(*factor_refs)


def _ssm_factors(lr_ref, li_ref, ls_ref, br_ref, bi_ref, cr_ref, ci_ref, d_ref,
                 bd_ref, vr_ref, vi_ref, wr_ref, wi_ref, a16_ref, y_scr):
    lr, li = lr_ref[...], li_ref[...]
    step = jnp.exp(ls_ref[...])
    decay = jnp.exp(lr * step)
    ar, ai = decay * jnp.cos(li * step), decay * jnp.sin(li * step)
    den = lr * lr + li * li
    nr, ni = ar - 1.0, ai
    coef_re = (nr * lr + ni * li) / den
    coef_im = (ni * lr - nr * li) / den
    br, bi = br_ref[...], bi_ref[...]
    bbar_re = coef_re * br - coef_im * bi
    bbar_im = coef_re * bi + coef_im * br
    cr, ci = cr_ref[...], ci_ref[...]
    lo = lax.broadcasted_iota(jnp.int32, lr.shape, 2) < STATE
    pr, pi = jnp.ones_like(ar), jnp.zeros_like(ai)
    for tau in range(SSM_CHUNK + 1):
        er, ei = pr * cr - pi * ci, pr * ci + pi * cr
        if tau < SSM_CHUNK:
            y_scr[:, tau * LANES:(tau + 1) * LANES, :] = jnp.where(lo, er, ei)
            s = SSM_CHUNK - 1 - tau
            vr_ref[:, s * LANES:(s + 1) * LANES, :] = pr * bbar_re - pi * bbar_im
            vi_ref[:, s * LANES:(s + 1) * LANES, :] = pr * bbar_im + pi * bbar_re
        if tau >= 1:
            wr_ref[:, (tau - 1) * LANES:tau * LANES, :] = er
            wi_ref[:, (tau - 1) * LANES:tau * LANES, :] = -ei
        if tau == SSM_CHUNK:
            a16_ref[...] = jnp.where(lo, pr, pi)
        pr, pi = pr * ar - pi * ai, pr * ai + pi * ar
    x = jnp.where(lo, bbar_re, -bbar_im)
    row = lax.broadcasted_iota(jnp.int32, (LANES, SLAB), 0)
    col = lax.broadcasted_iota(jnp.int32, (LANES, SLAB), 1)
    same_group = row // SSM_GROUP == (col % LANES) // SSM_GROUP
    diagonal = (lax.broadcasted_iota(jnp.int32, (LANES, LANES), 0)
                == lax.broadcasted_iota(jnp.int32, (LANES, LANES), 1))
    for gb in range(N_GROUP_BLOCKS):
        k = lax.dot_general(x[gb], y_scr[gb], (((1,), (1,)), ((), ())),
                            precision=lax.Precision.HIGHEST, preferred_element_type=F32)
        k = jnp.where(same_group, k, 0.0)
        bd_ref[gb, :, LANES:] = k[:, LANES:]
        bd_ref[gb, :, :LANES] = k[:, :LANES] + jnp.where(diagonal, d_ref[gb], 0.0)


def _ssm_prep(w_in, lam_re, lam_im, log_step, b_re, b_im, c_re, c_im, d_skip):
    nb = N_GROUP_BLOCKS
    rows = lambda a: jnp.concatenate([a, a], axis=-1).reshape(nb, LANES, 2 * STATE)
    per_group = lambda a: rows(jnp.repeat(a[:, None, :], SSM_GROUP, axis=1))
    factor_inputs = (
        per_group(lam_re), per_group(lam_im),
        jnp.broadcast_to(jnp.repeat(log_step, SSM_GROUP).reshape(nb, LANES, 1), (nb, LANES, 2 * STATE)),
        rows(jnp.swapaxes(b_re, 1, 2)), rows(jnp.swapaxes(b_im, 1, 2)), rows(c_re), rows(c_im),
        jnp.broadcast_to(d_skip.reshape(nb, 1, LANES), (nb, LANES, LANES)))
    compact = (nb, SLAB, 2 * STATE)
    factor_shapes = [(nb, LANES, SLAB),
                     compact, compact,
                     compact, compact,
                     (nb, LANES, 2 * STATE)]
    whole = lambda shape: pl.BlockSpec(shape, lambda i: (0,) * len(shape))
    slab = pl.BlockSpec((w_in.shape[0] // CAST_STEPS, w_in.shape[1]), lambda i: (i, 0))
    outs = pl.pallas_call(
        _ssm_prep_kernel,
        out_shape=[jax.ShapeDtypeStruct(w_in.shape, BF16)]
        + [jax.ShapeDtypeStruct(s, F32) for s in factor_shapes],
        grid=(CAST_STEPS,),
        in_specs=[slab] + [whole(a.shape) for a in factor_inputs],
        out_specs=[slab] + [whole(s) for s in factor_shapes],
        scratch_shapes=[pltpu.VMEM(compact, F32)],
        compiler_params=pltpu.CompilerParams(dimension_semantics=("arbitrary",),
                                             vmem_limit_bytes=VMEM_LIMIT),
        name="ssm_prep",
    )(w_in, *factor_inputs)
    return outs[0], outs[1:]


def _expand_block_diagonal(re_ref, im_ref, dst):
    shape = (SLAB, 2 * STATE)
    row_group = (lax.broadcasted_iota(jnp.int32, shape, 0) % LANES) // SSM_GROUP
    half = lax.broadcasted_iota(jnp.int32, shape, 1) // STATE
    for j in range(STATE_COLS // LANES):
        own = row_group == 2 * j + half
        for src, c0 in ((re_ref, 0), (im_ref, STATE_COLS)):
            dst[:, c0 + j * LANES:c0 + (j + 1) * LANES] = jnp.where(own, src[0], 0.0).astype(BF16)


def _ssm_state_kernel(u_ref, vr_ref, vi_ref, sr_ref, si_ref, v_scr):
    @pl.when(pl.program_id(1) == 0)
    def _():
        _expand_block_diagonal(vr_ref, vi_ref, v_scr)

    s = jnp.dot(u_ref[...], v_scr[...], preferred_element_type=F32)
    sr_ref[...] = s[:, :STATE_COLS]
    si_ref[...] = s[:, STATE_COLS:]


def _ssm_state(u2, v_re, v_im):
    n = u2.shape[0]
    state = pl.BlockSpec((SSM_TILE, STATE_COLS), lambda gb, i: (i, gb))
    compact = pl.BlockSpec((1, SLAB, 2 * STATE), lambda gb, i: (gb, 0, 0))
    return pl.pallas_call(
        _ssm_state_kernel,
        out_shape=[jax.ShapeDtypeStruct((n, N_GROUP_BLOCKS * STATE_COLS), F32)] * 2,
        grid=(N_GROUP_BLOCKS, n // SSM_TILE),
        in_specs=[pl.BlockSpec((SSM_TILE, SLAB), lambda gb, i: (i, gb)), compact, compact],
        out_specs=[state, state],
        scratch_shapes=[pltpu.VMEM((SLAB, 2 * STATE_COLS), BF16)],
        compiler_params=pltpu.CompilerParams(dimension_semantics=("arbitrary", "arbitrary"),
                                             vmem_limit_bytes=VMEM_LIMIT),
        name="ssm_state",
    )(u2, v_re, v_im)


def _ssm_scan_kernel(sr_ref, si_ref, ar_ref, ai_ref, hr_ref, hi_ref, cr_scr, ci_scr, s_scr, h_scr):
    @pl.when(pl.program_id(0) == 0)
    def _():
        cr_scr[...] = jnp.zeros_like(cr_scr)
        ci_scr[...] = jnp.zeros_like(ci_scr)

    ar, ai = ar_ref[...], ai_ref[...]
    nb = sr_ref.shape[0]
    tiles = ar.shape[0]
    for b in range(nb):
        for part, ref in enumerate((sr_ref, si_ref)):
            for r in range(tiles):
                s_scr[part, b, pl.ds(r, SCAN_TILE, stride=SCAN_PITCH), :] = (
                    ref[b, :, r * LANES:(r + 1) * LANES])

    def body(k, carry):
        rows = pl.ds(pl.multiple_of(k * SCAN_PITCH, 8), tiles)
        out = []
        for b in range(nb):
            hr, hi = carry[2 * b], carry[2 * b + 1]
            h_scr[0, b, rows, :] = hr
            h_scr[1, b, rows, :] = hi
            out += [ar * hr - ai * hi + s_scr[0, b, rows, :], ar * hi + ai * hr + s_scr[1, b, rows, :]]
        return tuple(out)

    init = tuple(scr[b] for b in range(nb) for scr in (cr_scr, ci_scr))
    fin = lax.fori_loop(0, SCAN_TILE, body, init, unroll=8)
    for b in range(nb):
        cr_scr[b] = fin[2 * b]
        ci_scr[b] = fin[2 * b + 1]
        for part, ref in enumerate((hr_ref, hi_ref)):
            for r in range(tiles):
                ref[b, :, r * LANES:(r + 1) * LANES] = (
                    h_scr[part, b, pl.ds(r, SCAN_TILE, stride=SCAN_PITCH), :].astype(BF16))


def _ssm_scan(s_re, s_im, a_re, a_im):
    bsz, k, cols = s_re.shape
    rows, lanes = a_re.shape
    blk = pl.BlockSpec((bsz, SCAN_TILE, cols), lambda i: (0, i, 0))
    return pl.pallas_call(
        _ssm_scan_kernel,
        out_shape=[jax.ShapeDtypeStruct(s_re.shape, BF16)] * 2,
        grid=(k // SCAN_TILE,),
        in_specs=[blk, blk, _resident((rows, lanes)), _resident((rows, lanes))],
        out_specs=[blk, blk],
        scratch_shapes=[pltpu.VMEM((bsz, rows, lanes), F32)] * 2
        + [pltpu.VMEM((2, bsz, SCAN_TILE * SCAN_PITCH, lanes), F32)] * 2,
        compiler_params=pltpu.CompilerParams(dimension_semantics=("arbitrary",)),
        name="ssm_scan",
    )(s_re, s_im, a_re, a_im)


def _ssm_out_kernel(u_ref, hr_ref, hi_ref, bd_ref, wr_ref, wi_ref, y_ref, t_scr, wt_scr):
    @pl.when(pl.program_id(1) == 0)
    def _():
        bd = bd_ref[0].astype(BF16)
        for s in range(SSM_CHUNK):
            rows = slice(s * LANES, (s + 1) * LANES)
            if s:
                t_scr[rows, :s * LANES] = jnp.zeros((LANES, s * LANES), BF16)
            t_scr[rows, s * LANES:] = bd[:, :(SSM_CHUNK - s) * LANES]
        _expand_block_diagonal(wr_ref, wi_ref, wt_scr)

    pair = 2 * LANES
    contract_last = (((1,), (1,)), ((), ()))
    for tb in range(SLAB // pair):
        k_rows = (tb + 1) * pair
        cols = slice(tb * pair, (tb + 1) * pair)
        y = jnp.dot(u_ref[:, :k_rows], t_scr[:k_rows, cols], preferred_element_type=F32)
        y += lax.dot_general(hr_ref[...], wt_scr[cols, :STATE_COLS], contract_last,
                             preferred_element_type=F32)
        y += lax.dot_general(hi_ref[...], wt_scr[cols, STATE_COLS:], contract_last,
                             preferred_element_type=F32)
        y_ref[:, cols] = y


def _ssm_out(u2, h_re, h_im, bd, w_re, w_im):
    n = u2.shape[0]
    state = pl.BlockSpec((SSM_TILE, STATE_COLS), lambda gb, i: (i, gb))
    slab = pl.BlockSpec((SSM_TILE, SLAB), lambda gb, i: (i, gb))
    compact = pl.BlockSpec((1, SLAB, 2 * STATE), lambda gb, i: (gb, 0, 0))
    return pl.pallas_call(
        _ssm_out_kernel,
        out_shape=jax.ShapeDtypeStruct(u2.shape, F32),
        grid=(N_GROUP_BLOCKS, n // SSM_TILE),
        in_specs=[slab, state, state,
                  pl.BlockSpec((1, LANES, SLAB), lambda gb, i: (gb, 0, 0)), compact, compact],
        out_specs=slab,
        scratch_shapes=[pltpu.VMEM((SLAB, SLAB), BF16), pltpu.VMEM((SLAB, 2 * STATE_COLS), BF16)],
        compiler_params=pltpu.CompilerParams(dimension_semantics=("arbitrary", "arbitrary"),
                                             vmem_limit_bytes=VMEM_LIMIT),
        name="ssm_out",
    )(u2, h_re, h_im, bd, w_re, w_im)


def _s5(u2, prep, bsz):
    bd, v_re, v_im, w_re, w_im = prep[:5]
    a16 = prep[5].reshape(N_SSM_GROUPS, SSM_GROUP, 2 * STATE)[:, 0]
    n = u2.shape[0]
    kc = n // bsz
    s_re, s_im = _ssm_state(u2, v_re, v_im)
    tile_shape = (N_SSM_GROUPS * STATE // LANES, LANES)
    a_re = a16[:, :STATE].reshape(tile_shape)
    a_im = a16[:, STATE:].reshape(tile_shape)
    h_re, h_im = _ssm_scan(s_re.reshape(bsz, kc, -1), s_im.reshape(bsz, kc, -1), a_re, a_im)
    return _ssm_out(u2, h_re.reshape(n, -1), h_im.reshape(n, -1), bd, w_re, w_im)


def _swap_lane_halves(x):
    packed = pltpu.bitcast(x, jnp.uint32)
    return pltpu.bitcast(pltpu.roll(packed, HEAD_DIM, axis=1), BF16)


def _attn_out_kernel(sink_ref, q_ref, za_ref, kv_ref, kvp_ref, y2_ref, zs_ref,
                     x_ref, gate_ref, ag_ref, gw_ref, gb_ref, sg_ref, wo_ref, fg_ref,
                     o_ref, kv_scr, attn_scr, y_scr, mix_scr, out_scr,
                     *, n_tiles, tiles_per_batch):
    step = pl.program_id(0)
    cur_slot = step % 2
    prev_slot = 1 - cur_slot

    @pl.when(step == 0)
    def _():
        mix_scr[1] = jnp.zeros(mix_scr.shape[1:], BF16)

    first = jnp.minimum(step, n_tiles - 1) % tiles_per_batch == 0
    keep = jnp.where(first, 0.0, 1.0).astype(BF16)
    kv_scr[0:BLOCK, :] = kvp_ref[...] * keep
    kv_scr[BLOCK:, :] = kv_ref[...]

    older = (lax.broadcasted_iota(jnp.int32, (BLOCK, BLOCK), 1)
             > lax.broadcasted_iota(jnp.int32, (BLOCK, BLOCK), 0))
    kv_half = lax.broadcasted_iota(jnp.int32, (2 * BLOCK, LANES), 1) // HEAD_DIM
    pairs = Q_PER_KV // 2

    def both_halves(r0, col):
        tile, half = col // LANES, (col % LANES) // HEAD_DIM
        band = kv_scr[pl.ds(r0, 2 * BLOCK), tile * LANES:(tile + 1) * LANES]
        own = jnp.where(kv_half == half, band, jnp.zeros_like(band))
        other = _swap_lane_halves(own)
        return (own, other) if half == 0 else (other, own)

    contract_last = (((1,), (1,)), ((), ()))
    for gb in range(N_GROUP_BLOCKS):
        for tok in range(SSM_CHUNK):
            c0 = gb * SLAB + tok * LANES
            y_scr[gb, pl.ds(tok, CHUNK_ROWS, stride=SSM_CHUNK), :] = y2_ref[:, c0:c0 + LANES]

    def block(j, carry):
        r0 = j * BLOCK
        rows = pl.ds(r0, BLOCK)
        out_scr[j] = jnp.dot(mix_scr[prev_slot], wo_ref[:, j * OUT_CHUNK:(j + 1) * OUT_CHUNK],
                             preferred_element_type=F32)
        no_past = jnp.where(first, NEG, 0.0) if j == 0 else None
        for h in range(N_KV_HEADS):
            k_par = both_halves(r0, h * HEAD_DIM)
            v_par = both_halves(r0, KV_WIDTH + h * HEAD_DIM)
            q0 = h * Q_PER_KV * HEAD_DIM
            qs = jnp.concatenate(
                [q_ref[pl.ds(r0, BLOCK), q0 + p * PAIR:q0 + (p + 1) * PAIR] for p in range(pairs)],
                axis=0)
            acc = [None] * pairs
            for par in range(2):
                s_all = lax.dot_general(qs, k_par[par], contract_last, preferred_element_type=F32)
                probs, inv = [], []
                for p in range(pairs):
                    past = s_all[p * BLOCK:(p + 1) * BLOCK, :BLOCK]
                    if no_past is not None:
                        past = past + no_past
                    s = jnp.where(older, past, s_all[p * BLOCK:(p + 1) * BLOCK, BLOCK:])
                    sink = sink_ref[h * Q_PER_KV + 2 * p + par] * LOG2_E
                    m = jnp.max(s, axis=-1, keepdims=True)
                    e = jnp.exp2(s - m)
                    l = jnp.sum(e, axis=-1, keepdims=True) + jnp.exp2(sink - m)
                    inv.append(1.0 / l)
                    e = e.astype(BF16)
                    zero = jnp.zeros_like(e)
                    probs.append(jnp.concatenate(
                        [jnp.where(older, e, zero), jnp.where(older, zero, e)], axis=1))
                pv = jnp.dot(jnp.concatenate(probs, axis=0), v_par[par], preferred_element_type=F32)
                for p in range(pairs):
                    o = pv[p * BLOCK:(p + 1) * BLOCK] * inv[p]
                    acc[p] = o if par == 0 else acc[p] + o
            for p in range(pairs):
                attn_scr[:, q0 + p * PAIR:q0 + (p + 1) * PAIR] = acc[p]
        a = attn_scr[...]
        a = a * lax.rsqrt(jnp.mean(a * a, axis=-1, keepdims=True) + EPS) * ag_ref[...]
        za = za_ref[rows, :]
        mix_scr[cur_slot, rows, :ATTN_WIDTH] = a.astype(BF16) * (za * jax.nn.sigmoid(za))
        y = jnp.concatenate([y_scr[gb, rows, :] for gb in range(N_GROUP_BLOCKS)], axis=1)
        y = 0.5 * y * (1.0 + lax.erf(y * math.sqrt(0.5)))
        glu = jnp.dot(y.astype(BF16), gw_ref[...], preferred_element_type=F32) + gb_ref[...]
        y = y * jax.nn.sigmoid(glu)
        y = y * lax.rsqrt(jnp.mean(y * y, axis=-1, keepdims=True) + EPS) * sg_ref[...]
        zs = zs_ref[rows, :]
        mix_scr[cur_slot, rows, ATTN_WIDTH:] = y.astype(BF16) * (zs * jax.nn.sigmoid(zs))
        return carry

    for j in range(q_ref.shape[0] // BLOCK):
        block(j, 0)

    out = jnp.concatenate([out_scr[c] for c in range(out_scr.shape[0])], axis=1)
    res = x_ref[...] + gate_ref[0] * out
    o_ref[...] = res * lax.rsqrt(jnp.mean(res * res, axis=-1, keepdims=True) + EPS) * fg_ref[...]


def _attn_out(sinks, q, za, kv, y2, zs, x2, gate, attn_gain, glu_w, glu_b, ssm_gain, w_out,
              final_gain, rows_per_batch):
    t, d = x2.shape
    n_tiles = t // ROW_TILE
    tiles_per_batch = rows_per_batch // ROW_TILE
    blocks_per_tile = ROW_TILE // BLOCK
    cur = lambda i: jnp.minimum(i, n_tiles - 1)
    old = lambda i: jnp.maximum(i - 1, 0)
    row = lambda w: pl.BlockSpec((ROW_TILE, w), lambda i: (cur(i), 0))
    prev_block = pl.BlockSpec(
        (BLOCK, 2 * KV_WIDTH), lambda i: (jnp.maximum(cur(i) * blocks_per_tile - 1, 0), 0))
    out_chunks = d // OUT_CHUNK
    return pl.pallas_call(
        functools.partial(_attn_out_kernel, n_tiles=n_tiles, tiles_per_batch=tiles_per_batch),
        out_shape=jax.ShapeDtypeStruct((t, d), F32),
        grid=(n_tiles + 1,),
        in_specs=[pl.BlockSpec(memory_space=pltpu.SMEM),
                  row(ATTN_WIDTH), row(ATTN_WIDTH), row(2 * KV_WIDTH), prev_block,
                  pl.BlockSpec((CHUNK_ROWS, y2.shape[1]), lambda i: (cur(i), 0)),
                  row(SSM_WIDTH),
                  pl.BlockSpec((ROW_TILE, d), lambda i: (old(i), 0)),
                  pl.BlockSpec((1, 1, d), lambda i: (old(i) // tiles_per_batch, 0, 0)),
                  _resident((1, ATTN_WIDTH)),
                  _resident((SSM_WIDTH, SSM_WIDTH)), _resident((1, SSM_WIDTH)),
                  _resident((1, SSM_WIDTH)),
                  _resident((d, d)), _resident((1, d))],
        out_specs=pl.BlockSpec((ROW_TILE, d), lambda i: (old(i), 0)),
        scratch_shapes=[pltpu.VMEM((ROW_TILE + BLOCK, 2 * KV_WIDTH), BF16),
                        pltpu.VMEM((BLOCK, ATTN_WIDTH), F32),
                        pltpu.VMEM((N_GROUP_BLOCKS, ROW_TILE, LANES), F32),
                        pltpu.VMEM((2, ROW_TILE, d), BF16),
                        pltpu.VMEM((out_chunks, ROW_TILE, OUT_CHUNK), F32)],
        compiler_params=pltpu.CompilerParams(dimension_semantics=("arbitrary",),
                                             vmem_limit_bytes=VMEM_LIMIT),
        name="attn_out",
    )(sinks, q, za, kv, kv, y2, zs, x2, gate, attn_gain, glu_w, glu_b, ssm_gain, w_out, final_gain)


def kernel(x, c, w_ada, b_ada, norm_gain, w_in, b_in, attn_sinks, attn_out_gain, ssm_lambda_re,
           ssm_lambda_im, ssm_log_step, ssm_b_re, ssm_b_im, ssm_c_re, ssm_c_im, ssm_d, glu_w, glu_b,
           ssm_out_gain, w_out, final_gain):
    bsz, seq, d = x.shape
    assert w_ada.shape[0] == 1, "single-layer trunk only"
    x2 = x.reshape(bsz * seq, d)
    mod = _ada(c, w_ada[0], b_ada[0])
    shift, scale, gate = (m.reshape(bsz, 1, d) for m in jnp.split(mod, 3, axis=-1))
    w_in16, prep = _ssm_prep(w_in[0], ssm_lambda_re[0], ssm_lambda_im[0], ssm_log_step[0],
                             ssm_b_re[0], ssm_b_im[0], ssm_c_re[0], ssm_c_im[0], ssm_d[0])
    q, kv, za, zs, u2, w_out16 = _inproj(x2, scale, shift, norm_gain[0].reshape(1, d),
                                         w_in16, b_in[0].reshape(1, -1), w_out[0], seq)
    y2 = _s5(u2, prep, bsz)
    out = _attn_out(attn_sinks[0], q, za, kv, y2, zs, x2, gate,
                    attn_out_gain[0].reshape(1, -1), glu_w[0].astype(BF16),
                    glu_b[0].reshape(1, -1), ssm_out_gain[0].reshape(1, -1),
                    w_out16, final_gain.reshape(1, d), seq)
    return out.reshape(bsz, seq, d)
```

```python
import functools
import math

import jax
import jax.numpy as jnp
from jax import lax
from jax.experimental import pallas as pl
from jax.experimental.pallas import tpu as pltpu

F32 = jnp.float32
BF16 = jnp.bfloat16

D_MODEL = 2048
HEAD_DIM = 64
N_Q_HEADS = 24
N_KV_HEADS = 3
Q_PER_KV = 8
ATTN_WIDTH = N_Q_HEADS * HEAD_DIM
BLOCK = 128
SSM_GROUP = 16
SSM_WIDTH = D_MODEL - ATTN_WIDTH
N_SSM_GROUPS = SSM_WIDTH // SSM_GROUP
STATE = 64
EPS = 1e-5
NEG = -1e30
LOG2_E = math.log2(math.e)

LANES = 128
SSM_CHUNK = 16
CHUNK_COLS = SSM_CHUNK * SSM_GROUP
GROUPS_PER_BLOCK = LANES // SSM_GROUP
N_GROUP_BLOCKS = N_SSM_GROUPS // GROUPS_PER_BLOCK
SLAB = SSM_CHUNK * LANES
STATE_COLS = GROUPS_PER_BLOCK * STATE
KV_WIDTH = N_KV_HEADS * HEAD_DIM
PAIR = 2 * HEAD_DIM
Q_SCALE = LOG2_E * HEAD_DIM ** -0.5
PROJ_COLS = 512

VMEM_LIMIT = 56 * 1024 * 1024

ROW_TILE = 512
CHUNK_ROWS = ROW_TILE // SSM_CHUNK
ADA_COLS = 768
OUT_CHUNK = D_MODEL // (ROW_TILE // BLOCK)
SSM_TILE = 1024
SCAN_TILE = 128
SCAN_PITCH = 24


def _resident(shape):
    return pl.BlockSpec(shape, lambda *_: (0,) * len(shape), pipeline_mode=pl.Buffered(1))


def _ada_kernel(ct_ref, w_ref, b_ref, win_ref, o_ref, win16_ref):
    win16_ref[...] = win_ref[...].astype(BF16)
    ct = ct_ref[...]
    s = ct * jax.nn.sigmoid(ct)
    w = w_ref[...]
    rows = [jnp.sum(s[:, b:b + 1] * w, axis=0, keepdims=True) for b in range(ct.shape[1])]
    o_ref[...] = jnp.concatenate(rows, axis=0) + b_ref[...]


def _ada(c, w_ada, b_ada, w_in):
    bsz, d = c.shape
    n = w_ada.shape[1]
    steps = n // ADA_COLS
    slab = pl.BlockSpec((w_in.shape[0] // steps, w_in.shape[1]), lambda i: (i, 0))
    return pl.pallas_call(
        _ada_kernel,
        out_shape=[jax.ShapeDtypeStruct((bsz, n), F32), jax.ShapeDtypeStruct(w_in.shape, BF16)],
        grid=(steps,),
        in_specs=[pl.BlockSpec((d, bsz), lambda i: (0, 0)),
                  pl.BlockSpec((d, ADA_COLS), lambda i: (0, i)),
                  pl.BlockSpec((1, ADA_COLS), lambda i: (0, i)),
                  slab],
        out_specs=[pl.BlockSpec((bsz, ADA_COLS), lambda i: (0, i)), slab],
        compiler_params=pltpu.CompilerParams(dimension_semantics=("arbitrary",),
                                             vmem_limit_bytes=VMEM_LIMIT),
        name="ada",
    )(c.T, w_ada, b_ada.reshape(1, n), w_in)


def _inproj_kernel(x_ref, sc_ref, sh_ref, g_ref, w_ref, b_ref, wo_ref,
                   q_ref, kv_ref, za_ref, zs_ref, u2_ref, wo16_ref, h_scr, u_scr):
    wo16_ref[...] = wo_ref[...].astype(BF16)
    x = x_ref[...]
    var = jnp.mean(x * x, axis=-1, keepdims=True)
    gain = g_ref[...] * (1.0 + sc_ref[0])
    h_scr[...] = (x * lax.rsqrt(var + EPS) * gain + sh_ref[0]).astype(BF16)

    def project(col, width):
        acc = jnp.dot(h_scr[...], w_ref[:, col:col + width], preferred_element_type=F32)
        return acc + b_ref[:, col:col + width]

    col_kv, col_za = ATTN_WIDTH, ATTN_WIDTH + 2 * KV_WIDTH
    col_u = col_za + ATTN_WIDTH
    col_zs = col_u + SSM_WIDTH
    for c0 in range(0, ATTN_WIDTH, PROJ_COLS):
        q_ref[:, c0:c0 + PROJ_COLS] = (project(c0, PROJ_COLS) * Q_SCALE).astype(BF16)
        za_ref[:, c0:c0 + PROJ_COLS] = project(col_za + c0, PROJ_COLS).astype(BF16)
    kv_ref[...] = project(col_kv, 2 * KV_WIDTH).astype(BF16)
    zs_ref[...] = project(col_zs, SSM_WIDTH).astype(BF16)
    u = project(col_u, SSM_WIDTH)
    for gb in range(N_GROUP_BLOCKS):
        u_scr[gb] = u[:, gb * LANES:(gb + 1) * LANES]
        for tok in range(SSM_CHUNK):
            c0 = gb * SLAB + tok * LANES
            u2_ref[:, c0:c0 + LANES] = (
                u_scr[gb, pl.ds(tok, CHUNK_ROWS, stride=SSM_CHUNK), :].astype(BF16))


def _inproj(x2, scale, shift, gain, w_cat, b_cat, w_out, rows_per_batch):
    t, d = x2.shape
    widths = (ATTN_WIDTH, 2 * KV_WIDTH, ATTN_WIDTH, SSM_WIDTH)
    n_all = w_cat.shape[1]
    n_tiles = t // ROW_TILE
    tiles_per_batch = rows_per_batch // ROW_TILE
    mod_spec = pl.BlockSpec((1, 1, d), lambda i: (i // tiles_per_batch, 0, 0))
    u2_cols = N_GROUP_BLOCKS * SLAB
    wo_spec = pl.BlockSpec((w_out.shape[0] // n_tiles, w_out.shape[1]), lambda i: (i, 0))
    return pl.pallas_call(
        _inproj_kernel,
        out_shape=[jax.ShapeDtypeStruct((t, w), BF16) for w in widths]
        + [jax.ShapeDtypeStruct((t // SSM_CHUNK, u2_cols), BF16),
           jax.ShapeDtypeStruct(w_out.shape, BF16)],
        grid=(n_tiles,),
        in_specs=[pl.BlockSpec((ROW_TILE, d), lambda i: (i, 0)),
                  mod_spec, mod_spec,
                  _resident((1, d)),
                  _resident((d, n_all)),
                  _resident((1, n_all)),
                  wo_spec],
        out_specs=[pl.BlockSpec((ROW_TILE, w), lambda i: (i, 0)) for w in widths]
        + [pl.BlockSpec((CHUNK_ROWS, u2_cols), lambda i: (i, 0)), wo_spec],
        scratch_shapes=[pltpu.VMEM((ROW_TILE, d), BF16),
                        pltpu.VMEM((N_GROUP_BLOCKS, ROW_TILE, LANES), F32)],
        compiler_params=pltpu.CompilerParams(dimension_semantics=("arbitrary",),
                                             vmem_limit_bytes=VMEM_LIMIT),
        name="inproj",
    )(x2, scale, shift, gain, w_cat, b_cat, w_out)


def _ssm_prep_kernel(lr_ref, li_ref, ls_ref, br_ref, bi_ref, cr_ref, ci_ref, d_ref,
                     bd_ref, vr_ref, vi_ref, wr_ref, wi_ref, a16_ref, y_scr):
    lr, li = lr_ref[...], li_ref[...]
    step = jnp.exp(ls_ref[...])
    decay = jnp.exp(lr * step)
    ar, ai = decay * jnp.cos(li * step), decay * jnp.sin(li * step)
    den = lr * lr + li * li
    nr, ni = ar - 1.0, ai
    coef_re = (nr * lr + ni * li) / den
    coef_im = (ni * lr - nr * li) / den
    br, bi = br_ref[...], bi_ref[...]
    bbar_re = coef_re * br - coef_im * bi
    bbar_im = coef_re * bi + coef_im * br
    cr, ci = cr_ref[...], ci_ref[...]
    lo = lax.broadcasted_iota(jnp.int32, lr.shape, 2) < STATE
    pr, pi = jnp.ones_like(ar), jnp.zeros_like(ai)
    for tau in range(SSM_CHUNK + 1):
        er, ei = pr * cr - pi * ci, pr * ci + pi * cr
        if tau < SSM_CHUNK:
            y_scr[:, tau * LANES:(tau + 1) * LANES, :] = jnp.where(lo, er, ei)
            s = SSM_CHUNK - 1 - tau
            vr_ref[:, s * LANES:(s + 1) * LANES, :] = pr * bbar_re - pi * bbar_im
            vi_ref[:, s * LANES:(s + 1) * LANES, :] = pr * bbar_im + pi * bbar_re
        if tau >= 1:
            wr_ref[:, (tau - 1) * LANES:tau * LANES, :] = er
            wi_ref[:, (tau - 1) * LANES:tau * LANES, :] = -ei
        if tau == SSM_CHUNK:
            a16_ref[...] = jnp.where(lo, pr, pi)
        pr, pi = pr * ar - pi * ai, pr * ai + pi * ar
    x = jnp.where(lo, bbar_re, -bbar_im)
    row = lax.broadcasted_iota(jnp.int32, (LANES, SLAB), 0)
    col = lax.broadcasted_iota(jnp.int32, (LANES, SLAB), 1)
    same_group = row // SSM_GROUP == (col % LANES) // SSM_GROUP
    diagonal = (lax.broadcasted_iota(jnp.int32, (LANES, LANES), 0)
                == lax.broadcasted_iota(jnp.int32, (LANES, LANES), 1))
    for gb in range(N_GROUP_BLOCKS):
        k = lax.dot_general(x[gb], y_scr[gb], (((1,), (1,)), ((), ())),
                            precision=lax.Precision.HIGHEST, preferred_element_type=F32)
        k = jnp.where(same_group, k, 0.0)
        bd_ref[gb, :, LANES:] = k[:, LANES:]
        bd_ref[gb, :, :LANES] = k[:, :LANES] + jnp.where(diagonal, d_ref[gb], 0.0)


def _ssm_prep(lam_re, lam_im, log_step, b_re, b_im, c_re, c_im, d_skip):
    nb = N_GROUP_BLOCKS
    rows = lambda a: jnp.concatenate([a, a], axis=-1).reshape(nb, LANES, 2 * STATE)
    per_group = lambda a: rows(jnp.repeat(a[:, None, :], SSM_GROUP, axis=1))
    compact = jax.ShapeDtypeStruct((nb, SLAB, 2 * STATE), F32)
    return pl.pallas_call(
        _ssm_prep_kernel,
        out_shape=[jax.ShapeDtypeStruct((nb, LANES, SLAB), F32),
                   compact, compact,
                   compact, compact,
                   jax.ShapeDtypeStruct((nb, LANES, 2 * STATE), F32)],
        scratch_shapes=[pltpu.VMEM((nb, SLAB, 2 * STATE), F32)],
        compiler_params=pltpu.CompilerParams(vmem_limit_bytes=VMEM_LIMIT),
        name="ssm_prep",
    )(per_group(lam_re), per_group(lam_im),
      jnp.broadcast_to(jnp.repeat(log_step, SSM_GROUP).reshape(nb, LANES, 1), (nb, LANES, 2 * STATE)),
      rows(jnp.swapaxes(b_re, 1, 2)), rows(jnp.swapaxes(b_im, 1, 2)), rows(c_re), rows(c_im),
      jnp.broadcast_to(d_skip.reshape(nb, 1, LANES), (nb, LANES, LANES)))


def _expand_block_diagonal(re_ref, im_ref, dst):
    shape = (SLAB, 2 * STATE)
    row_group = (lax.broadcasted_iota(jnp.int32, shape, 0) % LANES) // SSM_GROUP
    half = lax.broadcasted_iota(jnp.int32, shape, 1) // STATE
    for j in range(STATE_COLS // LANES):
        own = row_group == 2 * j + half
        for src, c0 in ((re_ref, 0), (im_ref, STATE_COLS)):
            dst[:, c0 + j * LANES:c0 + (j + 1) * LANES] = jnp.where(own, src[0], 0.0).astype(BF16)


def _ssm_state_kernel(u_ref, vr_ref, vi_ref, sr_ref, si_ref, v_scr):
    @pl.when(pl.program_id(1) == 0)
    def _():
        _expand_block_diagonal(vr_ref, vi_ref, v_scr)

    s = jnp.dot(u_ref[...], v_scr[...], preferred_element_type=F32)
    sr_ref[...] = s[:, :STATE_COLS]
    si_ref[...] = s[:, STATE_COLS:]


def _ssm_state(u2, v_re, v_im):
    n = u2.shape[0]
    state = pl.BlockSpec((SSM_TILE, STATE_COLS), lambda gb, i: (i, gb))
    compact = pl.BlockSpec((1, SLAB, 2 * STATE), lambda gb, i: (gb, 0, 0))
    return pl.pallas_call(
        _ssm_state_kernel,
        out_shape=[jax.ShapeDtypeStruct((n, N_GROUP_BLOCKS * STATE_COLS), F32)] * 2,
        grid=(N_GROUP_BLOCKS, n // SSM_TILE),
        in_specs=[pl.BlockSpec((SSM_TILE, SLAB), lambda gb, i: (i, gb)), compact, compact],
        out_specs=[state, state],
        scratch_shapes=[pltpu.VMEM((SLAB, 2 * STATE_COLS), BF16)],
        compiler_params=pltpu.CompilerParams(dimension_semantics=("arbitrary", "arbitrary"),
                                             vmem_limit_bytes=VMEM_LIMIT),
        name="ssm_state",
    )(u2, v_re, v_im)


def _ssm_scan_kernel(sr_ref, si_ref, ar_ref, ai_ref, hr_ref, hi_ref, cr_scr, ci_scr, s_scr, h_scr):
    @pl.when(pl.program_id(0) == 0)
    def _():
        cr_scr[...] = jnp.zeros_like(cr_scr)
        ci_scr[...] = jnp.zeros_like(ci_scr)

    ar, ai = ar_ref[...], ai_ref[...]
    nb = sr_ref.shape[0]
    tiles = ar.shape[0]
    for b in range(nb):
        for part, ref in enumerate((sr_ref, si_ref)):
            for r in range(tiles):
                s_scr[part, b, pl.ds(r, SCAN_TILE, stride=SCAN_PITCH), :] = (
                    ref[b, :, r * LANES:(r + 1) * LANES])

    def body(k, carry):
        rows = pl.ds(pl.multiple_of(k * SCAN_PITCH, 8), tiles)
        out = []
        for b in range(nb):
            hr, hi = carry[2 * b], carry[2 * b + 1]
            h_scr[0, b, rows, :] = hr
            h_scr[1, b, rows, :] = hi
            out += [ar * hr - ai * hi + s_scr[0, b, rows, :], ar * hi + ai * hr + s_scr[1, b, rows, :]]
        return tuple(out)

    init = tuple(scr[b] for b in range(nb) for scr in (cr_scr, ci_scr))
    fin = lax.fori_loop(0, SCAN_TILE, body, init, unroll=8)
    for b in range(nb):
        cr_scr[b] = fin[2 * b]
        ci_scr[b] = fin[2 * b + 1]
        for part, ref in enumerate((hr_ref, hi_ref)):
            for r in range(tiles):
                ref[b, :, r * LANES:(r + 1) * LANES] = (
                    h_scr[part, b, pl.ds(r, SCAN_TILE, stride=SCAN_PITCH), :].astype(BF16))


def _ssm_scan(s_re, s_im, a_re, a_im):
    bsz, k, cols = s_re.shape
    rows, lanes = a_re.shape
    blk = pl.BlockSpec((bsz, SCAN_TILE, cols), lambda i: (0, i, 0))
    return pl.pallas_call(
        _ssm_scan_kernel,
        out_shape=[jax.ShapeDtypeStruct(s_re.shape, BF16)] * 2,
        grid=(k // SCAN_TILE,),
        in_specs=[blk, blk, _resident((rows, lanes)), _resident((rows, lanes))],
        out_specs=[blk, blk],
        scratch_shapes=[pltpu.VMEM((bsz, rows, lanes), F32)] * 2
        + [pltpu.VMEM((2, bsz, SCAN_TILE * SCAN_PITCH, lanes), F32)] * 2,
        compiler_params=pltpu.CompilerParams(dimension_semantics=("arbitrary",)),
        name="ssm_scan",
    )(s_re, s_im, a_re, a_im)


def _ssm_out_kernel(u_ref, hr_ref, hi_ref, bd_ref, wr_ref, wi_ref, y_ref, t_scr, wt_scr):
    @pl.when(pl.program_id(1) == 0)
    def _():
        bd = bd_ref[0].astype(BF16)
        for s in range(SSM_CHUNK):
            rows = slice(s * LANES, (s + 1) * LANES)
            if s:
                t_scr[rows, :s * LANES] = jnp.zeros((LANES, s * LANES), BF16)
            t_scr[rows, s * LANES:] = bd[:, :(SSM_CHUNK - s) * LANES]
        _expand_block_diagonal(wr_ref, wi_ref, wt_scr)

    pair = 2 * LANES
    contract_last = (((1,), (1,)), ((), ()))
    for tb in range(SLAB // pair):
        k_rows = (tb + 1) * pair
        cols = slice(tb * pair, (tb + 1) * pair)
        y = jnp.dot(u_ref[:, :k_rows], t_scr[:k_rows, cols], preferred_element_type=F32)
        y += lax.dot_general(hr_ref[...], wt_scr[cols, :STATE_COLS], contract_last,
                             preferred_element_type=F32)
        y += lax.dot_general(hi_ref[...], wt_scr[cols, STATE_COLS:], contract_last,
                             preferred_element_type=F32)
        y_ref[:, cols] = y


def _ssm_out(u2, h_re, h_im, bd, w_re, w_im):
    n = u2.shape[0]
    state = pl.BlockSpec((SSM_TILE, STATE_COLS), lambda gb, i: (i, gb))
    slab = pl.BlockSpec((SSM_TILE, SLAB), lambda gb, i: (i, gb))
    compact = pl.BlockSpec((1, SLAB, 2 * STATE), lambda gb, i: (gb, 0, 0))
    return pl.pallas_call(
        _ssm_out_kernel,
        out_shape=jax.ShapeDtypeStruct(u2.shape, F32),
        grid=(N_GROUP_BLOCKS, n // SSM_TILE),
        in_specs=[slab, state, state,
                  pl.BlockSpec((1, LANES, SLAB), lambda gb, i: (gb, 0, 0)), compact, compact],
        out_specs=slab,
        scratch_shapes=[pltpu.VMEM((SLAB, SLAB), BF16), pltpu.VMEM((SLAB, 2 * STATE_COLS), BF16)],
        compiler_params=pltpu.CompilerParams(dimension_semantics=("arbitrary", "arbitrary"),
                                             vmem_limit_bytes=VMEM_LIMIT),
        name="ssm_out",
    )(u2, h_re, h_im, bd, w_re, w_im)


def _s5(u2, prep, bsz):
    bd, v_re, v_im, w_re, w_im = prep[:5]
    a16 = prep[5].reshape(N_SSM_GROUPS, SSM_GROUP, 2 * STATE)[:, 0]
    n = u2.shape[0]
    kc = n // bsz
    s_re, s_im = _ssm_state(u2, v_re, v_im)
    tile_shape = (N_SSM_GROUPS * STATE // LANES, LANES)
    a_re = a16[:, :STATE].reshape(tile_shape)
    a_im = a16[:, STATE:].reshape(tile_shape)
    h_re, h_im = _ssm_scan(s_re.reshape(bsz, kc, -1), s_im.reshape(bsz, kc, -1), a_re, a_im)
    return _ssm_out(u2, h_re.reshape(n, -1), h_im.reshape(n, -1), bd, w_re, w_im)


def _swap_lane_halves(x):
    packed = pltpu.bitcast(x, jnp.uint32)
    return pltpu.bitcast(pltpu.roll(packed, HEAD_DIM, axis=1), BF16)


def _attn_out_kernel(sink_ref, q_ref, za_ref, kv_ref, kvp_ref, y2_ref, zs_ref,
                     x_ref, gate_ref, ag_ref, gw_ref, gb_ref, sg_ref, wo_ref, fg_ref,
                     o_ref, kv_scr, attn_scr, y_scr, mix_scr, out_scr,
                     *, n_tiles, tiles_per_batch):
    step = pl.program_id(0)
    cur_slot = step % 2
    prev_slot = 1 - cur_slot

    @pl.when(step == 0)
    def _():
        mix_scr[1] = jnp.zeros(mix_scr.shape[1:], BF16)

    first = jnp.minimum(step, n_tiles - 1) % tiles_per_batch == 0
    keep = jnp.where(first, 0.0, 1.0).astype(BF16)
    kv_scr[0:BLOCK, :] = kvp_ref[...] * keep
    kv_scr[BLOCK:, :] = kv_ref[...]

    older = (lax.broadcasted_iota(jnp.int32, (BLOCK, BLOCK), 1)
             > lax.broadcasted_iota(jnp.int32, (BLOCK, BLOCK), 0))
    kv_half = lax.broadcasted_iota(jnp.int32, (2 * BLOCK, LANES), 1) // HEAD_DIM
    pairs = Q_PER_KV // 2

    def both_halves(r0, col):
        tile, half = col // LANES, (col % LANES) // HEAD_DIM
        band = kv_scr[pl.ds(r0, 2 * BLOCK), tile * LANES:(tile + 1) * LANES]
        own = jnp.where(kv_half == half, band, jnp.zeros_like(band))
        other = _swap_lane_halves(own)
        return (own, other) if half == 0 else (other, own)

    contract_last = (((1,), (1,)), ((), ()))
    for gb in range(N_GROUP_BLOCKS):
        for tok in range(SSM_CHUNK):
            c0 = gb * SLAB + tok * LANES
            y_scr[gb, pl.ds(tok, CHUNK_ROWS, stride=SSM_CHUNK), :] = y2_ref[:, c0:c0 + LANES]

    def block(j, carry):
        r0 = j * BLOCK
        rows = pl.ds(r0, BLOCK)
        out_scr[j] = jnp.dot(mix_scr[prev_slot], wo_ref[:, j * OUT_CHUNK:(j + 1) * OUT_CHUNK],
                             preferred_element_type=F32)
        no_past = jnp.where(first, NEG, 0.0) if j == 0 else None
        for h in range(N_KV_HEADS):
            k_par = both_halves(r0, h * HEAD_DIM)
            v_par = both_halves(r0, KV_WIDTH + h * HEAD_DIM)
            q0 = h * Q_PER_KV * HEAD_DIM
            qs = jnp.concatenate(
                [q_ref[pl.ds(r0, BLOCK), q0 + p * PAIR:q0 + (p + 1) * PAIR] for p in range(pairs)],
                axis=0)
            acc = [None] * pairs
            for par in range(2):
                s_all = lax.dot_general(qs, k_par[par], contract_last, preferred_element_type=F32)
                probs, inv = [], []
                for p in range(pairs):
                    past = s_all[p * BLOCK:(p + 1) * BLOCK, :BLOCK]
                    if no_past is not None:
                        past = past + no_past
                    s = jnp.where(older, past, s_all[p * BLOCK:(p + 1) * BLOCK, BLOCK:])
                    sink = sink_ref[h * Q_PER_KV + 2 * p + par] * LOG2_E
                    m = jnp.max(s, axis=-1, keepdims=True)
                    e = jnp.exp2(s - m)
                    l = jnp.sum(e, axis=-1, keepdims=True) + jnp.exp2(sink - m)
                    inv.append(1.0 / l)
                    e = e.astype(BF16)
                    zero = jnp.zeros_like(e)
                    probs.append(jnp.concatenate(
                        [jnp.where(older, e, zero), jnp.where(older, zero, e)], axis=1))
                pv = jnp.dot(jnp.concatenate(probs, axis=0), v_par[par], preferred_element_type=F32)
                for p in range(pairs):
                    o = pv[p * BLOCK:(p + 1) * BLOCK] * inv[p]
                    acc[p] = o if par == 0 else acc[p] + o
            for p in range(pairs):
                attn_scr[:, q0 + p * PAIR:q0 + (p + 1) * PAIR] = acc[p]
        a = attn_scr[...]
        a = a * lax.rsqrt(jnp.mean(a * a, axis=-1, keepdims=True) + EPS) * ag_ref[...]
        za = za_ref[rows, :]
        mix_scr[cur_slot, rows, :ATTN_WIDTH] = a.astype(BF16) * (za * jax.nn.sigmoid(za))
        y = jnp.concatenate([y_scr[gb, rows, :] for gb in range(N_GROUP_BLOCKS)], axis=1)
        y = 0.5 * y * (1.0 + lax.erf(y * math.sqrt(0.5)))
        glu = jnp.dot(y.astype(BF16), gw_ref[...], preferred_element_type=F32) + gb_ref[...]
        y = y * jax.nn.sigmoid(glu)
        y = y * lax.rsqrt(jnp.mean(y * y, axis=-1, keepdims=True) + EPS) * sg_ref[...]
        zs = zs_ref[rows, :]
        mix_scr[cur_slot, rows, ATTN_WIDTH:] = y.astype(BF16) * (zs * jax.nn.sigmoid(zs))
        return carry

    for j in range(q_ref.shape[0] // BLOCK):
        block(j, 0)

    out = jnp.concatenate([out_scr[c] for c in range(out_scr.shape[0])], axis=1)
    res = x_ref[...] + gate_ref[0] * out
    o_ref[...] = res * lax.rsqrt(jnp.mean(res * res, axis=-1, keepdims=True) + EPS) * fg_ref[...]


def _attn_out(sinks, q, za, kv, y2, zs, x2, gate, attn_gain, glu_w, glu_b, ssm_gain, w_out,
              final_gain, rows_per_batch):
    t, d = x2.shape
    n_tiles = t // ROW_TILE
    tiles_per_batch = rows_per_batch // ROW_TILE
    blocks_per_tile = ROW_TILE // BLOCK
    cur = lambda i: jnp.minimum(i, n_tiles - 1)
    old = lambda i: jnp.maximum(i - 1, 0)
    row = lambda w: pl.BlockSpec((ROW_TILE, w), lambda i: (cur(i), 0))
    prev_block = pl.BlockSpec(
        (BLOCK, 2 * KV_WIDTH), lambda i: (jnp.maximum(cur(i) * blocks_per_tile - 1, 0), 0))
    out_chunks = d // OUT_CHUNK
    return pl.pallas_call(
        functools.partial(_attn_out_kernel, n_tiles=n_tiles, tiles_per_batch=tiles_per_batch),
        out_shape=jax.ShapeDtypeStruct((t, d), F32),
        grid=(n_tiles + 1,),
        in_specs=[pl.BlockSpec(memory_space=pltpu.SMEM),
                  row(ATTN_WIDTH), row(ATTN_WIDTH), row(2 * KV_WIDTH), prev_block,
                  pl.BlockSpec((CHUNK_ROWS, y2.shape[1]), lambda i: (cur(i), 0)),
                  row(SSM_WIDTH),
                  pl.BlockSpec((ROW_TILE, d), lambda i: (old(i), 0)),
                  pl.BlockSpec((1, 1, d), lambda i: (old(i) // tiles_per_batch, 0, 0)),
                  _resident((1, ATTN_WIDTH)),
                  _resident((SSM_WIDTH, SSM_WIDTH)), _resident((1, SSM_WIDTH)),
                  _resident((1, SSM_WIDTH)),
                  _resident((d, d)), _resident((1, d))],
        out_specs=pl.BlockSpec((ROW_TILE, d), lambda i: (old(i), 0)),
        scratch_shapes=[pltpu.VMEM((ROW_TILE + BLOCK, 2 * KV_WIDTH), BF16),
                        pltpu.VMEM((BLOCK, ATTN_WIDTH), F32),
                        pltpu.VMEM((N_GROUP_BLOCKS, ROW_TILE, LANES), F32),
                        pltpu.VMEM((2, ROW_TILE, d), BF16),
                        pltpu.VMEM((out_chunks, ROW_TILE, OUT_CHUNK), F32)],
        compiler_params=pltpu.CompilerParams(dimension_semantics=("arbitrary",),
                                             vmem_limit_bytes=VMEM_LIMIT),
        name="attn_out",
    )(sinks, q, za, kv, kv, y2, zs, x2, gate, attn_gain, glu_w, glu_b, ssm_gain, w_out, final_gain)


def kernel(x, c, w_ada, b_ada, norm_gain, w_in, b_in, attn_sinks, attn_out_gain, ssm_lambda_re,
           ssm_lambda_im, ssm_log_step, ssm_b_re, ssm_b_im, ssm_c_re, ssm_c_im, ssm_d, glu_w, glu_b,
           ssm_out_gain, w_out, final_gain):
    bsz, seq, d = x.shape
    assert w_ada.shape[0] == 1, "single-layer trunk only"
    x2 = x.reshape(bsz * seq, d)
    mod, w_in16 = _ada(c, w_ada[0], b_ada[0], w_in[0])
    shift, scale, gate = (m.reshape(bsz, 1, d) for m in jnp.split(mod, 3, axis=-1))
    q, kv, za, zs, u2, w_out16 = _inproj(x2, scale, shift, norm_gain[0].reshape(1, d),
                                         w_in16, b_in[0].reshape(1, -1), w_out[0], seq)
    prep = _ssm_prep(ssm_lambda_re[0], ssm_lambda_im[0], ssm_log_step[0], ssm_b_re[0],
                     ssm_b_im[0], ssm_c_re[0], ssm_c_im[0], ssm_d[0])
    y2 = _s5(u2, prep, bsz)
    out = _attn_out(attn_sinks[0], q, za, kv, y2, zs, x2, gate,
                    attn_out_gain[0].reshape(1, -1), glu_w[0].astype(BF16),
                    glu_b[0].reshape(1, -1), ssm_out_gain[0].reshape(1, -1),
                    w_out16, final_gain.reshape(1, d), seq)
    return out.reshape(bsz, seq, d)
```

```python
import functools
import math

import jax
import jax.numpy as jnp
from jax import lax
from jax.experimental import pallas as pl
from jax.experimental.pallas import tpu as pltpu

F32 = jnp.float32
BF16 = jnp.bfloat16

D_MODEL = 2048
HEAD_DIM = 64
N_Q_HEADS = 24
N_KV_HEADS = 3
Q_PER_KV = 8
ATTN_WIDTH = N_Q_HEADS * HEAD_DIM
BLOCK = 128
SSM_GROUP = 16
SSM_WIDTH = D_MODEL - ATTN_WIDTH
N_SSM_GROUPS = SSM_WIDTH // SSM_GROUP
STATE = 64
EPS = 1e-5
NEG = -1e30
LOG2_E = math.log2(math.e)

LANES = 128
SSM_CHUNK = 16
GROUPS_PER_BLOCK = LANES // SSM_GROUP
N_GROUP_BLOCKS = N_SSM_GROUPS // GROUPS_PER_BLOCK
SLAB = SSM_CHUNK * LANES
STATE_COLS = GROUPS_PER_BLOCK * STATE
KV_WIDTH = N_KV_HEADS * HEAD_DIM
PAIR = 2 * HEAD_DIM
Q_SCALE = LOG2_E * HEAD_DIM ** -0.5
PROJ_COLS = 512

VMEM_LIMIT = 56 * 1024 * 1024

ROW_TILE = 512
CHUNK_ROWS = ROW_TILE // SSM_CHUNK
ADA_COLS = 768
OUT_CHUNK = D_MODEL // (ROW_TILE // BLOCK)
SSM_TILE = 1024
SCAN_TILE = 128
SCAN_PITCH = 24


def _resident(shape):
    return pl.BlockSpec(shape, lambda *_: (0,) * len(shape), pipeline_mode=pl.Buffered(1))


def _ada_kernel(ct_ref, w_ref, b_ref, win_ref, o_ref, win16_ref):
    win16_ref[...] = win_ref[...].astype(BF16)
    ct = ct_ref[...]
    s = ct * jax.nn.sigmoid(ct)
    w = w_ref[...]
    rows = [jnp.sum(s[:, b:b + 1] * w, axis=0, keepdims=True) for b in range(ct.shape[1])]
    o_ref[...] = jnp.concatenate(rows, axis=0) + b_ref[...]


def _ada(c, w_ada, b_ada, w_in):
    bsz, d = c.shape
    n = w_ada.shape[1]
    steps = n // ADA_COLS
    slab = pl.BlockSpec((w_in.shape[0] // steps, w_in.shape[1]), lambda i: (i, 0))
    return pl.pallas_call(
        _ada_kernel,
        out_shape=[jax.ShapeDtypeStruct((bsz, n), F32), jax.ShapeDtypeStruct(w_in.shape, BF16)],
        grid=(steps,),
        in_specs=[pl.BlockSpec((d, bsz), lambda i: (0, 0)),
                  pl.BlockSpec((d, ADA_COLS), lambda i: (0, i)),
                  pl.BlockSpec((1, ADA_COLS), lambda i: (0, i)),
                  slab],
        out_specs=[pl.BlockSpec((bsz, ADA_COLS), lambda i: (0, i)), slab],
        compiler_params=pltpu.CompilerParams(dimension_semantics=("arbitrary",),
                                             vmem_limit_bytes=VMEM_LIMIT),
        name="ada",
    )(c.T, w_ada, b_ada.reshape(1, n), w_in)


def _inproj_kernel(x_ref, sc_ref, sh_ref, g_ref, w_ref, b_ref, wo_ref,
                   q_ref, kv_ref, za_ref, zs_ref, u2_ref, wo16_ref, h_scr, u_scr):
    wo16_ref[...] = wo_ref[...].astype(BF16)
    x = x_ref[...]
    var = jnp.mean(x * x, axis=-1, keepdims=True)
    gain = g_ref[...] * (1.0 + sc_ref[0])
    h_scr[...] = (x * lax.rsqrt(var + EPS) * gain + sh_ref[0]).astype(BF16)

    def project(col, width):
        acc = jnp.dot(h_scr[...], w_ref[:, col:col + width], preferred_element_type=F32)
        return acc + b_ref[:, col:col + width]

    col_kv, col_za = ATTN_WIDTH, ATTN_WIDTH + 2 * KV_WIDTH
    col_u = col_za + ATTN_WIDTH
    col_zs = col_u + SSM_WIDTH
    for c0 in range(0, ATTN_WIDTH, PROJ_COLS):
        q_ref[:, c0:c0 + PROJ_COLS] = (project(c0, PROJ_COLS) * Q_SCALE).astype(BF16)
        za_ref[:, c0:c0 + PROJ_COLS] = project(col_za + c0, PROJ_COLS).astype(BF16)
    kv_ref[...] = project(col_kv, 2 * KV_WIDTH).astype(BF16)
    zs_ref[...] = project(col_zs, SSM_WIDTH).astype(BF16)
    u = project(col_u, SSM_WIDTH)
    for gb in range(N_GROUP_BLOCKS):
        u_scr[gb] = u[:, gb * LANES:(gb + 1) * LANES]
        for tok in range(SSM_CHUNK):
            c0 = gb * SLAB + tok * LANES
            u2_ref[:, c0:c0 + LANES] = (
                u_scr[gb, pl.ds(tok, CHUNK_ROWS, stride=SSM_CHUNK), :].astype(BF16))


def _inproj(x2, scale, shift, gain, w_cat, b_cat, w_out, rows_per_batch):
    t, d = x2.shape
    widths = (ATTN_WIDTH, 2 * KV_WIDTH, ATTN_WIDTH, SSM_WIDTH)
    n_all = w_cat.shape[1]
    n_tiles = t // ROW_TILE
    tiles_per_batch = rows_per_batch // ROW_TILE
    mod_spec = pl.BlockSpec((1, 1, d), lambda i: (i // tiles_per_batch, 0, 0))
    u2_cols = N_GROUP_BLOCKS * SLAB
    wo_spec = pl.BlockSpec((w_out.shape[0] // n_tiles, w_out.shape[1]), lambda i: (i, 0))
    return pl.pallas_call(
        _inproj_kernel,
        out_shape=[jax.ShapeDtypeStruct((t, w), BF16) for w in widths]
        + [jax.ShapeDtypeStruct((t // SSM_CHUNK, u2_cols), BF16),
           jax.ShapeDtypeStruct(w_out.shape, BF16)],
        grid=(n_tiles,),
        in_specs=[pl.BlockSpec((ROW_TILE, d), lambda i: (i, 0)),
                  mod_spec, mod_spec,
                  _resident((1, d)),
                  _resident((d, n_all)),
                  _resident((1, n_all)),
                  wo_spec],
        out_specs=[pl.BlockSpec((ROW_TILE, w), lambda i: (i, 0)) for w in widths]
        + [pl.BlockSpec((CHUNK_ROWS, u2_cols), lambda i: (i, 0)), wo_spec],
        scratch_shapes=[pltpu.VMEM((ROW_TILE, d), BF16),
                        pltpu.VMEM((N_GROUP_BLOCKS, ROW_TILE, LANES), F32)],
        compiler_params=pltpu.CompilerParams(dimension_semantics=("arbitrary",),
                                             vmem_limit_bytes=VMEM_LIMIT),
        name="inproj",
    )(x2, scale, shift, gain, w_cat, b_cat, w_out)


def _ssm_prep_kernel(lr_ref, li_ref, ls_ref, br_ref, bi_ref, cr_ref, ci_ref, d_ref,
                     bd_ref, vr_ref, vi_ref, wr_ref, wi_ref, a16_ref, y_scr):
    lr, li = lr_ref[...], li_ref[...]
    step = jnp.exp(ls_ref[...])
    decay = jnp.exp(lr * step)
    ar, ai = decay * jnp.cos(li * step), decay * jnp.sin(li * step)
    den = lr * lr + li * li
    nr, ni = ar - 1.0, ai
    coef_re = (nr * lr + ni * li) / den
    coef_im = (ni * lr - nr * li) / den
    br, bi = br_ref[...], bi_ref[...]
    bbar_re = coef_re * br - coef_im * bi
    bbar_im = coef_re * bi + coef_im * br
    cr, ci = cr_ref[...], ci_ref[...]
    lo = lax.broadcasted_iota(jnp.int32, lr.shape, 2) < STATE
    pr, pi = jnp.ones_like(ar), jnp.zeros_like(ai)
    for tau in range(SSM_CHUNK + 1):
        er, ei = pr * cr - pi * ci, pr * ci + pi * cr
        if tau < SSM_CHUNK:
            y_scr[:, tau * LANES:(tau + 1) * LANES, :] = jnp.where(lo, er, ei)
            s = SSM_CHUNK - 1 - tau
            vr_ref[:, s * LANES:(s + 1) * LANES, :] = pr * bbar_re - pi * bbar_im
            vi_ref[:, s * LANES:(s + 1) * LANES, :] = pr * bbar_im + pi * bbar_re
        if tau >= 1:
            wr_ref[:, (tau - 1) * LANES:tau * LANES, :] = er
            wi_ref[:, (tau - 1) * LANES:tau * LANES, :] = -ei
        if tau == SSM_CHUNK:
            a16_ref[...] = jnp.where(lo, pr, pi)
        pr, pi = pr * ar - pi * ai, pr * ai + pi * ar
    x = jnp.where(lo, bbar_re, -bbar_im)
    row = lax.broadcasted_iota(jnp.int32, (LANES, SLAB), 0)
    col = lax.broadcasted_iota(jnp.int32, (LANES, SLAB), 1)
    same_group = row // SSM_GROUP == (col % LANES) // SSM_GROUP
    diagonal = (lax.broadcasted_iota(jnp.int32, (LANES, LANES), 0)
                == lax.broadcasted_iota(jnp.int32, (LANES, LANES), 1))
    for gb in range(N_GROUP_BLOCKS):
        k = lax.dot_general(x[gb], y_scr[gb], (((1,), (1,)), ((), ())),
                            precision=lax.Precision.HIGHEST, preferred_element_type=F32)
        k = jnp.where(same_group, k, 0.0)
        bd_ref[gb, :, LANES:] = k[:, LANES:]
        bd_ref[gb, :, :LANES] = k[:, :LANES] + jnp.where(diagonal, d_ref[gb], 0.0)


def _ssm_prep(lam_re, lam_im, log_step, b_re, b_im, c_re, c_im, d_skip):
    nb = N_GROUP_BLOCKS
    rows = lambda a: jnp.concatenate([a, a], axis=-1).reshape(nb, LANES, 2 * STATE)
    per_group = lambda a: rows(jnp.repeat(a[:, None, :], SSM_GROUP, axis=1))
    compact = jax.ShapeDtypeStruct((nb, SLAB, 2 * STATE), F32)
    return pl.pallas_call(
        _ssm_prep_kernel,
        out_shape=[jax.ShapeDtypeStruct((nb, LANES, SLAB), F32),
                   compact, compact,
                   compact, compact,
                   jax.ShapeDtypeStruct((nb, LANES, 2 * STATE), F32)],
        scratch_shapes=[pltpu.VMEM((nb, SLAB, 2 * STATE), F32)],
        compiler_params=pltpu.CompilerParams(vmem_limit_bytes=VMEM_LIMIT),
        name="ssm_prep",
    )(per_group(lam_re), per_group(lam_im),
      jnp.broadcast_to(jnp.repeat(log_step, SSM_GROUP).reshape(nb, LANES, 1), (nb, LANES, 2 * STATE)),
      rows(jnp.swapaxes(b_re, 1, 2)), rows(jnp.swapaxes(b_im, 1, 2)), rows(c_re), rows(c_im),
      jnp.broadcast_to(d_skip.reshape(nb, 1, LANES), (nb, LANES, LANES)))


def _expand_block_diagonal(re_ref, im_ref, dst):
    shape = (SLAB, 2 * STATE)
    row_group = (lax.broadcasted_iota(jnp.int32, shape, 0) % LANES) // SSM_GROUP
    half = lax.broadcasted_iota(jnp.int32, shape, 1) // STATE
    for j in range(STATE_COLS // LANES):
        own = row_group == 2 * j + half
        for src, c0 in ((re_ref, 0), (im_ref, STATE_COLS)):
            dst[:, c0 + j * LANES:c0 + (j + 1) * LANES] = jnp.where(own, src[0], 0.0).astype(BF16)


def _ssm_state_kernel(u_ref, vr_ref, vi_ref, sr_ref, si_ref, v_scr):
    @pl.when(pl.program_id(1) == 0)
    def _():
        _expand_block_diagonal(vr_ref, vi_ref, v_scr)

    s = jnp.dot(u_ref[...], v_scr[...], preferred_element_type=F32)
    sr_ref[...] = s[:, :STATE_COLS]
    si_ref[...] = s[:, STATE_COLS:]


def _ssm_state(u2, v_re, v_im):
    n = u2.shape[0]
    state = pl.BlockSpec((SSM_TILE, STATE_COLS), lambda gb, i: (i, gb))
    compact = pl.BlockSpec((1, SLAB, 2 * STATE), lambda gb, i: (gb, 0, 0))
    return pl.pallas_call(
        _ssm_state_kernel,
        out_shape=[jax.ShapeDtypeStruct((n, N_GROUP_BLOCKS * STATE_COLS), F32)] * 2,
        grid=(N_GROUP_BLOCKS, n // SSM_TILE),
        in_specs=[pl.BlockSpec((SSM_TILE, SLAB), lambda gb, i: (i, gb)), compact, compact],
        out_specs=[state, state],
        scratch_shapes=[pltpu.VMEM((SLAB, 2 * STATE_COLS), BF16)],
        compiler_params=pltpu.CompilerParams(dimension_semantics=("arbitrary", "arbitrary"),
                                             vmem_limit_bytes=VMEM_LIMIT),
        name="ssm_state",
    )(u2, v_re, v_im)


def _ssm_scan_kernel(sr_ref, si_ref, ar_ref, ai_ref, hr_ref, hi_ref, cr_scr, ci_scr, s_scr, h_scr):
    @pl.when(pl.program_id(0) == 0)
    def _():
        cr_scr[...] = jnp.zeros_like(cr_scr)
        ci_scr[...] = jnp.zeros_like(ci_scr)

    ar, ai = ar_ref[...], ai_ref[...]
    nb = sr_ref.shape[0]
    tiles = ar.shape[0]
    for b in range(nb):
        for part, ref in enumerate((sr_ref, si_ref)):
            for r in range(tiles):
                s_scr[part, b, pl.ds(r, SCAN_TILE, stride=SCAN_PITCH), :] = (
                    ref[b, :, r * LANES:(r + 1) * LANES])

    def body(k, carry):
        rows = pl.ds(pl.multiple_of(k * SCAN_PITCH, 8), tiles)
        out = []
        for b in range(nb):
            hr, hi = carry[2 * b], carry[2 * b + 1]
            h_scr[0, b, rows, :] = hr
            h_scr[1, b, rows, :] = hi
            out += [ar * hr - ai * hi + s_scr[0, b, rows, :], ar * hi + ai * hr + s_scr[1, b, rows, :]]
        return tuple(out)

    init = tuple(scr[b] for b in range(nb) for scr in (cr_scr, ci_scr))
    fin = lax.fori_loop(0, SCAN_TILE, body, init, unroll=8)
    for b in range(nb):
        cr_scr[b] = fin[2 * b]
        ci_scr[b] = fin[2 * b + 1]
        for part, ref in enumerate((hr_ref, hi_ref)):
            for r in range(tiles):
                ref[b, :, r * LANES:(r + 1) * LANES] = (
                    h_scr[part, b, pl.ds(r, SCAN_TILE, stride=SCAN_PITCH), :].astype(BF16))


def _ssm_scan(s_re, s_im, a_re, a_im):
    bsz, k, cols = s_re.shape
    rows, lanes = a_re.shape
    blk = pl.BlockSpec((bsz, SCAN_TILE, cols), lambda i: (0, i, 0))
    return pl.pallas_call(
        _ssm_scan_kernel,
        out_shape=[jax.ShapeDtypeStruct(s_re.shape, BF16)] * 2,
        grid=(k // SCAN_TILE,),
        in_specs=[blk, blk, _resident((rows, lanes)), _resident((rows, lanes))],
        out_specs=[blk, blk],
        scratch_shapes=[pltpu.VMEM((bsz, rows, lanes), F32)] * 2
        + [pltpu.VMEM((2, bsz, SCAN_TILE * SCAN_PITCH, lanes), F32)] * 2,
        compiler_params=pltpu.CompilerParams(dimension_semantics=("arbitrary",)),
        name="ssm_scan",
    )(s_re, s_im, a_re, a_im)


def _ssm_out_kernel(u_ref, hr_ref, hi_ref, bd_ref, wr_ref, wi_ref, y_ref, t_scr, wt_scr):
    @pl.when(pl.program_id(1) == 0)
    def _():
        bd = bd_ref[0].astype(BF16)
        for s in range(SSM_CHUNK):
            rows = slice(s * LANES, (s + 1) * LANES)
            if s:
                t_scr[rows, :s * LANES] = jnp.zeros((LANES, s * LANES), BF16)
            t_scr[rows, s * LANES:] = bd[:, :(SSM_CHUNK - s) * LANES]
        _expand_block_diagonal(wr_ref, wi_ref, wt_scr)

    pair = 2 * LANES
    contract_last = (((1,), (1,)), ((), ()))
    for tb in range(SLAB // pair):
        k_rows = (tb + 1) * pair
        cols = slice(tb * pair, (tb + 1) * pair)
        y = jnp.dot(u_ref[:, :k_rows], t_scr[:k_rows, cols], preferred_element_type=F32)
        y += lax.dot_general(hr_ref[...], wt_scr[cols, :STATE_COLS], contract_last,
                             preferred_element_type=F32)
        y += lax.dot_general(hi_ref[...], wt_scr[cols, STATE_COLS:], contract_last,
                             preferred_element_type=F32)
        y_ref[:, cols] = y


def _ssm_out(u2, h_re, h_im, bd, w_re, w_im):
    n = u2.shape[0]
    state = pl.BlockSpec((SSM_TILE, STATE_COLS), lambda gb, i: (i, gb))
    slab = pl.BlockSpec((SSM_TILE, SLAB), lambda gb, i: (i, gb))
    compact = pl.BlockSpec((1, SLAB, 2 * STATE), lambda gb, i: (gb, 0, 0))
    return pl.pallas_call(
        _ssm_out_kernel,
        out_shape=jax.ShapeDtypeStruct(u2.shape, F32),
        grid=(N_GROUP_BLOCKS, n // SSM_TILE),
        in_specs=[slab, state, state,
                  pl.BlockSpec((1, LANES, SLAB), lambda gb, i: (gb, 0, 0)), compact, compact],
        out_specs=slab,
        scratch_shapes=[pltpu.VMEM((SLAB, SLAB), BF16), pltpu.VMEM((SLAB, 2 * STATE_COLS), BF16)],
        compiler_params=pltpu.CompilerParams(dimension_semantics=("arbitrary", "arbitrary"),
                                             vmem_limit_bytes=VMEM_LIMIT),
        name="ssm_out",
    )(u2, h_re, h_im, bd, w_re, w_im)


def _s5(u2, prep, bsz):
    bd, v_re, v_im, w_re, w_im = prep[:5]
    a16 = prep[5].reshape(N_SSM_GROUPS, SSM_GROUP, 2 * STATE)[:, 0]
    n = u2.shape[0]
    kc = n // bsz
    s_re, s_im = _ssm_state(u2, v_re, v_im)
    tile_shape = (N_SSM_GROUPS * STATE // LANES, LANES)
    a_re = a16[:, :STATE].reshape(tile_shape)
    a_im = a16[:, STATE:].reshape(tile_shape)
    h_re, h_im = _ssm_scan(s_re.reshape(bsz, kc, -1), s_im.reshape(bsz, kc, -1), a_re, a_im)
    return _ssm_out(u2, h_re.reshape(n, -1), h_im.reshape(n, -1), bd, w_re, w_im)


def _swap_lane_halves(x):
    packed = pltpu.bitcast(x, jnp.uint32)
    return pltpu.bitcast(pltpu.roll(packed, HEAD_DIM, axis=1), BF16)


def _attn_out_kernel(sink_ref, q_ref, za_ref, kv_ref, kvp_ref, y2_ref, zs_ref,
                     x_ref, gate_ref, ag_ref, gw_ref, gb_ref, sg_ref, wo_ref, fg_ref,
                     o_ref, kv_scr, attn_scr, y_scr, mix_scr, out_scr,
                     *, n_tiles, tiles_per_batch):
    step = pl.program_id(0)
    cur_slot = step % 2
    prev_slot = 1 - cur_slot

    @pl.when(step == 0)
    def _():
        mix_scr[1] = jnp.zeros(mix_scr.shape[1:], BF16)

    first = jnp.minimum(step, n_tiles - 1) % tiles_per_batch == 0
    keep = jnp.where(first, 0.0, 1.0).astype(BF16)
    kv_scr[0:BLOCK, :] = kvp_ref[...] * keep
    kv_scr[BLOCK:, :] = kv_ref[...]

    older = (lax.broadcasted_iota(jnp.int32, (BLOCK, BLOCK), 1)
             > lax.broadcasted_iota(jnp.int32, (BLOCK, BLOCK), 0))
    kv_half = lax.broadcasted_iota(jnp.int32, (2 * BLOCK, LANES), 1) // HEAD_DIM
    pairs = Q_PER_KV // 2

    def both_halves(r0, col):
        tile, half = col // LANES, (col % LANES) // HEAD_DIM
        band = kv_scr[pl.ds(r0, 2 * BLOCK), tile * LANES:(tile + 1) * LANES]
        own = jnp.where(kv_half == half, band, jnp.zeros_like(band))
        other = _swap_lane_halves(own)
        return (own, other) if half == 0 else (other, own)

    contract_last = (((1,), (1,)), ((), ()))
    for gb in range(N_GROUP_BLOCKS):
        for tok in range(SSM_CHUNK):
            c0 = gb * SLAB + tok * LANES
            y_scr[gb, pl.ds(tok, CHUNK_ROWS, stride=SSM_CHUNK), :] = y2_ref[:, c0:c0 + LANES]

    def block(j, carry):
        r0 = j * BLOCK
        rows = pl.ds(r0, BLOCK)
        out_scr[j] = jnp.dot(mix_scr[prev_slot], wo_ref[:, j * OUT_CHUNK:(j + 1) * OUT_CHUNK],
                             preferred_element_type=F32)
        no_past = jnp.where(first, NEG, 0.0) if j == 0 else None
        for h in range(N_KV_HEADS):
            k_par = both_halves(r0, h * HEAD_DIM)
            v_par = both_halves(r0, KV_WIDTH + h * HEAD_DIM)
            q0 = h * Q_PER_KV * HEAD_DIM
            qs = jnp.concatenate(
                [q_ref[pl.ds(r0, BLOCK), q0 + p * PAIR:q0 + (p + 1) * PAIR] for p in range(pairs)],
                axis=0)
            acc = [None] * pairs
            for par in range(2):
                s_all = lax.dot_general(qs, k_par[par], contract_last, preferred_element_type=F32)
                probs, inv = [], []
                for p in range(pairs):
                    past = s_all[p * BLOCK:(p + 1) * BLOCK, :BLOCK]
                    if no_past is not None:
                        past = past + no_past
                    s = jnp.where(older, past, s_all[p * BLOCK:(p + 1) * BLOCK, BLOCK:])
                    sink = sink_ref[h * Q_PER_KV + 2 * p + par] * LOG2_E
                    m = jnp.max(s, axis=-1, keepdims=True)
                    e = jnp.exp2(s - m)
                    l = jnp.sum(e, axis=-1, keepdims=True) + jnp.exp2(sink - m)
                    inv.append(1.0 / l)
                    e = e.astype(BF16)
                    zero = jnp.zeros_like(e)
                    probs.append(jnp.concatenate(
                        [jnp.where(older, e, zero), jnp.where(older, zero, e)], axis=1))
                pv = jnp.dot(jnp.concatenate(probs, axis=0), v_par[par], preferred_element_type=F32)
                for p in range(pairs):
                    o = pv[p * BLOCK:(p + 1) * BLOCK] * inv[p]
                    acc[p] = o if par == 0 else acc[p] + o
            for p in range(pairs):
                attn_scr[:, q0 + p * PAIR:q0 + (p + 1) * PAIR] = acc[p]
        a = attn_scr[...]
        a = a * lax.rsqrt(jnp.mean(a * a, axis=-1, keepdims=True) + EPS) * ag_ref[...]
        za = za_ref[rows, :]
        mix_scr[cur_slot, rows, :ATTN_WIDTH] = a.astype(BF16) * (za * jax.nn.sigmoid(za))
        y = jnp.concatenate([y_scr[gb, rows, :] for gb in range(N_GROUP_BLOCKS)], axis=1)
        y = 0.5 * y * (1.0 + lax.erf(y * math.sqrt(0.5)))
        glu = jnp.dot(y.astype(BF16), gw_ref[...], preferred_element_type=F32) + gb_ref[...]
        y = y * jax.nn.sigmoid(glu)
        y = y * lax.rsqrt(jnp.mean(y * y, axis=-1, keepdims=True) + EPS) * sg_ref[...]
        zs = zs_ref[rows, :]
        mix_scr[cur_slot, rows, ATTN_WIDTH:] = y.astype(BF16) * (zs * jax.nn.sigmoid(zs))
        return carry

    for j in range(q_ref.shape[0] // BLOCK):
        block(j, 0)

    out = jnp.concatenate([out_scr[c] for c in range(out_scr.shape[0])], axis=1)
    res = x_ref[...] + gate_ref[0] * out
    o_ref[...] = res * lax.rsqrt(jnp.mean(res * res, axis=-1, keepdims=True) + EPS) * fg_ref[...]


def _attn_out(sinks, q, za, kv, y2, zs, x2, gate, attn_gain, glu_w, glu_b, ssm_gain, w_out,
              final_gain, rows_per_batch):
    t, d = x2.shape
    n_tiles = t // ROW_TILE
    tiles_per_batch = rows_per_batch // ROW_TILE
    blocks_per_tile = ROW_TILE // BLOCK
    cur = lambda i: jnp.minimum(i, n_tiles - 1)
    old = lambda i: jnp.maximum(i - 1, 0)
    row = lambda w: pl.BlockSpec((ROW_TILE, w), lambda i: (cur(i), 0))
    prev_block = pl.BlockSpec(
        (BLOCK, 2 * KV_WIDTH), lambda i: (jnp.maximum(cur(i) * blocks_per_tile - 1, 0), 0))
    out_chunks = d // OUT_CHUNK
    return pl.pallas_call(
        functools.partial(_attn_out_kernel, n_tiles=n_tiles, tiles_per_batch=tiles_per_batch),
        out_shape=jax.ShapeDtypeStruct((t, d), F32),
        grid=(n_tiles + 1,),
        in_specs=[pl.BlockSpec(memory_space=pltpu.SMEM),
                  row(ATTN_WIDTH), row(ATTN_WIDTH), row(2 * KV_WIDTH), prev_block,
                  pl.BlockSpec((CHUNK_ROWS, y2.shape[1]), lambda i: (cur(i), 0)),
                  row(SSM_WIDTH),
                  pl.BlockSpec((ROW_TILE, d), lambda i: (old(i), 0)),
                  pl.BlockSpec((1, 1, d), lambda i: (old(i) // tiles_per_batch, 0, 0)),
                  _resident((1, ATTN_WIDTH)),
                  _resident((SSM_WIDTH, SSM_WIDTH)), _resident((1, SSM_WIDTH)),
                  _resident((1, SSM_WIDTH)),
                  _resident((d, d)), _resident((1, d))],
        out_specs=pl.BlockSpec((ROW_TILE, d), lambda i: (old(i), 0)),
        scratch_shapes=[pltpu.VMEM((ROW_TILE + BLOCK, 2 * KV_WIDTH), BF16),
                        pltpu.VMEM((BLOCK, ATTN_WIDTH), F32),
                        pltpu.VMEM((N_GROUP_BLOCKS, ROW_TILE, LANES), F32),
                        pltpu.VMEM((2, ROW_TILE, d), BF16),
                        pltpu.VMEM((out_chunks, ROW_TILE, OUT_CHUNK), F32)],
        compiler_params=pltpu.CompilerParams(dimension_semantics=("arbitrary",),
                                             vmem_limit_bytes=VMEM_LIMIT),
        name="attn_out",
    )(sinks, q, za, kv, kv, y2, zs, x2, gate, attn_gain, glu_w, glu_b, ssm_gain, w_out, final_gain)


def kernel(x, c, w_ada, b_ada, norm_gain, w_in, b_in, attn_sinks, attn_out_gain, ssm_lambda_re,
           ssm_lambda_im, ssm_log_step, ssm_b_re, ssm_b_im, ssm_c_re, ssm_c_im, ssm_d, glu_w, glu_b,
           ssm_out_gain, w_out, final_gain):
    bsz, seq, d = x.shape
    assert w_ada.shape[0] == 1, "single-layer trunk only"
    assert d == D_MODEL and w_in.shape[2] == 2 * (ATTN_WIDTH + KV_WIDTH + SSM_WIDTH)
    assert seq % ROW_TILE == 0 and (bsz * seq // SSM_CHUNK) % SSM_TILE == 0
    assert (seq // SSM_CHUNK) % SCAN_TILE == 0
    x2 = x.reshape(bsz * seq, d)
    mod, w_in16 = _ada(c, w_ada[0], b_ada[0], w_in[0])
    shift, scale, gate = (m.reshape(bsz, 1, d) for m in jnp.split(mod, 3, axis=-1))
    q, kv, za, zs, u2, w_out16 = _inproj(x2, scale, shift, norm_gain[0].reshape(1, d),
                                         w_in16, b_in[0].reshape(1, -1), w_out[0], seq)
    prep = _ssm_prep(ssm_lambda_re[0], ssm_lambda_im[0], ssm_log_step[0], ssm_b_re[0],
                     ssm_b_im[0], ssm_c_re[0], ssm_c_im[0], ssm_d[0])
    y2 = _s5(u2, prep, bsz)
    out = _attn_out(attn_sinks[0], q, za, kv, y2, zs, x2, gate,
                    attn_out_gain[0].reshape(1, -1), glu_w[0].astype(BF16),
                    glu_b[0].reshape(1, -1), ssm_out_gain[0].reshape(1, -1),
                    w_out16, final_gain.reshape(1, d), seq)
    return out.reshape(bsz, seq, d)
```

```python
import functools
import math

import jax
import jax.numpy as jnp
from jax import lax
from jax.experimental import pallas as pl
from jax.experimental.pallas import tpu as pltpu

F32 = jnp.float32
BF16 = jnp.bfloat16

D_MODEL = 2048
HEAD_DIM = 64
N_Q_HEADS = 24
N_KV_HEADS = 3
Q_PER_KV = 8
ATTN_WIDTH = N_Q_HEADS * HEAD_DIM
BLOCK = 128
SSM_GROUP = 16
SSM_WIDTH = D_MODEL - ATTN_WIDTH
N_SSM_GROUPS = SSM_WIDTH // SSM_GROUP
STATE = 64
EPS = 1e-5
NEG = -1e30
LOG2_E = math.log2(math.e)

LANES = 128
SSM_CHUNK = 16
GROUPS_PER_BLOCK = LANES // SSM_GROUP
N_GROUP_BLOCKS = N_SSM_GROUPS // GROUPS_PER_BLOCK
SLAB = SSM_CHUNK * LANES
STATE_COLS = GROUPS_PER_BLOCK * STATE
KV_WIDTH = N_KV_HEADS * HEAD_DIM
PAIR = 2 * HEAD_DIM
Q_SCALE = LOG2_E * HEAD_DIM ** -0.5
PROJ_COLS = 512

VMEM_LIMIT = 56 * 1024 * 1024

ROW_TILE = 512
CHUNK_ROWS = ROW_TILE // SSM_CHUNK
ADA_COLS = 768
OUT_CHUNK = D_MODEL // (ROW_TILE // BLOCK)
SSM_TILE = 1024
SCAN_TILE = 128
SCAN_PITCH = 24


def _resident(shape):
    return pl.BlockSpec(shape, lambda *_: (0,) * len(shape), pipeline_mode=pl.Buffered(1))


def _ada_kernel(ct_ref, w_ref, b_ref, win_ref, o_ref, win16_ref):
    win16_ref[...] = win_ref[...].astype(BF16)
    ct = ct_ref[...]
    s = ct * jax.nn.sigmoid(ct)
    w = w_ref[...]
    rows = [jnp.sum(s[:, b:b + 1] * w, axis=0, keepdims=True) for b in range(ct.shape[1])]
    o_ref[...] = jnp.concatenate(rows, axis=0) + b_ref[...]


def _ada(c, w_ada, b_ada, w_in):
    bsz, d = c.shape
    n = w_ada.shape[1]
    steps = n // ADA_COLS
    slab = pl.BlockSpec((w_in.shape[0] // steps, w_in.shape[1]), lambda i: (i, 0))
    return pl.pallas_call(
        _ada_kernel,
        out_shape=[jax.ShapeDtypeStruct((bsz, n), F32), jax.ShapeDtypeStruct(w_in.shape, BF16)],
        grid=(steps,),
        in_specs=[pl.BlockSpec((d, bsz), lambda i: (0, 0)),
                  pl.BlockSpec((d, ADA_COLS), lambda i: (0, i)),
                  pl.BlockSpec((1, ADA_COLS), lambda i: (0, i)),
                  slab],
        out_specs=[pl.BlockSpec((bsz, ADA_COLS), lambda i: (0, i)), slab],
        compiler_params=pltpu.CompilerParams(dimension_semantics=("arbitrary",),
                                             vmem_limit_bytes=VMEM_LIMIT),
        name="ada",
    )(c.T, w_ada, b_ada.reshape(1, n), w_in)


def _inproj_kernel(x_ref, sc_ref, sh_ref, g_ref, w_ref, b_ref, wo_ref,
                   q_ref, kv_ref, za_ref, zs_ref, u2_ref, wo16_ref, h_scr, u_scr):
    wo16_ref[...] = wo_ref[...].astype(BF16)
    x = x_ref[...]
    var = jnp.mean(x * x, axis=-1, keepdims=True)
    gain = g_ref[...] * (1.0 + sc_ref[0])
    h_scr[...] = (x * lax.rsqrt(var + EPS) * gain + sh_ref[0]).astype(BF16)

    def project(col, width):
        acc = jnp.dot(h_scr[...], w_ref[:, col:col + width], preferred_element_type=F32)
        return acc + b_ref[:, col:col + width]

    col_kv, col_za = ATTN_WIDTH, ATTN_WIDTH + 2 * KV_WIDTH
    col_u = col_za + ATTN_WIDTH
    col_zs = col_u + SSM_WIDTH
    for c0 in range(0, ATTN_WIDTH, PROJ_COLS):
        q_ref[:, c0:c0 + PROJ_COLS] = (project(c0, PROJ_COLS) * Q_SCALE).astype(BF16)
        za_ref[:, c0:c0 + PROJ_COLS] = project(col_za + c0, PROJ_COLS).astype(BF16)
    kv_ref[...] = project(col_kv, 2 * KV_WIDTH).astype(BF16)
    zs_ref[...] = project(col_zs, SSM_WIDTH).astype(BF16)
    u = project(col_u, SSM_WIDTH)
    for gb in range(N_GROUP_BLOCKS):
        u_scr[gb] = u[:, gb * LANES:(gb + 1) * LANES]
        for tok in range(SSM_CHUNK):
            c0 = gb * SLAB + tok * LANES
            u2_ref[:, c0:c0 + LANES] = (
                u_scr[gb, pl.ds(tok, CHUNK_ROWS, stride=SSM_CHUNK), :].astype(BF16))


def _inproj(x2, scale, shift, gain, w_cat, b_cat, w_out, rows_per_batch):
    t, d = x2.shape
    widths = (ATTN_WIDTH, 2 * KV_WIDTH, ATTN_WIDTH, SSM_WIDTH)
    n_all = w_cat.shape[1]
    n_tiles = t // ROW_TILE
    tiles_per_batch = rows_per_batch // ROW_TILE
    mod_spec = pl.BlockSpec((1, 1, d), lambda i: (i // tiles_per_batch, 0, 0))
    u2_cols = N_GROUP_BLOCKS * SLAB
    wo_spec = pl.BlockSpec((w_out.shape[0] // n_tiles, w_out.shape[1]), lambda i: (i, 0))
    return pl.pallas_call(
        _inproj_kernel,
        out_shape=[jax.ShapeDtypeStruct((t, w), BF16) for w in widths]
        + [jax.ShapeDtypeStruct((t // SSM_CHUNK, u2_cols), BF16),
           jax.ShapeDtypeStruct(w_out.shape, BF16)],
        grid=(n_tiles,),
        in_specs=[pl.BlockSpec((ROW_TILE, d), lambda i: (i, 0)),
                  mod_spec, mod_spec,
                  _resident((1, d)),
                  _resident((d, n_all)),
                  _resident((1, n_all)),
                  wo_spec],
        out_specs=[pl.BlockSpec((ROW_TILE, w), lambda i: (i, 0)) for w in widths]
        + [pl.BlockSpec((CHUNK_ROWS, u2_cols), lambda i: (i, 0)), wo_spec],
        scratch_shapes=[pltpu.VMEM((ROW_TILE, d), BF16),
                        pltpu.VMEM((N_GROUP_BLOCKS, ROW_TILE, LANES), F32)],
        compiler_params=pltpu.CompilerParams(dimension_semantics=("arbitrary",),
                                             vmem_limit_bytes=VMEM_LIMIT),
        name="inproj",
    )(x2, scale, shift, gain, w_cat, b_cat, w_out)


def _ssm_prep_kernel(lr_ref, li_ref, ls_ref, br_ref, bi_ref, cr_ref, ci_ref, d_ref,
                     bd_ref, vr_ref, vi_ref, wr_ref, wi_ref, a16_ref, y_scr):
    lr, li = lr_ref[...], li_ref[...]
    step = jnp.exp(ls_ref[...])
    decay = jnp.exp(lr * step)
    ar, ai = decay * jnp.cos(li * step), decay * jnp.sin(li * step)
    den = lr * lr + li * li
    nr, ni = ar - 1.0, ai
    coef_re = (nr * lr + ni * li) / den
    coef_im = (ni * lr - nr * li) / den
    br, bi = br_ref[...], bi_ref[...]
    bbar_re = coef_re * br - coef_im * bi
    bbar_im = coef_re * bi + coef_im * br
    cr, ci = cr_ref[...], ci_ref[...]
    lo = lax.broadcasted_iota(jnp.int32, lr.shape, 2) < STATE
    pr, pi = jnp.ones_like(ar), jnp.zeros_like(ai)
    for tau in range(SSM_CHUNK + 1):
        er, ei = pr * cr - pi * ci, pr * ci + pi * cr
        if tau < SSM_CHUNK:
            y_scr[:, tau * LANES:(tau + 1) * LANES, :] = jnp.where(lo, er, ei)
            s = SSM_CHUNK - 1 - tau
            vr_ref[:, s * LANES:(s + 1) * LANES, :] = pr * bbar_re - pi * bbar_im
            vi_ref[:, s * LANES:(s + 1) * LANES, :] = pr * bbar_im + pi * bbar_re
        if tau >= 1:
            wr_ref[:, (tau - 1) * LANES:tau * LANES, :] = er
            wi_ref[:, (tau - 1) * LANES:tau * LANES, :] = -ei
        if tau == SSM_CHUNK:
            a16_ref[...] = jnp.where(lo, pr, pi)
        pr, pi = pr * ar - pi * ai, pr * ai + pi * ar
    x = jnp.where(lo, bbar_re, -bbar_im)
    row = lax.broadcasted_iota(jnp.int32, (LANES, SLAB), 0)
    col = lax.broadcasted_iota(jnp.int32, (LANES, SLAB), 1)
    same_group = row // SSM_GROUP == (col % LANES) // SSM_GROUP
    diagonal = (lax.broadcasted_iota(jnp.int32, (LANES, LANES), 0)
                == lax.broadcasted_iota(jnp.int32, (LANES, LANES), 1))
    for gb in range(N_GROUP_BLOCKS):
        k = lax.dot_general(x[gb], y_scr[gb], (((1,), (1,)), ((), ())),
                            precision=lax.Precision.HIGHEST, preferred_element_type=F32)
        k = jnp.where(same_group, k, 0.0)
        bd_ref[gb, :, LANES:] = k[:, LANES:]
        bd_ref[gb, :, :LANES] = k[:, :LANES] + jnp.where(diagonal, d_ref[gb], 0.0)


def _ssm_prep(lam_re, lam_im, log_step, b_re, b_im, c_re, c_im, d_skip):
    nb = N_GROUP_BLOCKS
    rows = lambda a: jnp.concatenate([a, a], axis=-1).reshape(nb, LANES, 2 * STATE)
    per_group = lambda a: rows(jnp.repeat(a[:, None, :], SSM_GROUP, axis=1))
    compact = jax.ShapeDtypeStruct((nb, SLAB, 2 * STATE), F32)
    return pl.pallas_call(
        _ssm_prep_kernel,
        out_shape=[jax.ShapeDtypeStruct((nb, LANES, SLAB), F32),
                   compact, compact,
                   compact, compact,
                   jax.ShapeDtypeStruct((nb, LANES, 2 * STATE), F32)],
        scratch_shapes=[pltpu.VMEM((nb, SLAB, 2 * STATE), F32)],
        compiler_params=pltpu.CompilerParams(vmem_limit_bytes=VMEM_LIMIT),
        name="ssm_prep",
    )(per_group(lam_re), per_group(lam_im),
      jnp.broadcast_to(jnp.repeat(log_step, SSM_GROUP).reshape(nb, LANES, 1), (nb, LANES, 2 * STATE)),
      rows(jnp.swapaxes(b_re, 1, 2)), rows(jnp.swapaxes(b_im, 1, 2)), rows(c_re), rows(c_im),
      jnp.broadcast_to(d_skip.reshape(nb, 1, LANES), (nb, LANES, LANES)))


def _expand_block_diagonal(re_ref, im_ref, dst):
    shape = (SLAB, 2 * STATE)
    row_group = (lax.broadcasted_iota(jnp.int32, shape, 0) % LANES) // SSM_GROUP
    half = lax.broadcasted_iota(jnp.int32, shape, 1) // STATE
    for j in range(STATE_COLS // LANES):
        own = row_group == 2 * j + half
        for src, c0 in ((re_ref, 0), (im_ref, STATE_COLS)):
            dst[:, c0 + j * LANES:c0 + (j + 1) * LANES] = jnp.where(own, src[0], 0.0).astype(BF16)


def _ssm_state_kernel(u_ref, vr_ref, vi_ref, sr_ref, si_ref, v_scr):
    @pl.when(pl.program_id(1) == 0)
    def _():
        _expand_block_diagonal(vr_ref, vi_ref, v_scr)

    s = jnp.dot(u_ref[...], v_scr[...], preferred_element_type=F32)
    sr_ref[...] = s[:, :STATE_COLS]
    si_ref[...] = s[:, STATE_COLS:]


def _ssm_state(u2, v_re, v_im):
    n = u2.shape[0]
    state = pl.BlockSpec((SSM_TILE, STATE_COLS), lambda gb, i: (i, gb))
    compact = pl.BlockSpec((1, SLAB, 2 * STATE), lambda gb, i: (gb, 0, 0))
    return pl.pallas_call(
        _ssm_state_kernel,
        out_shape=[jax.ShapeDtypeStruct((n, N_GROUP_BLOCKS * STATE_COLS), F32)] * 2,
        grid=(N_GROUP_BLOCKS, n // SSM_TILE),
        in_specs=[pl.BlockSpec((SSM_TILE, SLAB), lambda gb, i: (i, gb)), compact, compact],
        out_specs=[state, state],
        scratch_shapes=[pltpu.VMEM((SLAB, 2 * STATE_COLS), BF16)],
        compiler_params=pltpu.CompilerParams(dimension_semantics=("arbitrary", "arbitrary"),
                                             vmem_limit_bytes=VMEM_LIMIT),
        name="ssm_state",
    )(u2, v_re, v_im)


def _ssm_scan_kernel(sr_ref, si_ref, ar_ref, ai_ref, hr_ref, hi_ref, cr_scr, ci_scr, s_scr, h_scr):
    @pl.when(pl.program_id(0) == 0)
    def _():
        cr_scr[...] = jnp.zeros_like(cr_scr)
        ci_scr[...] = jnp.zeros_like(ci_scr)

    ar, ai = ar_ref[...], ai_ref[...]
    nb = sr_ref.shape[0]
    tiles = ar.shape[0]
    for b in range(nb):
        for part, ref in enumerate((sr_ref, si_ref)):
            for r in range(tiles):
                s_scr[part, b, pl.ds(r, SCAN_TILE, stride=SCAN_PITCH), :] = (
                    ref[b, :, r * LANES:(r + 1) * LANES])

    def body(k, carry):
        rows = pl.ds(pl.multiple_of(k * SCAN_PITCH, 8), tiles)
        out = []
        for b in range(nb):
            hr, hi = carry[2 * b], carry[2 * b + 1]
            h_scr[0, b, rows, :] = hr
            h_scr[1, b, rows, :] = hi
            out += [ar * hr - ai * hi + s_scr[0, b, rows, :], ar * hi + ai * hr + s_scr[1, b, rows, :]]
        return tuple(out)

    init = tuple(scr[b] for b in range(nb) for scr in (cr_scr, ci_scr))
    fin = lax.fori_loop(0, SCAN_TILE, body, init, unroll=8)
    for b in range(nb):
        cr_scr[b] = fin[2 * b]
        ci_scr[b] = fin[2 * b + 1]
        for part, ref in enumerate((hr_ref, hi_ref)):
            for r in range(tiles):
                ref[b, :, r * LANES:(r + 1) * LANES] = (
                    h_scr[part, b, pl.ds(r, SCAN_TILE, stride=SCAN_PITCH), :].astype(BF16))


def _ssm_scan(s_re, s_im, a_re, a_im):
    bsz, k, cols = s_re.shape
    rows, lanes = a_re.shape
    blk = pl.BlockSpec((bsz, SCAN_TILE, cols), lambda i: (0, i, 0))
    return pl.pallas_call(
        _ssm_scan_kernel,
        out_shape=[jax.ShapeDtypeStruct(s_re.shape, BF16)] * 2,
        grid=(k // SCAN_TILE,),
        in_specs=[blk, blk, _resident((rows, lanes)), _resident((rows, lanes))],
        out_specs=[blk, blk],
        scratch_shapes=[pltpu.VMEM((bsz, rows, lanes), F32)] * 2
        + [pltpu.VMEM((2, bsz, SCAN_TILE * SCAN_PITCH, lanes), F32)] * 2,
        compiler_params=pltpu.CompilerParams(dimension_semantics=("arbitrary",)),
        name="ssm_scan",
    )(s_re, s_im, a_re, a_im)


def _ssm_out_kernel(u_ref, hr_ref, hi_ref, bd_ref, wr_ref, wi_ref, y_ref, t_scr, wt_scr):
    @pl.when(pl.program_id(1) == 0)
    def _():
        bd = bd_ref[0].astype(BF16)
        for s in range(SSM_CHUNK):
            rows = slice(s * LANES, (s + 1) * LANES)
            if s:
                t_scr[rows, :s * LANES] = jnp.zeros((LANES, s * LANES), BF16)
            t_scr[rows, s * LANES:] = bd[:, :(SSM_CHUNK - s) * LANES]
        _expand_block_diagonal(wr_ref, wi_ref, wt_scr)

    pair = 2 * LANES
    contract_last = (((1,), (1,)), ((), ()))
    for tb in range(SLAB // pair):
        k_rows = (tb + 1) * pair
        cols = slice(tb * pair, (tb + 1) * pair)
        y = jnp.dot(u_ref[:, :k_rows], t_scr[:k_rows, cols], preferred_element_type=F32)
        y += lax.dot_general(hr_ref[...], wt_scr[cols, :STATE_COLS], contract_last,
                             preferred_element_type=F32)
        y += lax.dot_general(hi_ref[...], wt_scr[cols, STATE_COLS:], contract_last,
                             preferred_element_type=F32)
        y_ref[:, cols] = y


def _ssm_out(u2, h_re, h_im, bd, w_re, w_im):
    n = u2.shape[0]
    state = pl.BlockSpec((SSM_TILE, STATE_COLS), lambda gb, i: (i, gb))
    slab = pl.BlockSpec((SSM_TILE, SLAB), lambda gb, i: (i, gb))
    compact = pl.BlockSpec((1, SLAB, 2 * STATE), lambda gb, i: (gb, 0, 0))
    return pl.pallas_call(
        _ssm_out_kernel,
        out_shape=jax.ShapeDtypeStruct(u2.shape, F32),
        grid=(N_GROUP_BLOCKS, n // SSM_TILE),
        in_specs=[slab, state, state,
                  pl.BlockSpec((1, LANES, SLAB), lambda gb, i: (gb, 0, 0)), compact, compact],
        out_specs=slab,
        scratch_shapes=[pltpu.VMEM((SLAB, SLAB), BF16), pltpu.VMEM((SLAB, 2 * STATE_COLS), BF16)],
        compiler_params=pltpu.CompilerParams(dimension_semantics=("arbitrary", "arbitrary"),
                                             vmem_limit_bytes=VMEM_LIMIT),
        name="ssm_out",
    )(u2, h_re, h_im, bd, w_re, w_im)


def _s5(u2, prep, bsz):
    bd, v_re, v_im, w_re, w_im = prep[:5]
    a16 = prep[5].reshape(N_SSM_GROUPS, SSM_GROUP, 2 * STATE)[:, 0]
    n = u2.shape[0]
    kc = n // bsz
    s_re, s_im = _ssm_state(u2, v_re, v_im)
    tile_shape = (N_SSM_GROUPS * STATE // LANES, LANES)
    a_re = a16[:, :STATE].reshape(tile_shape)
    a_im = a16[:, STATE:].reshape(tile_shape)
    h_re, h_im = _ssm_scan(s_re.reshape(bsz, kc, -1), s_im.reshape(bsz, kc, -1), a_re, a_im)
    return _ssm_out(u2, h_re.reshape(n, -1), h_im.reshape(n, -1), bd, w_re, w_im)


def _swap_lane_halves(x):
    packed = pltpu.bitcast(x, jnp.uint32)
    return pltpu.bitcast(pltpu.roll(packed, HEAD_DIM, axis=1), BF16)


def _attn_out_kernel(*refs, n_tiles, tiles_per_batch):
    step = pl.program_id(0)
    body = functools.partial(_attn_out_step, *refs, tiles_per_batch=tiles_per_batch)
    pl.when(step == 0)(functools.partial(body, do_attn=True, do_proj=False))
    pl.when((step > 0) & (step < n_tiles))(functools.partial(body, do_attn=True, do_proj=True))
    pl.when(step == n_tiles)(functools.partial(body, do_attn=False, do_proj=True))


def _attn_out_step(sink_ref, q_ref, za_ref, kv_ref, kvp_ref, y2_ref, zs_ref,
                   x_ref, gate_ref, ag_ref, gw_ref, gb_ref, sg_ref, wo_ref, fg_ref,
                   o_ref, kv_scr, attn_scr, y_scr, mix_scr, out_scr,
                   *, tiles_per_batch, do_attn, do_proj):
    step = pl.program_id(0)
    cur_slot = step % 2
    prev_slot = 1 - cur_slot
    first = step % tiles_per_batch == 0
    if do_attn:
        keep = jnp.where(first, 0.0, 1.0).astype(BF16)
        kv_scr[0:BLOCK, :] = kvp_ref[...] * keep
        kv_scr[BLOCK:, :] = kv_ref[...]

    older = (lax.broadcasted_iota(jnp.int32, (BLOCK, BLOCK), 1)
             > lax.broadcasted_iota(jnp.int32, (BLOCK, BLOCK), 0))
    kv_half = lax.broadcasted_iota(jnp.int32, (2 * BLOCK, LANES), 1) // HEAD_DIM
    pairs = Q_PER_KV // 2

    def both_halves(r0, col):
        tile, half = col // LANES, (col % LANES) // HEAD_DIM
        band = kv_scr[pl.ds(r0, 2 * BLOCK), tile * LANES:(tile + 1) * LANES]
        own = jnp.where(kv_half == half, band, jnp.zeros_like(band))
        other = _swap_lane_halves(own)
        return (own, other) if half == 0 else (other, own)

    contract_last = (((1,), (1,)), ((), ()))
    for gb in range(N_GROUP_BLOCKS if do_attn else 0):
        for tok in range(SSM_CHUNK):
            c0 = gb * SLAB + tok * LANES
            y_scr[gb, pl.ds(tok, CHUNK_ROWS, stride=SSM_CHUNK), :] = y2_ref[:, c0:c0 + LANES]

    def block(j, carry):
        r0 = j * BLOCK
        rows = pl.ds(r0, BLOCK)
        if do_proj:
            out_scr[j] = jnp.dot(mix_scr[prev_slot], wo_ref[:, j * OUT_CHUNK:(j + 1) * OUT_CHUNK],
                                 preferred_element_type=F32)
        if not do_attn:
            return carry
        no_past = jnp.where(first, NEG, 0.0) if j == 0 else None
        for h in range(N_KV_HEADS):
            k_par = both_halves(r0, h * HEAD_DIM)
            v_par = both_halves(r0, KV_WIDTH + h * HEAD_DIM)
            q0 = h * Q_PER_KV * HEAD_DIM
            qs = jnp.concatenate(
                [q_ref[pl.ds(r0, BLOCK), q0 + p * PAIR:q0 + (p + 1) * PAIR] for p in range(pairs)],
                axis=0)
            acc = [None] * pairs
            for par in range(2):
                s_all = lax.dot_general(qs, k_par[par], contract_last, preferred_element_type=F32)
                probs, inv = [], []
                for p in range(pairs):
                    past = s_all[p * BLOCK:(p + 1) * BLOCK, :BLOCK]
                    if no_past is not None:
                        past = past + no_past
                    s = jnp.where(older, past, s_all[p * BLOCK:(p + 1) * BLOCK, BLOCK:])
                    sink = sink_ref[h * Q_PER_KV + 2 * p + par] * LOG2_E
                    m = jnp.max(s, axis=-1, keepdims=True)
                    e = jnp.exp2(s - m)
                    l = jnp.sum(e, axis=-1, keepdims=True) + jnp.exp2(sink - m)
                    inv.append(1.0 / l)
                    e = e.astype(BF16)
                    zero = jnp.zeros_like(e)
                    probs.append(jnp.concatenate(
                        [jnp.where(older, e, zero), jnp.where(older, zero, e)], axis=1))
                pv = jnp.dot(jnp.concatenate(probs, axis=0), v_par[par], preferred_element_type=F32)
                for p in range(pairs):
                    o = pv[p * BLOCK:(p + 1) * BLOCK] * inv[p]
                    acc[p] = o if par == 0 else acc[p] + o
            for p in range(pairs):
                attn_scr[:, q0 + p * PAIR:q0 + (p + 1) * PAIR] = acc[p]
        a = attn_scr[...]
        a = a * lax.rsqrt(jnp.mean(a * a, axis=-1, keepdims=True) + EPS) * ag_ref[...]
        za = za_ref[rows, :]
        mix_scr[cur_slot, rows, :ATTN_WIDTH] = a.astype(BF16) * (za * jax.nn.sigmoid(za))
        y = jnp.concatenate([y_scr[gb, rows, :] for gb in range(N_GROUP_BLOCKS)], axis=1)
        y = 0.5 * y * (1.0 + lax.erf(y * math.sqrt(0.5)))
        glu = jnp.dot(y.astype(BF16), gw_ref[...], preferred_element_type=F32) + gb_ref[...]
        y = y * jax.nn.sigmoid(glu)
        y = y * lax.rsqrt(jnp.mean(y * y, axis=-1, keepdims=True) + EPS) * sg_ref[...]
        zs = zs_ref[rows, :]
        mix_scr[cur_slot, rows, ATTN_WIDTH:] = y.astype(BF16) * (zs * jax.nn.sigmoid(zs))
        return carry

    for j in range(q_ref.shape[0] // BLOCK):
        block(j, 0)

    if do_proj:
        out = jnp.concatenate([out_scr[c] for c in range(out_scr.shape[0])], axis=1)
        res = x_ref[...] + gate_ref[0] * out
        o_ref[...] = (res * lax.rsqrt(jnp.mean(res * res, axis=-1, keepdims=True) + EPS)
                      * fg_ref[...])


def _attn_out(sinks, q, za, kv, y2, zs, x2, gate, attn_gain, glu_w, glu_b, ssm_gain, w_out,
              final_gain, rows_per_batch):
    t, d = x2.shape
    n_tiles = t // ROW_TILE
    tiles_per_batch = rows_per_batch // ROW_TILE
    blocks_per_tile = ROW_TILE // BLOCK
    cur = lambda i: jnp.minimum(i, n_tiles - 1)
    old = lambda i: jnp.maximum(i - 1, 0)
    row = lambda w: pl.BlockSpec((ROW_TILE, w), lambda i: (cur(i), 0))
    prev_block = pl.BlockSpec(
        (BLOCK, 2 * KV_WIDTH), lambda i: (jnp.maximum(cur(i) * blocks_per_tile - 1, 0), 0))
    out_chunks = d // OUT_CHUNK
    return pl.pallas_call(
        functools.partial(_attn_out_kernel, n_tiles=n_tiles, tiles_per_batch=tiles_per_batch),
        out_shape=jax.ShapeDtypeStruct((t, d), F32),
        grid=(n_tiles + 1,),
        in_specs=[pl.BlockSpec(memory_space=pltpu.SMEM),
                  row(ATTN_WIDTH), row(ATTN_WIDTH), row(2 * KV_WIDTH), prev_block,
                  pl.BlockSpec((CHUNK_ROWS, y2.shape[1]), lambda i: (cur(i), 0)),
                  row(SSM_WIDTH),
                  pl.BlockSpec((ROW_TILE, d), lambda i: (old(i), 0)),
                  pl.BlockSpec((1, 1, d), lambda i: (old(i) // tiles_per_batch, 0, 0)),
                  _resident((1, ATTN_WIDTH)),
                  _resident((SSM_WIDTH, SSM_WIDTH)), _resident((1, SSM_WIDTH)),
                  _resident((1, SSM_WIDTH)),
                  _resident((d, d)), _resident((1, d))],
        out_specs=pl.BlockSpec((ROW_TILE, d), lambda i: (old(i), 0)),
        scratch_shapes=[pltpu.VMEM((ROW_TILE + BLOCK, 2 * KV_WIDTH), BF16),
                        pltpu.VMEM((BLOCK, ATTN_WIDTH), F32),
                        pltpu.VMEM((N_GROUP_BLOCKS, ROW_TILE, LANES), F32),
                        pltpu.VMEM((2, ROW_TILE, d), BF16),
                        pltpu.VMEM((out_chunks, ROW_TILE, OUT_CHUNK), F32)],
        compiler_params=pltpu.CompilerParams(dimension_semantics=("arbitrary",),
                                             vmem_limit_bytes=60 * 1024 * 1024),
        name="attn_out",
    )(sinks, q, za, kv, kv, y2, zs, x2, gate, attn_gain, glu_w, glu_b, ssm_gain, w_out, final_gain)


def kernel(x, c, w_ada, b_ada, norm_gain, w_in, b_in, attn_sinks, attn_out_gain, ssm_lambda_re,
           ssm_lambda_im, ssm_log_step, ssm_b_re, ssm_b_im, ssm_c_re, ssm_c_im, ssm_d, glu_w, glu_b,
           ssm_out_gain, w_out, final_gain):
    bsz, seq, d = x.shape
    assert w_ada.shape[0] == 1, "single-layer trunk only"
    assert d == D_MODEL and w_in.shape[2] == 2 * (ATTN_WIDTH + KV_WIDTH + SSM_WIDTH)
    assert seq % ROW_TILE == 0 and (bsz * seq // SSM_CHUNK) % SSM_TILE == 0
    assert (seq // SSM_CHUNK) % SCAN_TILE == 0
    x2 = x.reshape(bsz * seq, d)
    mod, w_in16 = _ada(c, w_ada[0], b_ada[0], w_in[0])
    shift, scale, gate = (m.reshape(bsz, 1, d) for m in jnp.split(mod, 3, axis=-1))
    q, kv, za, zs, u2, w_out16 = _inproj(x2, scale, shift, norm_gain[0].reshape(1, d),
                                         w_in16, b_in[0].reshape(1, -1), w_out[0], seq)
    prep = _ssm_prep(ssm_lambda_re[0], ssm_lambda_im[0], ssm_log_step[0], ssm_b_re[0],
                     ssm_b_im[0], ssm_c_re[0], ssm_c_im[0], ssm_d[0])
    y2 = _s5(u2, prep, bsz)
    out = _attn_out(attn_sinks[0], q, za, kv, y2, zs, x2, gate,
                    attn_out_gain[0].reshape(1, -1), glu_w[0].astype(BF16),
                    glu_b[0].reshape(1, -1), ssm_out_gain[0].reshape(1, -1),
                    w_out16, final_gain.reshape(1, d), seq)
    return out.reshape(bsz, seq, d)
```

```python
import functools
import math

import jax
import jax.numpy as jnp
from jax import lax
from jax.experimental import pallas as pl
from jax.experimental.pallas import tpu as pltpu

F32 = jnp.float32
BF16 = jnp.bfloat16

D_MODEL = 2048
HEAD_DIM = 64
N_Q_HEADS = 24
N_KV_HEADS = 3
Q_PER_KV = 8
ATTN_WIDTH = N_Q_HEADS * HEAD_DIM
BLOCK = 128
SSM_GROUP = 16
SSM_WIDTH = D_MODEL - ATTN_WIDTH
N_SSM_GROUPS = SSM_WIDTH // SSM_GROUP
STATE = 64
EPS = 1e-5
NEG = -1e30
LOG2_E = math.log2(math.e)

LANES = 128
SSM_CHUNK = 16
GROUPS_PER_BLOCK = LANES // SSM_GROUP
N_GROUP_BLOCKS = N_SSM_GROUPS // GROUPS_PER_BLOCK
SLAB = SSM_CHUNK * LANES
STATE_COLS = GROUPS_PER_BLOCK * STATE
KV_WIDTH = N_KV_HEADS * HEAD_DIM
PAIR = 2 * HEAD_DIM
Q_SCALE = LOG2_E * HEAD_DIM ** -0.5
PROJ_COLS = 512

VMEM_LIMIT = 56 * 1024 * 1024

ROW_TILE = 512
CHUNK_ROWS = ROW_TILE // SSM_CHUNK
ADA_COLS = 768
OUT_CHUNK = D_MODEL // (ROW_TILE // BLOCK)
SSM_TILE = 1024
SCAN_TILE = 128
SCAN_PITCH = 24


def _resident(shape):
    return pl.BlockSpec(shape, lambda *_: (0,) * len(shape), pipeline_mode=pl.Buffered(1))


def _ada_kernel(ct_ref, w_ref, b_ref, win_ref, o_ref, win16_ref):
    win16_ref[...] = win_ref[...].astype(BF16)
    ct = ct_ref[...]
    s = ct * jax.nn.sigmoid(ct)
    w = w_ref[...]
    rows = [jnp.sum(s[:, b:b + 1] * w, axis=0, keepdims=True) for b in range(ct.shape[1])]
    o_ref[...] = jnp.concatenate(rows, axis=0) + b_ref[...]


def _ada(c, w_ada, b_ada, w_in):
    bsz, d = c.shape
    n = w_ada.shape[1]
    steps = n // ADA_COLS
    slab = pl.BlockSpec((w_in.shape[0] // steps, w_in.shape[1]), lambda i: (i, 0))
    return pl.pallas_call(
        _ada_kernel,
        out_shape=[jax.ShapeDtypeStruct((bsz, n), F32), jax.ShapeDtypeStruct(w_in.shape, BF16)],
        grid=(steps,),
        in_specs=[pl.BlockSpec((d, bsz), lambda i: (0, 0)),
                  pl.BlockSpec((d, ADA_COLS), lambda i: (0, i)),
                  pl.BlockSpec((1, ADA_COLS), lambda i: (0, i)),
                  slab],
        out_specs=[pl.BlockSpec((bsz, ADA_COLS), lambda i: (0, i)), slab],
        compiler_params=pltpu.CompilerParams(dimension_semantics=("arbitrary",),
                                             vmem_limit_bytes=VMEM_LIMIT),
        name="ada",
    )(c.T, w_ada, b_ada.reshape(1, n), w_in)


def _inproj_kernel(x_ref, sc_ref, sh_ref, g_ref, w_ref, b_ref, wo_ref,
                   q_ref, kv_ref, za_ref, zs_ref, u2_ref, wo16_ref, h_scr, u_scr):
    wo16_ref[...] = wo_ref[...].astype(BF16)
    x = x_ref[...]
    var = jnp.mean(x * x, axis=-1, keepdims=True)
    gain = g_ref[...] * (1.0 + sc_ref[0])
    h_scr[...] = (x * lax.rsqrt(var + EPS) * gain + sh_ref[0]).astype(BF16)

    def project(col, width):
        acc = jnp.dot(h_scr[...], w_ref[:, col:col + width], preferred_element_type=F32)
        return acc + b_ref[:, col:col + width]

    col_kv, col_za = ATTN_WIDTH, ATTN_WIDTH + 2 * KV_WIDTH
    col_u = col_za + ATTN_WIDTH
    col_zs = col_u + SSM_WIDTH
    for c0 in range(0, ATTN_WIDTH, PROJ_COLS):
        q_ref[:, c0:c0 + PROJ_COLS] = (project(c0, PROJ_COLS) * Q_SCALE).astype(BF16)
        za_ref[:, c0:c0 + PROJ_COLS] = project(col_za + c0, PROJ_COLS).astype(BF16)
    kv_ref[...] = project(col_kv, 2 * KV_WIDTH).astype(BF16)
    zs_ref[...] = project(col_zs, SSM_WIDTH).astype(BF16)
    u = project(col_u, SSM_WIDTH)
    for gb in range(N_GROUP_BLOCKS):
        u_scr[gb] = u[:, gb * LANES:(gb + 1) * LANES]
        for tok in range(SSM_CHUNK):
            c0 = gb * SLAB + tok * LANES
            u2_ref[:, c0:c0 + LANES] = (
                u_scr[gb, pl.ds(tok, CHUNK_ROWS, stride=SSM_CHUNK), :].astype(BF16))


def _inproj(x2, scale, shift, gain, w_cat, b_cat, w_out, rows_per_batch):
    t, d = x2.shape
    widths = (ATTN_WIDTH, 2 * KV_WIDTH, ATTN_WIDTH, SSM_WIDTH)
    n_all = w_cat.shape[1]
    n_tiles = t // ROW_TILE
    tiles_per_batch = rows_per_batch // ROW_TILE
    mod_spec = pl.BlockSpec((1, 1, d), lambda i: (i // tiles_per_batch, 0, 0))
    u2_cols = N_GROUP_BLOCKS * SLAB
    wo_spec = pl.BlockSpec((w_out.shape[0] // n_tiles, w_out.shape[1]), lambda i: (i, 0))
    return pl.pallas_call(
        _inproj_kernel,
        out_shape=[jax.ShapeDtypeStruct((t, w), BF16) for w in widths]
        + [jax.ShapeDtypeStruct((t // SSM_CHUNK, u2_cols), BF16),
           jax.ShapeDtypeStruct(w_out.shape, BF16)],
        grid=(n_tiles,),
        in_specs=[pl.BlockSpec((ROW_TILE, d), lambda i: (i, 0)),
                  mod_spec, mod_spec,
                  _resident((1, d)),
                  _resident((d, n_all)),
                  _resident((1, n_all)),
                  wo_spec],
        out_specs=[pl.BlockSpec((ROW_TILE, w), lambda i: (i, 0)) for w in widths]
        + [pl.BlockSpec((CHUNK_ROWS, u2_cols), lambda i: (i, 0)), wo_spec],
        scratch_shapes=[pltpu.VMEM((ROW_TILE, d), BF16),
                        pltpu.VMEM((N_GROUP_BLOCKS, ROW_TILE, LANES), F32)],
        compiler_params=pltpu.CompilerParams(dimension_semantics=("arbitrary",),
                                             vmem_limit_bytes=VMEM_LIMIT),
        name="inproj",
    )(x2, scale, shift, gain, w_cat, b_cat, w_out)


def _ssm_prep_kernel(lr_ref, li_ref, ls_ref, br_ref, bi_ref, cr_ref, ci_ref, d_ref,
                     bd_ref, vr_ref, vi_ref, wr_ref, wi_ref, a16_ref, y_scr):
    lr, li = lr_ref[...], li_ref[...]
    step = jnp.exp(ls_ref[...])
    decay = jnp.exp(lr * step)
    ar, ai = decay * jnp.cos(li * step), decay * jnp.sin(li * step)
    den = lr * lr + li * li
    nr, ni = ar - 1.0, ai
    coef_re = (nr * lr + ni * li) / den
    coef_im = (ni * lr - nr * li) / den
    br, bi = br_ref[...], bi_ref[...]
    bbar_re = coef_re * br - coef_im * bi
    bbar_im = coef_re * bi + coef_im * br
    cr, ci = cr_ref[...], ci_ref[...]
    lo = lax.broadcasted_iota(jnp.int32, lr.shape, 2) < STATE
    pr, pi = jnp.ones_like(ar), jnp.zeros_like(ai)
    for tau in range(SSM_CHUNK + 1):
        er, ei = pr * cr - pi * ci, pr * ci + pi * cr
        if tau < SSM_CHUNK:
            y_scr[:, tau * LANES:(tau + 1) * LANES, :] = jnp.where(lo, er, ei)
            s = SSM_CHUNK - 1 - tau
            vr_ref[:, s * LANES:(s + 1) * LANES, :] = pr * bbar_re - pi * bbar_im
            vi_ref[:, s * LANES:(s + 1) * LANES, :] = pr * bbar_im + pi * bbar_re
        if tau >= 1:
            wr_ref[:, (tau - 1) * LANES:tau * LANES, :] = er
            wi_ref[:, (tau - 1) * LANES:tau * LANES, :] = -ei
        if tau == SSM_CHUNK:
            a16_ref[...] = jnp.where(lo, pr, pi)
        pr, pi = pr * ar - pi * ai, pr * ai + pi * ar
    x = jnp.where(lo, bbar_re, -bbar_im)
    row = lax.broadcasted_iota(jnp.int32, (LANES, SLAB), 0)
    col = lax.broadcasted_iota(jnp.int32, (LANES, SLAB), 1)
    same_group = row // SSM_GROUP == (col % LANES) // SSM_GROUP
    diagonal = (lax.broadcasted_iota(jnp.int32, (LANES, LANES), 0)
                == lax.broadcasted_iota(jnp.int32, (LANES, LANES), 1))
    for gb in range(N_GROUP_BLOCKS):
        k = lax.dot_general(x[gb], y_scr[gb], (((1,), (1,)), ((), ())),
                            precision=lax.Precision.HIGHEST, preferred_element_type=F32)
        k = jnp.where(same_group, k, 0.0)
        bd_ref[gb, :, LANES:] = k[:, LANES:]
        bd_ref[gb, :, :LANES] = k[:, :LANES] + jnp.where(diagonal, d_ref[gb], 0.0)


def _ssm_prep(lam_re, lam_im, log_step, b_re, b_im, c_re, c_im, d_skip):
    nb = N_GROUP_BLOCKS
    rows = lambda a: jnp.concatenate([a, a], axis=-1).reshape(nb, LANES, 2 * STATE)
    per_group = lambda a: rows(jnp.repeat(a[:, None, :], SSM_GROUP, axis=1))
    compact = jax.ShapeDtypeStruct((nb, SLAB, 2 * STATE), F32)
    return pl.pallas_call(
        _ssm_prep_kernel,
        out_shape=[jax.ShapeDtypeStruct((nb, LANES, SLAB), F32),
                   compact, compact,
                   compact, compact,
                   jax.ShapeDtypeStruct((nb, LANES, 2 * STATE), F32)],
        scratch_shapes=[pltpu.VMEM((nb, SLAB, 2 * STATE), F32)],
        compiler_params=pltpu.CompilerParams(vmem_limit_bytes=VMEM_LIMIT),
        name="ssm_prep",
    )(per_group(lam_re), per_group(lam_im),
      jnp.broadcast_to(jnp.repeat(log_step, SSM_GROUP).reshape(nb, LANES, 1), (nb, LANES, 2 * STATE)),
      rows(jnp.swapaxes(b_re, 1, 2)), rows(jnp.swapaxes(b_im, 1, 2)), rows(c_re), rows(c_im),
      jnp.broadcast_to(d_skip.reshape(nb, 1, LANES), (nb, LANES, LANES)))


def _expand_block_diagonal(re_ref, im_ref, dst):
    shape = (SLAB, 2 * STATE)
    row_group = (lax.broadcasted_iota(jnp.int32, shape, 0) % LANES) // SSM_GROUP
    half = lax.broadcasted_iota(jnp.int32, shape, 1) // STATE
    for j in range(STATE_COLS // LANES):
        own = row_group == 2 * j + half
        for src, c0 in ((re_ref, 0), (im_ref, STATE_COLS)):
            dst[:, c0 + j * LANES:c0 + (j + 1) * LANES] = jnp.where(own, src[0], 0.0).astype(BF16)


def _ssm_state_kernel(u_ref, vr_ref, vi_ref, sr_ref, si_ref, v_scr):
    @pl.when(pl.program_id(1) == 0)
    def _():
        _expand_block_diagonal(vr_ref, vi_ref, v_scr)

    s = jnp.dot(u_ref[...], v_scr[...], preferred_element_type=F32)
    sr_ref[...] = s[:, :STATE_COLS]
    si_ref[...] = s[:, STATE_COLS:]


def _ssm_state(u2, v_re, v_im):
    n = u2.shape[0]
    state = pl.BlockSpec((SSM_TILE, STATE_COLS), lambda gb, i: (i, gb))
    compact = pl.BlockSpec((1, SLAB, 2 * STATE), lambda gb, i: (gb, 0, 0))
    return pl.pallas_call(
        _ssm_state_kernel,
        out_shape=[jax.ShapeDtypeStruct((n, N_GROUP_BLOCKS * STATE_COLS), F32)] * 2,
        grid=(N_GROUP_BLOCKS, n // SSM_TILE),
        in_specs=[pl.BlockSpec((SSM_TILE, SLAB), lambda gb, i: (i, gb)), compact, compact],
        out_specs=[state, state],
        scratch_shapes=[pltpu.VMEM((SLAB, 2 * STATE_COLS), BF16)],
        compiler_params=pltpu.CompilerParams(dimension_semantics=("arbitrary", "arbitrary"),
                                             vmem_limit_bytes=VMEM_LIMIT),
        name="ssm_state",
    )(u2, v_re, v_im)


def _ssm_scan_kernel(sr_ref, si_ref, ar_ref, ai_ref, hr_ref, hi_ref, cr_scr, ci_scr, s_scr, h_scr):
    @pl.when(pl.program_id(0) == 0)
    def _():
        cr_scr[...] = jnp.zeros_like(cr_scr)
        ci_scr[...] = jnp.zeros_like(ci_scr)

    ar, ai = ar_ref[...], ai_ref[...]
    nb = sr_ref.shape[0]
    tiles = ar.shape[0]
    for b in range(nb):
        for part, ref in enumerate((sr_ref, si_ref)):
            for r in range(tiles):
                s_scr[part, b, pl.ds(r, SCAN_TILE, stride=SCAN_PITCH), :] = (
                    ref[b, :, r * LANES:(r + 1) * LANES])

    def body(k, carry):
        rows = pl.ds(pl.multiple_of(k * SCAN_PITCH, 8), tiles)
        out = []
        for b in range(nb):
            hr, hi = carry[2 * b], carry[2 * b + 1]
            h_scr[0, b, rows, :] = hr
            h_scr[1, b, rows, :] = hi
            out += [ar * hr - ai * hi + s_scr[0, b, rows, :], ar * hi + ai * hr + s_scr[1, b, rows, :]]
        return tuple(out)

    init = tuple(scr[b] for b in range(nb) for scr in (cr_scr, ci_scr))
    fin = lax.fori_loop(0, SCAN_TILE, body, init, unroll=8)
    for b in range(nb):
        cr_scr[b] = fin[2 * b]
        ci_scr[b] = fin[2 * b + 1]
        for part, ref in enumerate((hr_ref, hi_ref)):
            for r in range(tiles):
                ref[b, :, r * LANES:(r + 1) * LANES] = (
                    h_scr[part, b, pl.ds(r, SCAN_TILE, stride=SCAN_PITCH), :].astype(BF16))


def _ssm_scan(s_re, s_im, a_re, a_im):
    bsz, k, cols = s_re.shape
    rows, lanes = a_re.shape
    blk = pl.BlockSpec((bsz, SCAN_TILE, cols), lambda i: (0, i, 0))
    return pl.pallas_call(
        _ssm_scan_kernel,
        out_shape=[jax.ShapeDtypeStruct(s_re.shape, BF16)] * 2,
        grid=(k // SCAN_TILE,),
        in_specs=[blk, blk, _resident((rows, lanes)), _resident((rows, lanes))],
        out_specs=[blk, blk],
        scratch_shapes=[pltpu.VMEM((bsz, rows, lanes), F32)] * 2
        + [pltpu.VMEM((2, bsz, SCAN_TILE * SCAN_PITCH, lanes), F32)] * 2,
        compiler_params=pltpu.CompilerParams(dimension_semantics=("arbitrary",)),
        name="ssm_scan",
    )(s_re, s_im, a_re, a_im)


def _ssm_out_kernel(u_ref, hr_ref, hi_ref, bd_ref, wr_ref, wi_ref, y_ref, t_scr, wt_scr):
    @pl.when(pl.program_id(1) == 0)
    def _():
        bd = bd_ref[0].astype(BF16)
        for s in range(SSM_CHUNK):
            rows = slice(s * LANES, (s + 1) * LANES)
            if s:
                t_scr[rows, :s * LANES] = jnp.zeros((LANES, s * LANES), BF16)
            t_scr[rows, s * LANES:] = bd[:, :(SSM_CHUNK - s) * LANES]
        _expand_block_diagonal(wr_ref, wi_ref, wt_scr)

    pair = 2 * LANES
    contract_last = (((1,), (1,)), ((), ()))
    for tb in range(SLAB // pair):
        k_rows = (tb + 1) * pair
        cols = slice(tb * pair, (tb + 1) * pair)
        y = jnp.dot(u_ref[:, :k_rows], t_scr[:k_rows, cols], preferred_element_type=F32)
        y += lax.dot_general(hr_ref[...], wt_scr[cols, :STATE_COLS], contract_last,
                             preferred_element_type=F32)
        y += lax.dot_general(hi_ref[...], wt_scr[cols, STATE_COLS:], contract_last,
                             preferred_element_type=F32)
        y_ref[:, cols] = y


def _ssm_out(u2, h_re, h_im, bd, w_re, w_im):
    n = u2.shape[0]
    state = pl.BlockSpec((SSM_TILE, STATE_COLS), lambda gb, i: (i, gb))
    slab = pl.BlockSpec((SSM_TILE, SLAB), lambda gb, i: (i, gb))
    compact = pl.BlockSpec((1, SLAB, 2 * STATE), lambda gb, i: (gb, 0, 0))
    return pl.pallas_call(
        _ssm_out_kernel,
        out_shape=jax.ShapeDtypeStruct(u2.shape, F32),
        grid=(N_GROUP_BLOCKS, n // SSM_TILE),
        in_specs=[slab, state, state,
                  pl.BlockSpec((1, LANES, SLAB), lambda gb, i: (gb, 0, 0)), compact, compact],
        out_specs=slab,
        scratch_shapes=[pltpu.VMEM((SLAB, SLAB), BF16), pltpu.VMEM((SLAB, 2 * STATE_COLS), BF16)],
        compiler_params=pltpu.CompilerParams(dimension_semantics=("arbitrary", "arbitrary"),
                                             vmem_limit_bytes=VMEM_LIMIT),
        name="ssm_out",
    )(u2, h_re, h_im, bd, w_re, w_im)


def _s5(u2, prep, bsz):
    bd, v_re, v_im, w_re, w_im = prep[:5]
    a16 = prep[5].reshape(N_SSM_GROUPS, SSM_GROUP, 2 * STATE)[:, 0]
    n = u2.shape[0]
    kc = n // bsz
    s_re, s_im = _ssm_state(u2, v_re, v_im)
    tile_shape = (N_SSM_GROUPS * STATE // LANES, LANES)
    a_re = a16[:, :STATE].reshape(tile_shape)
    a_im = a16[:, STATE:].reshape(tile_shape)
    h_re, h_im = _ssm_scan(s_re.reshape(bsz, kc, -1), s_im.reshape(bsz, kc, -1), a_re, a_im)
    return _ssm_out(u2, h_re.reshape(n, -1), h_im.reshape(n, -1), bd, w_re, w_im)


def _swap_lane_halves(x):
    packed = pltpu.bitcast(x, jnp.uint32)
    return pltpu.bitcast(pltpu.roll(packed, HEAD_DIM, axis=1), BF16)


def _attn_out_kernel(*refs, n_tiles, tiles_per_batch):
    step = pl.program_id(0)
    body = functools.partial(_attn_out_step, *refs, tiles_per_batch=tiles_per_batch)
    pl.when(step == 0)(functools.partial(body, do_attn=True, do_proj=False))
    pl.when((step > 0) & (step < n_tiles))(functools.partial(body, do_attn=True, do_proj=True))
    pl.when(step == n_tiles)(functools.partial(body, do_attn=False, do_proj=True))


def _attn_out_step(sink_ref, q_ref, za_ref, kv_ref, kvp_ref, y2_ref, zs_ref,
                   x_ref, gate_ref, ag_ref, gw_ref, gb_ref, sg_ref, wo_ref, fg_ref,
                   o_ref, kv_scr, attn_scr, y_scr, mix_scr, out_scr,
                   *, tiles_per_batch, do_attn, do_proj):
    step = pl.program_id(0)
    cur_slot = step % 2
    prev_slot = 1 - cur_slot
    first = step % tiles_per_batch == 0
    if do_attn:
        keep = jnp.where(first, 0.0, 1.0).astype(BF16)
        kv_scr[0:BLOCK, :] = kvp_ref[...] * keep
        kv_scr[BLOCK:, :] = kv_ref[...]

    older = (lax.broadcasted_iota(jnp.int32, (BLOCK, BLOCK), 1)
             > lax.broadcasted_iota(jnp.int32, (BLOCK, BLOCK), 0))
    kv_half = lax.broadcasted_iota(jnp.int32, (2 * BLOCK, LANES), 1) // HEAD_DIM
    pairs = Q_PER_KV // 2

    def both_halves(r0, col):
        tile, half = col // LANES, (col % LANES) // HEAD_DIM
        band = kv_scr[pl.ds(r0, 2 * BLOCK), tile * LANES:(tile + 1) * LANES]
        own = jnp.where(kv_half == half, band, jnp.zeros_like(band))
        other = _swap_lane_halves(own)
        return (own, other) if half == 0 else (other, own)

    contract_last = (((1,), (1,)), ((), ()))
    for gb in range(N_GROUP_BLOCKS if do_attn else 0):
        for tok in range(SSM_CHUNK):
            c0 = gb * SLAB + tok * LANES
            y_scr[gb, pl.ds(tok, CHUNK_ROWS, stride=SSM_CHUNK), :] = y2_ref[:, c0:c0 + LANES]

    def block(j, carry):
        r0 = j * BLOCK
        rows = pl.ds(r0, BLOCK)
        if do_proj:
            out_scr[j] = jnp.dot(mix_scr[prev_slot], wo_ref[:, j * OUT_CHUNK:(j + 1) * OUT_CHUNK],
                                 preferred_element_type=F32)
        if not do_attn:
            return carry
        no_past = jnp.where(first, NEG, 0.0) if j == 0 else None
        heads = range(N_KV_HEADS)
        head_q = lambda h, p: h * Q_PER_KV * HEAD_DIM + p * PAIR
        k_pars = [both_halves(r0, h * HEAD_DIM) for h in heads]
        v_pars = [both_halves(r0, KV_WIDTH + h * HEAD_DIM) for h in heads]
        qss = [jnp.concatenate(
            [q_ref[pl.ds(r0, BLOCK), head_q(h, p):head_q(h, p + 1)] for p in range(pairs)], axis=0)
            for h in heads]
        s_alls = {(h, par): lax.dot_general(qss[h], k_pars[h][par], contract_last,
                                            preferred_element_type=F32)
                  for h in heads for par in range(2)}
        probs, invs = {}, {}
        for h in heads:
            for par in range(2):
                s_all = s_alls[h, par]
                pr, iv = [], []
                for p in range(pairs):
                    past = s_all[p * BLOCK:(p + 1) * BLOCK, :BLOCK]
                    if no_past is not None:
                        past = past + no_past
                    s = jnp.where(older, past, s_all[p * BLOCK:(p + 1) * BLOCK, BLOCK:])
                    sink = sink_ref[h * Q_PER_KV + 2 * p + par] * LOG2_E
                    m = jnp.max(s, axis=-1, keepdims=True)
                    e = jnp.exp2(s - m)
                    l = jnp.sum(e, axis=-1, keepdims=True) + jnp.exp2(sink - m)
                    iv.append(1.0 / l)
                    e = e.astype(BF16)
                    zero = jnp.zeros_like(e)
                    pr.append(jnp.concatenate(
                        [jnp.where(older, e, zero), jnp.where(older, zero, e)], axis=1))
                probs[h, par], invs[h, par] = jnp.concatenate(pr, axis=0), iv
        pvs = {(h, par): jnp.dot(probs[h, par], v_pars[h][par], preferred_element_type=F32)
               for h in heads for par in range(2)}
        for h in heads:
            for p in range(pairs):
                blk = slice(p * BLOCK, (p + 1) * BLOCK)
                attn_scr[:, head_q(h, p):head_q(h, p + 1)] = (
                    pvs[h, 0][blk] * invs[h, 0][p] + pvs[h, 1][blk] * invs[h, 1][p])
        a = attn_scr[...]
        a = a * lax.rsqrt(jnp.mean(a * a, axis=-1, keepdims=True) + EPS) * ag_ref[...]
        za = za_ref[rows, :]
        mix_scr[cur_slot, rows, :ATTN_WIDTH] = a.astype(BF16) * (za * jax.nn.sigmoid(za))
        y = jnp.concatenate([y_scr[gb, rows, :] for gb in range(N_GROUP_BLOCKS)], axis=1)
        y = 0.5 * y * (1.0 + lax.erf(y * math.sqrt(0.5)))
        glu = jnp.dot(y.astype(BF16), gw_ref[...], preferred_element_type=F32) + gb_ref[...]
        y = y * jax.nn.sigmoid(glu)
        y = y * lax.rsqrt(jnp.mean(y * y, axis=-1, keepdims=True) + EPS) * sg_ref[...]
        zs = zs_ref[rows, :]
        mix_scr[cur_slot, rows, ATTN_WIDTH:] = y.astype(BF16) * (zs * jax.nn.sigmoid(zs))
        return carry

    for j in range(q_ref.shape[0] // BLOCK):
        block(j, 0)

    if do_proj:
        out = jnp.concatenate([out_scr[c] for c in range(out_scr.shape[0])], axis=1)
        res = x_ref[...] + gate_ref[0] * out
        o_ref[...] = (res * lax.rsqrt(jnp.mean(res * res, axis=-1, keepdims=True) + EPS)
                      * fg_ref[...])


def _attn_out(sinks, q, za, kv, y2, zs, x2, gate, attn_gain, glu_w, glu_b, ssm_gain, w_out,
              final_gain, rows_per_batch):
    t, d = x2.shape
    n_tiles = t // ROW_TILE
    tiles_per_batch = rows_per_batch // ROW_TILE
    blocks_per_tile = ROW_TILE // BLOCK
    cur = lambda i: jnp.minimum(i, n_tiles - 1)
    old = lambda i: jnp.maximum(i - 1, 0)
    row = lambda w: pl.BlockSpec((ROW_TILE, w), lambda i: (cur(i), 0))
    prev_block = pl.BlockSpec(
        (BLOCK, 2 * KV_WIDTH), lambda i: (jnp.maximum(cur(i) * blocks_per_tile - 1, 0), 0))
    out_chunks = d // OUT_CHUNK
    return pl.pallas_call(
        functools.partial(_attn_out_kernel, n_tiles=n_tiles, tiles_per_batch=tiles_per_batch),
        out_shape=jax.ShapeDtypeStruct((t, d), F32),
        grid=(n_tiles + 1,),
        in_specs=[pl.BlockSpec(memory_space=pltpu.SMEM),
                  row(ATTN_WIDTH), row(ATTN_WIDTH), row(2 * KV_WIDTH), prev_block,
                  pl.BlockSpec((CHUNK_ROWS, y2.shape[1]), lambda i: (cur(i), 0)),
                  row(SSM_WIDTH),
                  pl.BlockSpec((ROW_TILE, d), lambda i: (old(i), 0)),
                  pl.BlockSpec((1, 1, d), lambda i: (old(i) // tiles_per_batch, 0, 0)),
                  _resident((1, ATTN_WIDTH)),
                  _resident((SSM_WIDTH, SSM_WIDTH)), _resident((1, SSM_WIDTH)),
                  _resident((1, SSM_WIDTH)),
                  _resident((d, d)), _resident((1, d))],
        out_specs=pl.BlockSpec((ROW_TILE, d), lambda i: (old(i), 0)),
        scratch_shapes=[pltpu.VMEM((ROW_TILE + BLOCK, 2 * KV_WIDTH), BF16),
                        pltpu.VMEM((BLOCK, ATTN_WIDTH), F32),
                        pltpu.VMEM((N_GROUP_BLOCKS, ROW_TILE, LANES), F32),
                        pltpu.VMEM((2, ROW_TILE, d), BF16),
                        pltpu.VMEM((out_chunks, ROW_TILE, OUT_CHUNK), F32)],
        compiler_params=pltpu.CompilerParams(dimension_semantics=("arbitrary",),
                                             vmem_limit_bytes=60 * 1024 * 1024),
        name="attn_out",
    )(sinks, q, za, kv, kv, y2, zs, x2, gate, attn_gain, glu_w, glu_b, ssm_gain, w_out, final_gain)


def kernel(x, c, w_ada, b_ada, norm_gain, w_in, b_in, attn_sinks, attn_out_gain, ssm_lambda_re,
           ssm_lambda_im, ssm_log_step, ssm_b_re, ssm_b_im, ssm_c_re, ssm_c_im, ssm_d, glu_w, glu_b,
           ssm_out_gain, w_out, final_gain):
    bsz, seq, d = x.shape
    assert w_ada.shape[0] == 1, "single-layer trunk only"
    assert d == D_MODEL and w_in.shape[2] == 2 * (ATTN_WIDTH + KV_WIDTH + SSM_WIDTH)
    assert seq % ROW_TILE == 0 and (bsz * seq // SSM_CHUNK) % SSM_TILE == 0
    assert (seq // SSM_CHUNK) % SCAN_TILE == 0
    x2 = x.reshape(bsz * seq, d)
    mod, w_in16 = _ada(c, w_ada[0], b_ada[0], w_in[0])
    shift, scale, gate = (m.reshape(bsz, 1, d) for m in jnp.split(mod, 3, axis=-1))
    q, kv, za, zs, u2, w_out16 = _inproj(x2, scale, shift, norm_gain[0].reshape(1, d),
                                         w_in16, b_in[0].reshape(1, -1), w_out[0], seq)
    prep = _ssm_prep(ssm_lambda_re[0], ssm_lambda_im[0], ssm_log_step[0], ssm_b_re[0],
                     ssm_b_im[0], ssm_c_re[0], ssm_c_im[0], ssm_d[0])
    y2 = _s5(u2, prep, bsz)
    out = _attn_out(attn_sinks[0], q, za, kv, y2, zs, x2, gate,
                    attn_out_gain[0].reshape(1, -1), glu_w[0].astype(BF16),
                    glu_b[0].reshape(1, -1), ssm_out_gain[0].reshape(1, -1),
                    w_out16, final_gain.reshape(1, d), seq)
    return out.reshape(bsz, seq, d)
```

```python
import functools
import math

import jax
import jax.numpy as jnp
from jax import lax
from jax.experimental import pallas as pl
from jax.experimental.pallas import tpu as pltpu

F32 = jnp.float32
BF16 = jnp.bfloat16

D_MODEL = 2048
HEAD_DIM = 64
N_Q_HEADS = 24
N_KV_HEADS = 3
Q_PER_KV = 8
ATTN_WIDTH = N_Q_HEADS * HEAD_DIM
BLOCK = 128
SSM_GROUP = 16
SSM_WIDTH = D_MODEL - ATTN_WIDTH
N_SSM_GROUPS = SSM_WIDTH // SSM_GROUP
STATE = 64
EPS = 1e-5
NEG = -1e30
LOG2_E = math.log2(math.e)

LANES = 128
SSM_CHUNK = 16
GROUPS_PER_BLOCK = LANES // SSM_GROUP
N_GROUP_BLOCKS = N_SSM_GROUPS // GROUPS_PER_BLOCK
SLAB = SSM_CHUNK * LANES
STATE_COLS = GROUPS_PER_BLOCK * STATE
KV_WIDTH = N_KV_HEADS * HEAD_DIM
PAIR = 2 * HEAD_DIM
Q_SCALE = LOG2_E * HEAD_DIM ** -0.5
PROJ_COLS = 512

VMEM_LIMIT = 56 * 1024 * 1024

ROW_TILE = 512
CHUNK_ROWS = ROW_TILE // SSM_CHUNK
ADA_COLS = 768
OUT_CHUNK = D_MODEL // (ROW_TILE // BLOCK)
SSM_TILE = 1024
SCAN_TILE = 128
SCAN_PITCH = 24


def _resident(shape):
    return pl.BlockSpec(shape, lambda *_: (0,) * len(shape), pipeline_mode=pl.Buffered(1))


def _ada_kernel(ct_ref, w_ref, b_ref, win_ref, o_ref, win16_ref):
    win16_ref[...] = win_ref[...].astype(BF16)
    ct = ct_ref[...]
    s = ct * jax.nn.sigmoid(ct)
    w = w_ref[...]
    rows = [jnp.sum(s[:, b:b + 1] * w, axis=0, keepdims=True) for b in range(ct.shape[1])]
    o_ref[...] = jnp.concatenate(rows, axis=0) + b_ref[...]


def _ada(c, w_ada, b_ada, w_in):
    bsz, d = c.shape
    n = w_ada.shape[1]
    steps = n // ADA_COLS
    slab = pl.BlockSpec((w_in.shape[0] // steps, w_in.shape[1]), lambda i: (i, 0))
    return pl.pallas_call(
        _ada_kernel,
        out_shape=[jax.ShapeDtypeStruct((bsz, n), F32), jax.ShapeDtypeStruct(w_in.shape, BF16)],
        grid=(steps,),
        in_specs=[pl.BlockSpec((d, bsz), lambda i: (0, 0)),
                  pl.BlockSpec((d, ADA_COLS), lambda i: (0, i)),
                  pl.BlockSpec((1, ADA_COLS), lambda i: (0, i)),
                  slab],
        out_specs=[pl.BlockSpec((bsz, ADA_COLS), lambda i: (0, i)), slab],
        compiler_params=pltpu.CompilerParams(dimension_semantics=("arbitrary",),
                                             vmem_limit_bytes=VMEM_LIMIT),
        name="ada",
    )(c.T, w_ada, b_ada.reshape(1, n), w_in)


def _inproj_kernel(x_ref, sc_ref, sh_ref, g_ref, w_ref, b_ref, wo_ref,
                   q_ref, kv_ref, za_ref, zs_ref, u2_ref, wo16_ref, h_scr, u_scr):
    wo16_ref[...] = wo_ref[...].astype(BF16)
    x = x_ref[...]
    var = jnp.mean(x * x, axis=-1, keepdims=True)
    gain = g_ref[...] * (1.0 + sc_ref[0])
    h_scr[...] = (x * lax.rsqrt(var + EPS) * gain + sh_ref[0]).astype(BF16)

    def project(col, width):
        acc = jnp.dot(h_scr[...], w_ref[:, col:col + width], preferred_element_type=F32)
        return acc + b_ref[:, col:col + width]

    col_kv, col_za = ATTN_WIDTH, ATTN_WIDTH + 2 * KV_WIDTH
    col_u = col_za + ATTN_WIDTH
    col_zs = col_u + SSM_WIDTH
    for c0 in range(0, ATTN_WIDTH, PROJ_COLS):
        q_ref[:, c0:c0 + PROJ_COLS] = (project(c0, PROJ_COLS) * Q_SCALE).astype(BF16)
        za_ref[:, c0:c0 + PROJ_COLS] = project(col_za + c0, PROJ_COLS).astype(BF16)
    kv_ref[...] = project(col_kv, 2 * KV_WIDTH).astype(BF16)
    zs_ref[...] = project(col_zs, SSM_WIDTH).astype(BF16)
    u = project(col_u, SSM_WIDTH)
    for gb in range(N_GROUP_BLOCKS):
        u_scr[gb] = u[:, gb * LANES:(gb + 1) * LANES]
        for tok in range(SSM_CHUNK):
            c0 = gb * SLAB + tok * LANES
            u2_ref[:, c0:c0 + LANES] = (
                u_scr[gb, pl.ds(tok, CHUNK_ROWS, stride=SSM_CHUNK), :].astype(BF16))


def _inproj(x2, scale, shift, gain, w_cat, b_cat, w_out, rows_per_batch):
    t, d = x2.shape
    widths = (ATTN_WIDTH, 2 * KV_WIDTH, ATTN_WIDTH, SSM_WIDTH)
    n_all = w_cat.shape[1]
    n_tiles = t // ROW_TILE
    tiles_per_batch = rows_per_batch // ROW_TILE
    mod_spec = pl.BlockSpec((1, 1, d), lambda i: (i // tiles_per_batch, 0, 0))
    u2_cols = N_GROUP_BLOCKS * SLAB
    wo_spec = pl.BlockSpec((w_out.shape[0] // n_tiles, w_out.shape[1]), lambda i: (i, 0))
    return pl.pallas_call(
        _inproj_kernel,
        out_shape=[jax.ShapeDtypeStruct((t, w), BF16) for w in widths]
        + [jax.ShapeDtypeStruct((t // SSM_CHUNK, u2_cols), BF16),
           jax.ShapeDtypeStruct(w_out.shape, BF16)],
        grid=(n_tiles,),
        in_specs=[pl.BlockSpec((ROW_TILE, d), lambda i: (i, 0)),
                  mod_spec, mod_spec,
                  _resident((1, d)),
                  _resident((d, n_all)),
                  _resident((1, n_all)),
                  wo_spec],
        out_specs=[pl.BlockSpec((ROW_TILE, w), lambda i: (i, 0)) for w in widths]
        + [pl.BlockSpec((CHUNK_ROWS, u2_cols), lambda i: (i, 0)), wo_spec],
        scratch_shapes=[pltpu.VMEM((ROW_TILE, d), BF16),
                        pltpu.VMEM((N_GROUP_BLOCKS, ROW_TILE, LANES), F32)],
        compiler_params=pltpu.CompilerParams(dimension_semantics=("arbitrary",),
                                             vmem_limit_bytes=VMEM_LIMIT),
        name="inproj",
    )(x2, scale, shift, gain, w_cat, b_cat, w_out)


def _ssm_prep_kernel(lr_ref, li_ref, ls_ref, br_ref, bi_ref, cr_ref, ci_ref, d_ref,
                     bd_ref, vr_ref, vi_ref, wr_ref, wi_ref, a16_ref, y_scr):
    lr, li = lr_ref[...], li_ref[...]
    step = jnp.exp(ls_ref[...])
    decay = jnp.exp(lr * step)
    ar, ai = decay * jnp.cos(li * step), decay * jnp.sin(li * step)
    den = lr * lr + li * li
    nr, ni = ar - 1.0, ai
    coef_re = (nr * lr + ni * li) / den
    coef_im = (ni * lr - nr * li) / den
    br, bi = br_ref[...], bi_ref[...]
    bbar_re = coef_re * br - coef_im * bi
    bbar_im = coef_re * bi + coef_im * br
    cr, ci = cr_ref[...], ci_ref[...]
    lo = lax.broadcasted_iota(jnp.int32, lr.shape, 2) < STATE
    pr, pi = jnp.ones_like(ar), jnp.zeros_like(ai)
    for tau in range(SSM_CHUNK + 1):
        er, ei = pr * cr - pi * ci, pr * ci + pi * cr
        if tau < SSM_CHUNK:
            y_scr[:, tau * LANES:(tau + 1) * LANES, :] = jnp.where(lo, er, ei)
            s = SSM_CHUNK - 1 - tau
            vr_ref[:, s * LANES:(s + 1) * LANES, :] = pr * bbar_re - pi * bbar_im
            vi_ref[:, s * LANES:(s + 1) * LANES, :] = pr * bbar_im + pi * bbar_re
        if tau >= 1:
            wr_ref[:, (tau - 1) * LANES:tau * LANES, :] = er
            wi_ref[:, (tau - 1) * LANES:tau * LANES, :] = -ei
        if tau == SSM_CHUNK:
            a16_ref[...] = jnp.where(lo, pr, pi)
        pr, pi = pr * ar - pi * ai, pr * ai + pi * ar
    x = jnp.where(lo, bbar_re, -bbar_im)
    row = lax.broadcasted_iota(jnp.int32, (LANES, SLAB), 0)
    col = lax.broadcasted_iota(jnp.int32, (LANES, SLAB), 1)
    same_group = row // SSM_GROUP == (col % LANES) // SSM_GROUP
    diagonal = (lax.broadcasted_iota(jnp.int32, (LANES, LANES), 0)
                == lax.broadcasted_iota(jnp.int32, (LANES, LANES), 1))
    for gb in range(N_GROUP_BLOCKS):
        k = lax.dot_general(x[gb], y_scr[gb], (((1,), (1,)), ((), ())),
                            precision=lax.Precision.HIGHEST, preferred_element_type=F32)
        k = jnp.where(same_group, k, 0.0)
        bd_ref[gb, :, LANES:] = k[:, LANES:]
        bd_ref[gb, :, :LANES] = k[:, :LANES] + jnp.where(diagonal, d_ref[gb], 0.0)


def _ssm_prep(lam_re, lam_im, log_step, b_re, b_im, c_re, c_im, d_skip):
    nb = N_GROUP_BLOCKS
    rows = lambda a: jnp.concatenate([a, a], axis=-1).reshape(nb, LANES, 2 * STATE)
    per_group = lambda a: rows(jnp.repeat(a[:, None, :], SSM_GROUP, axis=1))
    compact = jax.ShapeDtypeStruct((nb, SLAB, 2 * STATE), F32)
    return pl.pallas_call(
        _ssm_prep_kernel,
        out_shape=[jax.ShapeDtypeStruct((nb, LANES, SLAB), F32),
                   compact, compact,
                   compact, compact,
                   jax.ShapeDtypeStruct((nb, LANES, 2 * STATE), F32)],
        scratch_shapes=[pltpu.VMEM((nb, SLAB, 2 * STATE), F32)],
        compiler_params=pltpu.CompilerParams(vmem_limit_bytes=VMEM_LIMIT),
        name="ssm_prep",
    )(per_group(lam_re), per_group(lam_im),
      jnp.broadcast_to(jnp.repeat(log_step, SSM_GROUP).reshape(nb, LANES, 1), (nb, LANES, 2 * STATE)),
      rows(jnp.swapaxes(b_re, 1, 2)), rows(jnp.swapaxes(b_im, 1, 2)), rows(c_re), rows(c_im),
      jnp.broadcast_to(d_skip.reshape(nb, 1, LANES), (nb, LANES, LANES)))


def _expand_block_diagonal(re_ref, im_ref, dst):
    shape = (SLAB, 2 * STATE)
    row_group = (lax.broadcasted_iota(jnp.int32, shape, 0) % LANES) // SSM_GROUP
    half = lax.broadcasted_iota(jnp.int32, shape, 1) // STATE
    for j in range(STATE_COLS // LANES):
        own = row_group == 2 * j + half
        for src, c0 in ((re_ref, 0), (im_ref, STATE_COLS)):
            dst[:, c0 + j * LANES:c0 + (j + 1) * LANES] = jnp.where(own, src[0], 0.0).astype(BF16)


def _ssm_state_kernel(u_ref, vr_ref, vi_ref, sr_ref, si_ref, v_scr):
    @pl.when(pl.program_id(1) == 0)
    def _():
        _expand_block_diagonal(vr_ref, vi_ref, v_scr)

    s = jnp.dot(u_ref[...], v_scr[...], preferred_element_type=F32)
    sr_ref[...] = s[:, :STATE_COLS]
    si_ref[...] = s[:, STATE_COLS:]


def _ssm_state(u2, v_re, v_im):
    n = u2.shape[0]
    state = pl.BlockSpec((SSM_TILE, STATE_COLS), lambda gb, i: (i, gb))
    compact = pl.BlockSpec((1, SLAB, 2 * STATE), lambda gb, i: (gb, 0, 0))
    return pl.pallas_call(
        _ssm_state_kernel,
        out_shape=[jax.ShapeDtypeStruct((n, N_GROUP_BLOCKS * STATE_COLS), F32)] * 2,
        grid=(N_GROUP_BLOCKS, n // SSM_TILE),
        in_specs=[pl.BlockSpec((SSM_TILE, SLAB), lambda gb, i: (i, gb)), compact, compact],
        out_specs=[state, state],
        scratch_shapes=[pltpu.VMEM((SLAB, 2 * STATE_COLS), BF16)],
        compiler_params=pltpu.CompilerParams(dimension_semantics=("arbitrary", "arbitrary"),
                                             vmem_limit_bytes=VMEM_LIMIT),
        name="ssm_state",
    )(u2, v_re, v_im)


def _ssm_scan_kernel(sr_ref, si_ref, ar_ref, ai_ref, hr_ref, hi_ref, cr_scr, ci_scr, s_scr, h_scr):
    @pl.when(pl.program_id(0) == 0)
    def _():
        cr_scr[...] = jnp.zeros_like(cr_scr)
        ci_scr[...] = jnp.zeros_like(ci_scr)

    ar, ai = ar_ref[...], ai_ref[...]
    nb = sr_ref.shape[0]
    tiles = ar.shape[0]
    for b in range(nb):
        for part, ref in enumerate((sr_ref, si_ref)):
            for r in range(tiles):
                s_scr[part, b, pl.ds(r, SCAN_TILE, stride=SCAN_PITCH), :] = (
                    ref[b, :, r * LANES:(r + 1) * LANES])

    def body(k, carry):
        rows = pl.ds(pl.multiple_of(k * SCAN_PITCH, 8), tiles)
        out = []
        for b in range(nb):
            hr, hi = carry[2 * b], carry[2 * b + 1]
            h_scr[0, b, rows, :] = hr
            h_scr[1, b, rows, :] = hi
            out += [ar * hr - ai * hi + s_scr[0, b, rows, :], ar * hi + ai * hr + s_scr[1, b, rows, :]]
        return tuple(out)

    init = tuple(scr[b] for b in range(nb) for scr in (cr_scr, ci_scr))
    fin = lax.fori_loop(0, SCAN_TILE, body, init, unroll=8)
    for b in range(nb):
        cr_scr[b] = fin[2 * b]
        ci_scr[b] = fin[2 * b + 1]
        for part, ref in enumerate((hr_ref, hi_ref)):
            for r in range(tiles):
                ref[b, :, r * LANES:(r + 1) * LANES] = (
                    h_scr[part, b, pl.ds(r, SCAN_TILE, stride=SCAN_PITCH), :].astype(BF16))


def _ssm_scan(s_re, s_im, a_re, a_im):
    bsz, k, cols = s_re.shape
    rows, lanes = a_re.shape
    blk = pl.BlockSpec((bsz, SCAN_TILE, cols), lambda i: (0, i, 0))
    return pl.pallas_call(
        _ssm_scan_kernel,
        out_shape=[jax.ShapeDtypeStruct(s_re.shape, BF16)] * 2,
        grid=(k // SCAN_TILE,),
        in_specs=[blk, blk, _resident((rows, lanes)), _resident((rows, lanes))],
        out_specs=[blk, blk],
        scratch_shapes=[pltpu.VMEM((bsz, rows, lanes), F32)] * 2
        + [pltpu.VMEM((2, bsz, SCAN_TILE * SCAN_PITCH, lanes), F32)] * 2,
        compiler_params=pltpu.CompilerParams(dimension_semantics=("arbitrary",)),
        name="ssm_scan",
    )(s_re, s_im, a_re, a_im)


def _ssm_out_kernel(u_ref, hr_ref, hi_ref, bd_ref, wr_ref, wi_ref, y_ref, t_scr, wt_scr):
    @pl.when(pl.program_id(1) == 0)
    def _():
        bd = bd_ref[0].astype(BF16)
        for s in range(SSM_CHUNK):
            rows = slice(s * LANES, (s + 1) * LANES)
            if s:
                t_scr[rows, :s * LANES] = jnp.zeros((LANES, s * LANES), BF16)
            t_scr[rows, s * LANES:] = bd[:, :(SSM_CHUNK - s) * LANES]
        _expand_block_diagonal(wr_ref, wi_ref, wt_scr)

    pair = 2 * LANES
    contract_last = (((1,), (1,)), ((), ()))
    for tb in range(SLAB // pair):
        k_rows = (tb + 1) * pair
        cols = slice(tb * pair, (tb + 1) * pair)
        y = jnp.dot(u_ref[:, :k_rows], t_scr[:k_rows, cols], preferred_element_type=F32)
        y += lax.dot_general(hr_ref[...], wt_scr[cols, :STATE_COLS], contract_last,
                             preferred_element_type=F32)
        y += lax.dot_general(hi_ref[...], wt_scr[cols, STATE_COLS:], contract_last,
                             preferred_element_type=F32)
        y_ref[:, cols] = y


def _ssm_out(u2, h_re, h_im, bd, w_re, w_im):
    n = u2.shape[0]
    state = pl.BlockSpec((SSM_TILE, STATE_COLS), lambda gb, i: (i, gb))
    slab = pl.BlockSpec((SSM_TILE, SLAB), lambda gb, i: (i, gb))
    compact = pl.BlockSpec((1, SLAB, 2 * STATE), lambda gb, i: (gb, 0, 0))
    return pl.pallas_call(
        _ssm_out_kernel,
        out_shape=jax.ShapeDtypeStruct(u2.shape, F32),
        grid=(N_GROUP_BLOCKS, n // SSM_TILE),
        in_specs=[slab, state, state,
                  pl.BlockSpec((1, LANES, SLAB), lambda gb, i: (gb, 0, 0)), compact, compact],
        out_specs=slab,
        scratch_shapes=[pltpu.VMEM((SLAB, SLAB), BF16), pltpu.VMEM((SLAB, 2 * STATE_COLS), BF16)],
        compiler_params=pltpu.CompilerParams(dimension_semantics=("arbitrary", "arbitrary"),
                                             vmem_limit_bytes=VMEM_LIMIT),
        name="ssm_out",
    )(u2, h_re, h_im, bd, w_re, w_im)


def _s5(u2, prep, bsz):
    bd, v_re, v_im, w_re, w_im = prep[:5]
    a16 = prep[5].reshape(N_SSM_GROUPS, SSM_GROUP, 2 * STATE)[:, 0]
    n = u2.shape[0]
    kc = n // bsz
    s_re, s_im = _ssm_state(u2, v_re, v_im)
    tile_shape = (N_SSM_GROUPS * STATE // LANES, LANES)
    a_re = a16[:, :STATE].reshape(tile_shape)
    a_im = a16[:, STATE:].reshape(tile_shape)
    h_re, h_im = _ssm_scan(s_re.reshape(bsz, kc, -1), s_im.reshape(bsz, kc, -1), a_re, a_im)
    return _ssm_out(u2, h_re.reshape(n, -1), h_im.reshape(n, -1), bd, w_re, w_im)


def _swap_lane_halves(x):
    packed = pltpu.bitcast(x, jnp.uint32)
    return pltpu.bitcast(pltpu.roll(packed, HEAD_DIM, axis=1), BF16)


def _attn_out_kernel(*refs, n_tiles, tiles_per_batch):
    step = pl.program_id(0)
    body = functools.partial(_attn_out_step, *refs, tiles_per_batch=tiles_per_batch)
    pl.when(step == 0)(functools.partial(body, do_attn=True, do_proj=False))
    pl.when((step > 0) & (step < n_tiles))(functools.partial(body, do_attn=True, do_proj=True))
    pl.when(step == n_tiles)(functools.partial(body, do_attn=False, do_proj=True))


def _attn_out_step(sink_ref, q_ref, za_ref, kv_ref, kvp_ref, y2_ref, zs_ref,
                   x_ref, gate_ref, ag_ref, gw_ref, gb_ref, sg_ref, wo_ref, fg_ref,
                   o_ref, kv_scr, attn_scr, y_scr, mix_scr, out_scr,
                   *, tiles_per_batch, do_attn, do_proj):
    step = pl.program_id(0)
    cur_slot = step % 2
    prev_slot = 1 - cur_slot
    first = step % tiles_per_batch == 0
    if do_attn:
        keep = jnp.where(first, 0.0, 1.0).astype(BF16)
        kv_scr[0:BLOCK, :] = kvp_ref[...] * keep
        kv_scr[BLOCK:, :] = kv_ref[...]

    older = (lax.broadcasted_iota(jnp.int32, (BLOCK, BLOCK), 1)
             > lax.broadcasted_iota(jnp.int32, (BLOCK, BLOCK), 0))
    kv_half = lax.broadcasted_iota(jnp.int32, (2 * BLOCK, LANES), 1) // HEAD_DIM
    pairs = Q_PER_KV // 2

    def both_halves(r0, col):
        tile, half = col // LANES, (col % LANES) // HEAD_DIM
        band = kv_scr[pl.ds(r0, 2 * BLOCK), tile * LANES:(tile + 1) * LANES]
        own = jnp.where(kv_half == half, band, jnp.zeros_like(band))
        other = _swap_lane_halves(own)
        return (own, other) if half == 0 else (other, own)

    contract_last = (((1,), (1,)), ((), ()))
    for gb in range(N_GROUP_BLOCKS if do_attn else 0):
        for tok in range(SSM_CHUNK):
            c0 = gb * SLAB + tok * LANES
            y_scr[gb, pl.ds(tok, CHUNK_ROWS, stride=SSM_CHUNK), :] = y2_ref[:, c0:c0 + LANES]

    heads = range(N_KV_HEADS)
    head_q = lambda h, p: h * Q_PER_KV * HEAD_DIM + p * PAIR

    def scores(j):
        r0 = j * BLOCK
        k_pars = [both_halves(r0, h * HEAD_DIM) for h in heads]
        qss = [jnp.concatenate(
            [q_ref[pl.ds(r0, BLOCK), head_q(h, p):head_q(h, p + 1)] for p in range(pairs)], axis=0)
            for h in heads]
        return {(h, par): lax.dot_general(qss[h], k_pars[h][par], contract_last,
                                          preferred_element_type=F32)
                for h in heads for par in range(2)}

    def softmax(j, s_alls):
        no_past = jnp.where(first, NEG, 0.0) if j == 0 else None
        probs, invs = {}, {}
        for h in heads:
            for par in range(2):
                s_all = s_alls[h, par]
                pr, iv = [], []
                for p in range(pairs):
                    past = s_all[p * BLOCK:(p + 1) * BLOCK, :BLOCK]
                    if no_past is not None:
                        past = past + no_past
                    s = jnp.where(older, past, s_all[p * BLOCK:(p + 1) * BLOCK, BLOCK:])
                    sink = sink_ref[h * Q_PER_KV + 2 * p + par] * LOG2_E
                    m = jnp.max(s, axis=-1, keepdims=True)
                    e = jnp.exp2(s - m)
                    l = jnp.sum(e, axis=-1, keepdims=True) + jnp.exp2(sink - m)
                    iv.append(1.0 / l)
                    e = e.astype(BF16)
                    zero = jnp.zeros_like(e)
                    pr.append(jnp.concatenate(
                        [jnp.where(older, e, zero), jnp.where(older, zero, e)], axis=1))
                probs[h, par], invs[h, par] = jnp.concatenate(pr, axis=0), iv
        return probs, invs

    def finish(j, probs, invs):
        r0 = j * BLOCK
        rows = pl.ds(r0, BLOCK)
        v_pars = [both_halves(r0, KV_WIDTH + h * HEAD_DIM) for h in heads]
        pvs = {(h, par): jnp.dot(probs[h, par], v_pars[h][par], preferred_element_type=F32)
               for h in heads for par in range(2)}
        for h in heads:
            for p in range(pairs):
                blk = slice(p * BLOCK, (p + 1) * BLOCK)
                attn_scr[:, head_q(h, p):head_q(h, p + 1)] = (
                    pvs[h, 0][blk] * invs[h, 0][p] + pvs[h, 1][blk] * invs[h, 1][p])
        a = attn_scr[...]
        a = a * lax.rsqrt(jnp.mean(a * a, axis=-1, keepdims=True) + EPS) * ag_ref[...]
        za = za_ref[rows, :]
        mix_scr[cur_slot, rows, :ATTN_WIDTH] = a.astype(BF16) * (za * jax.nn.sigmoid(za))
        y = jnp.concatenate([y_scr[gb, rows, :] for gb in range(N_GROUP_BLOCKS)], axis=1)
        y = 0.5 * y * (1.0 + lax.erf(y * math.sqrt(0.5)))
        glu = jnp.dot(y.astype(BF16), gw_ref[...], preferred_element_type=F32) + gb_ref[...]
        y = y * jax.nn.sigmoid(glu)
        y = y * lax.rsqrt(jnp.mean(y * y, axis=-1, keepdims=True) + EPS) * sg_ref[...]
        zs = zs_ref[rows, :]
        mix_scr[cur_slot, rows, ATTN_WIDTH:] = y.astype(BF16) * (zs * jax.nn.sigmoid(zs))

    n_blocks = q_ref.shape[0] // BLOCK
    s_next = scores(0) if do_attn else None
    for j in range(n_blocks):
        if do_attn:
            probs, invs = softmax(j, s_next)
            s_next = scores(j + 1) if j + 1 < n_blocks else None
        if do_proj:
            out_scr[j] = jnp.dot(mix_scr[prev_slot], wo_ref[:, j * OUT_CHUNK:(j + 1) * OUT_CHUNK],
                                 preferred_element_type=F32)
        if do_attn:
            finish(j, probs, invs)

    if do_proj:
        out = jnp.concatenate([out_scr[c] for c in range(out_scr.shape[0])], axis=1)
        res = x_ref[...] + gate_ref[0] * out
        o_ref[...] = (res * lax.rsqrt(jnp.mean(res * res, axis=-1, keepdims=True) + EPS)
                      * fg_ref[...])


def _attn_out(sinks, q, za, kv, y2, zs, x2, gate, attn_gain, glu_w, glu_b, ssm_gain, w_out,
              final_gain, rows_per_batch):
    t, d = x2.shape
    n_tiles = t // ROW_TILE
    tiles_per_batch = rows_per_batch // ROW_TILE
    blocks_per_tile = ROW_TILE // BLOCK
    cur = lambda i: jnp.minimum(i, n_tiles - 1)
    old = lambda i: jnp.maximum(i - 1, 0)
    row = lambda w: pl.BlockSpec((ROW_TILE, w), lambda i: (cur(i), 0))
    prev_block = pl.BlockSpec(
        (BLOCK, 2 * KV_WIDTH), lambda i: (jnp.maximum(cur(i) * blocks_per_tile - 1, 0), 0))
    out_chunks = d // OUT_CHUNK
    return pl.pallas_call(
        functools.partial(_attn_out_kernel, n_tiles=n_tiles, tiles_per_batch=tiles_per_batch),
        out_shape=jax.ShapeDtypeStruct((t, d), F32),
        grid=(n_tiles + 1,),
        in_specs=[pl.BlockSpec(memory_space=pltpu.SMEM),
                  row(ATTN_WIDTH), row(ATTN_WIDTH), row(2 * KV_WIDTH), prev_block,
                  pl.BlockSpec((CHUNK_ROWS, y2.shape[1]), lambda i: (cur(i), 0)),
                  row(SSM_WIDTH),
                  pl.BlockSpec((ROW_TILE, d), lambda i: (old(i), 0)),
                  pl.BlockSpec((1, 1, d), lambda i: (old(i) // tiles_per_batch, 0, 0)),
                  _resident((1, ATTN_WIDTH)),
                  _resident((SSM_WIDTH, SSM_WIDTH)), _resident((1, SSM_WIDTH)),
                  _resident((1, SSM_WIDTH)),
                  _resident((d, d)), _resident((1, d))],
        out_specs=pl.BlockSpec((ROW_TILE, d), lambda i: (old(i), 0)),
        scratch_shapes=[pltpu.VMEM((ROW_TILE + BLOCK, 2 * KV_WIDTH), BF16),
                        pltpu.VMEM((BLOCK, ATTN_WIDTH), F32),
                        pltpu.VMEM((N_GROUP_BLOCKS, ROW_TILE, LANES), F32),
                        pltpu.VMEM((2, ROW_TILE, d), BF16),
                        pltpu.VMEM((out_chunks, ROW_TILE, OUT_CHUNK), F32)],
        compiler_params=pltpu.CompilerParams(dimension_semantics=("arbitrary",),
                                             vmem_limit_bytes=60 * 1024 * 1024),
        name="attn_out",
    )(sinks, q, za, kv, kv, y2, zs, x2, gate, attn_gain, glu_w, glu_b, ssm_gain, w_out, final_gain)


def kernel(x, c, w_ada, b_ada, norm_gain, w_in, b_in, attn_sinks, attn_out_gain, ssm_lambda_re,
           ssm_lambda_im, ssm_log_step, ssm_b_re, ssm_b_im, ssm_c_re, ssm_c_im, ssm_d, glu_w, glu_b,
           ssm_out_gain, w_out, final_gain):
    bsz, seq, d = x.shape
    assert w_ada.shape[0] == 1, "single-layer trunk only"
    assert d == D_MODEL and w_in.shape[2] == 2 * (ATTN_WIDTH + KV_WIDTH + SSM_WIDTH)
    assert seq % ROW_TILE == 0 and (bsz * seq // SSM_CHUNK) % SSM_TILE == 0
    assert (seq // SSM_CHUNK) % SCAN_TILE == 0
    x2 = x.reshape(bsz * seq, d)
    mod, w_in16 = _ada(c, w_ada[0], b_ada[0], w_in[0])
    shift, scale, gate = (m.reshape(bsz, 1, d) for m in jnp.split(mod, 3, axis=-1))
    q, kv, za, zs, u2, w_out16 = _inproj(x2, scale, shift, norm_gain[0].reshape(1, d),
                                         w_in16, b_in[0].reshape(1, -1), w_out[0], seq)
    prep = _ssm_prep(ssm_lambda_re[0], ssm_lambda_im[0], ssm_log_step[0], ssm_b_re[0],
                     ssm_b_im[0], ssm_c_re[0], ssm_c_im[0], ssm_d[0])
    y2 = _s5(u2, prep, bsz)
    out = _attn_out(attn_sinks[0], q, za, kv, y2, zs, x2, gate,
                    attn_out_gain[0].reshape(1, -1), glu_w[0].astype(BF16),
                    glu_b[0].reshape(1, -1), ssm_out_gain[0].reshape(1, -1),
                    w_out16, final_gain.reshape(1, d), seq)
    return out.reshape(bsz, seq, d)
```

```python
import functools
import math

import jax
import jax.numpy as jnp
from jax import lax
from jax.experimental import pallas as pl
from jax.experimental.pallas import tpu as pltpu

F32 = jnp.float32
BF16 = jnp.bfloat16

D_MODEL = 2048
HEAD_DIM = 64
N_Q_HEADS = 24
N_KV_HEADS = 3
Q_PER_KV = 8
ATTN_WIDTH = N_Q_HEADS * HEAD_DIM
BLOCK = 128
SSM_GROUP = 16
SSM_WIDTH = D_MODEL - ATTN_WIDTH
N_SSM_GROUPS = SSM_WIDTH // SSM_GROUP
STATE = 64
EPS = 1e-5
NEG = -1e30
LOG2_E = math.log2(math.e)

LANES = 128
SSM_CHUNK = 16
GROUPS_PER_BLOCK = LANES // SSM_GROUP
N_GROUP_BLOCKS = N_SSM_GROUPS // GROUPS_PER_BLOCK
SLAB = SSM_CHUNK * LANES
STATE_COLS = GROUPS_PER_BLOCK * STATE
KV_WIDTH = N_KV_HEADS * HEAD_DIM
PAIR = 2 * HEAD_DIM
Q_SCALE = LOG2_E * HEAD_DIM ** -0.5
PROJ_COLS = 512

VMEM_LIMIT = 56 * 1024 * 1024

ROW_TILE = 512
CHUNK_ROWS = ROW_TILE // SSM_CHUNK
ADA_COLS = 768
OUT_CHUNK = D_MODEL // (ROW_TILE // BLOCK)
SSM_TILE = 1024
SCAN_TILE = 128
SCAN_PITCH = 24


def _resident(shape):
    return pl.BlockSpec(shape, lambda *_: (0,) * len(shape), pipeline_mode=pl.Buffered(1))


def _ada_kernel(ct_ref, w_ref, b_ref, win_ref, o_ref, win16_ref):
    win16_ref[...] = win_ref[...].astype(BF16)
    ct = ct_ref[...]
    s = ct * jax.nn.sigmoid(ct)
    w = w_ref[...]
    rows = [jnp.sum(s[:, b:b + 1] * w, axis=0, keepdims=True) for b in range(ct.shape[1])]
    o_ref[...] = jnp.concatenate(rows, axis=0) + b_ref[...]


def _ada(c, w_ada, b_ada, w_in):
    bsz, d = c.shape
    n = w_ada.shape[1]
    steps = n // ADA_COLS
    slab = pl.BlockSpec((w_in.shape[0] // steps, w_in.shape[1]), lambda i: (i, 0))
    return pl.pallas_call(
        _ada_kernel,
        out_shape=[jax.ShapeDtypeStruct((bsz, n), F32), jax.ShapeDtypeStruct(w_in.shape, BF16)],
        grid=(steps,),
        in_specs=[pl.BlockSpec((d, bsz), lambda i: (0, 0)),
                  pl.BlockSpec((d, ADA_COLS), lambda i: (0, i)),
                  pl.BlockSpec((1, ADA_COLS), lambda i: (0, i)),
                  slab],
        out_specs=[pl.BlockSpec((bsz, ADA_COLS), lambda i: (0, i)), slab],
        compiler_params=pltpu.CompilerParams(dimension_semantics=("arbitrary",),
                                             vmem_limit_bytes=VMEM_LIMIT),
        name="ada",
    )(c.T, w_ada, b_ada.reshape(1, n), w_in)


def _inproj_kernel(x_ref, sc_ref, sh_ref, g_ref, w_ref, b_ref, wo_ref,
                   q_ref, kv_ref, za_ref, zs_ref, u2_ref, wo16_ref, h_scr, u_scr):
    wo16_ref[...] = wo_ref[...].astype(BF16)
    x = x_ref[...]
    var = jnp.mean(x * x, axis=-1, keepdims=True)
    gain = g_ref[...] * (1.0 + sc_ref[0])
    h_scr[...] = (x * lax.rsqrt(var + EPS) * gain + sh_ref[0]).astype(BF16)

    def project(col, width):
        acc = jnp.dot(h_scr[...], w_ref[:, col:col + width], preferred_element_type=F32)
        return acc + b_ref[:, col:col + width]

    col_kv, col_za = ATTN_WIDTH, ATTN_WIDTH + 2 * KV_WIDTH
    col_u = col_za + ATTN_WIDTH
    col_zs = col_u + SSM_WIDTH
    for c0 in range(0, ATTN_WIDTH, PROJ_COLS):
        q_ref[:, c0:c0 + PROJ_COLS] = (project(c0, PROJ_COLS) * Q_SCALE).astype(BF16)
        za_ref[:, c0:c0 + PROJ_COLS] = project(col_za + c0, PROJ_COLS).astype(BF16)
    kv_ref[...] = project(col_kv, 2 * KV_WIDTH).astype(BF16)
    zs_ref[...] = project(col_zs, SSM_WIDTH).astype(BF16)
    u = project(col_u, SSM_WIDTH)
    for gb in range(N_GROUP_BLOCKS):
        for n in range(CHUNK_ROWS):
            u_scr[gb, n * SCAN_PITCH:n * SCAN_PITCH + SSM_CHUNK, :] = (
                u[n * SSM_CHUNK:(n + 1) * SSM_CHUNK, gb * LANES:(gb + 1) * LANES])
        for tok in range(SSM_CHUNK):
            c0 = gb * SLAB + tok * LANES
            u2_ref[:, c0:c0 + LANES] = (
                u_scr[gb, pl.ds(tok, CHUNK_ROWS, stride=SCAN_PITCH), :].astype(BF16))


def _inproj(x2, scale, shift, gain, w_cat, b_cat, w_out, rows_per_batch):
    t, d = x2.shape
    widths = (ATTN_WIDTH, 2 * KV_WIDTH, ATTN_WIDTH, SSM_WIDTH)
    n_all = w_cat.shape[1]
    n_tiles = t // ROW_TILE
    tiles_per_batch = rows_per_batch // ROW_TILE
    mod_spec = pl.BlockSpec((1, 1, d), lambda i: (i // tiles_per_batch, 0, 0))
    u2_cols = N_GROUP_BLOCKS * SLAB
    wo_spec = pl.BlockSpec((w_out.shape[0] // n_tiles, w_out.shape[1]), lambda i: (i, 0))
    return pl.pallas_call(
        _inproj_kernel,
        out_shape=[jax.ShapeDtypeStruct((t, w), BF16) for w in widths]
        + [jax.ShapeDtypeStruct((t // SSM_CHUNK, u2_cols), BF16),
           jax.ShapeDtypeStruct(w_out.shape, BF16)],
        grid=(n_tiles,),
        in_specs=[pl.BlockSpec((ROW_TILE, d), lambda i: (i, 0)),
                  mod_spec, mod_spec,
                  _resident((1, d)),
                  _resident((d, n_all)),
                  _resident((1, n_all)),
                  wo_spec],
        out_specs=[pl.BlockSpec((ROW_TILE, w), lambda i: (i, 0)) for w in widths]
        + [pl.BlockSpec((CHUNK_ROWS, u2_cols), lambda i: (i, 0)), wo_spec],
        scratch_shapes=[pltpu.VMEM((ROW_TILE, d), BF16),
                        pltpu.VMEM((N_GROUP_BLOCKS, CHUNK_ROWS * SCAN_PITCH, LANES), F32)],
        compiler_params=pltpu.CompilerParams(dimension_semantics=("arbitrary",),
                                             vmem_limit_bytes=VMEM_LIMIT),
        name="inproj",
    )(x2, scale, shift, gain, w_cat, b_cat, w_out)


def _ssm_prep_kernel(lr_ref, li_ref, ls_ref, br_ref, bi_ref, cr_ref, ci_ref, d_ref,
                     bd_ref, vr_ref, vi_ref, wr_ref, wi_ref, a16_ref, y_scr):
    lr, li = lr_ref[...], li_ref[...]
    step = jnp.exp(ls_ref[...])
    decay = jnp.exp(lr * step)
    ar, ai = decay * jnp.cos(li * step), decay * jnp.sin(li * step)
    den = lr * lr + li * li
    nr, ni = ar - 1.0, ai
    coef_re = (nr * lr + ni * li) / den
    coef_im = (ni * lr - nr * li) / den
    br, bi = br_ref[...], bi_ref[...]
    bbar_re = coef_re * br - coef_im * bi
    bbar_im = coef_re * bi + coef_im * br
    cr, ci = cr_ref[...], ci_ref[...]
    lo = lax.broadcasted_iota(jnp.int32, lr.shape, 2) < STATE
    pr, pi = jnp.ones_like(ar), jnp.zeros_like(ai)
    for tau in range(SSM_CHUNK + 1):
        er, ei = pr * cr - pi * ci, pr * ci + pi * cr
        if tau < SSM_CHUNK:
            y_scr[:, tau * LANES:(tau + 1) * LANES, :] = jnp.where(lo, er, ei)
            s = SSM_CHUNK - 1 - tau
            vr_ref[:, s * LANES:(s + 1) * LANES, :] = pr * bbar_re - pi * bbar_im
            vi_ref[:, s * LANES:(s + 1) * LANES, :] = pr * bbar_im + pi * bbar_re
        if tau >= 1:
            wr_ref[:, (tau - 1) * LANES:tau * LANES, :] = er
            wi_ref[:, (tau - 1) * LANES:tau * LANES, :] = -ei
        if tau == SSM_CHUNK:
            a16_ref[...] = jnp.where(lo, pr, pi)
        pr, pi = pr * ar - pi * ai, pr * ai + pi * ar
    x = jnp.where(lo, bbar_re, -bbar_im)
    row = lax.broadcasted_iota(jnp.int32, (LANES, SLAB), 0)
    col = lax.broadcasted_iota(jnp.int32, (LANES, SLAB), 1)
    same_group = row // SSM_GROUP == (col % LANES) // SSM_GROUP
    diagonal = (lax.broadcasted_iota(jnp.int32, (LANES, LANES), 0)
                == lax.broadcasted_iota(jnp.int32, (LANES, LANES), 1))
    for gb in range(N_GROUP_BLOCKS):
        k = lax.dot_general(x[gb], y_scr[gb], (((1,), (1,)), ((), ())),
                            precision=lax.Precision.HIGHEST, preferred_element_type=F32)
        k = jnp.where(same_group, k, 0.0)
        bd_ref[gb, :, LANES:] = k[:, LANES:]
        bd_ref[gb, :, :LANES] = k[:, :LANES] + jnp.where(diagonal, d_ref[gb], 0.0)


def _ssm_prep(lam_re, lam_im, log_step, b_re, b_im, c_re, c_im, d_skip):
    nb = N_GROUP_BLOCKS
    rows = lambda a: jnp.concatenate([a, a], axis=-1).reshape(nb, LANES, 2 * STATE)
    per_group = lambda a: rows(jnp.repeat(a[:, None, :], SSM_GROUP, axis=1))
    compact = jax.ShapeDtypeStruct((nb, SLAB, 2 * STATE), F32)
    return pl.pallas_call(
        _ssm_prep_kernel,
        out_shape=[jax.ShapeDtypeStruct((nb, LANES, SLAB), F32),
                   compact, compact,
                   compact, compact,
                   jax.ShapeDtypeStruct((nb, LANES, 2 * STATE), F32)],
        scratch_shapes=[pltpu.VMEM((nb, SLAB, 2 * STATE), F32)],
        compiler_params=pltpu.CompilerParams(vmem_limit_bytes=VMEM_LIMIT),
        name="ssm_prep",
    )(per_group(lam_re), per_group(lam_im),
      jnp.broadcast_to(jnp.repeat(log_step, SSM_GROUP).reshape(nb, LANES, 1), (nb, LANES, 2 * STATE)),
      rows(jnp.swapaxes(b_re, 1, 2)), rows(jnp.swapaxes(b_im, 1, 2)), rows(c_re), rows(c_im),
      jnp.broadcast_to(d_skip.reshape(nb, 1, LANES), (nb, LANES, LANES)))


def _expand_block_diagonal(re_ref, im_ref, dst):
    shape = (SLAB, 2 * STATE)
    row_group = (lax.broadcasted_iota(jnp.int32, shape, 0) % LANES) // SSM_GROUP
    half = lax.broadcasted_iota(jnp.int32, shape, 1) // STATE
    for j in range(STATE_COLS // LANES):
        own = row_group == 2 * j + half
        for src, c0 in ((re_ref, 0), (im_ref, STATE_COLS)):
            dst[:, c0 + j * LANES:c0 + (j + 1) * LANES] = jnp.where(own, src[0], 0.0).astype(BF16)


def _ssm_state_kernel(u_ref, vr_ref, vi_ref, sr_ref, si_ref, v_scr):
    @pl.when(pl.program_id(1) == 0)
    def _():
        _expand_block_diagonal(vr_ref, vi_ref, v_scr)

    s = jnp.dot(u_ref[...], v_scr[...], preferred_element_type=F32)
    sr_ref[...] = s[:, :STATE_COLS]
    si_ref[...] = s[:, STATE_COLS:]


def _ssm_state(u2, v_re, v_im):
    n = u2.shape[0]
    state = pl.BlockSpec((SSM_TILE, STATE_COLS), lambda gb, i: (i, gb))
    compact = pl.BlockSpec((1, SLAB, 2 * STATE), lambda gb, i: (gb, 0, 0))
    return pl.pallas_call(
        _ssm_state_kernel,
        out_shape=[jax.ShapeDtypeStruct((n, N_GROUP_BLOCKS * STATE_COLS), F32)] * 2,
        grid=(N_GROUP_BLOCKS, n // SSM_TILE),
        in_specs=[pl.BlockSpec((SSM_TILE, SLAB), lambda gb, i: (i, gb)), compact, compact],
        out_specs=[state, state],
        scratch_shapes=[pltpu.VMEM((SLAB, 2 * STATE_COLS), BF16)],
        compiler_params=pltpu.CompilerParams(dimension_semantics=("arbitrary", "arbitrary"),
                                             vmem_limit_bytes=VMEM_LIMIT),
        name="ssm_state",
    )(u2, v_re, v_im)


def _ssm_scan_kernel(sr_ref, si_ref, ar_ref, ai_ref, hr_ref, hi_ref, cr_scr, ci_scr, s_scr, h_scr):
    @pl.when(pl.program_id(0) == 0)
    def _():
        cr_scr[...] = jnp.zeros_like(cr_scr)
        ci_scr[...] = jnp.zeros_like(ci_scr)

    ar, ai = ar_ref[...], ai_ref[...]
    nb = sr_ref.shape[0]
    tiles = ar.shape[0]
    for b in range(nb):
        for part, ref in enumerate((sr_ref, si_ref)):
            for r in range(tiles):
                s_scr[part, b, pl.ds(r, SCAN_TILE, stride=SCAN_PITCH), :] = (
                    ref[b, :, r * LANES:(r + 1) * LANES])

    def body(k, carry):
        rows = pl.ds(pl.multiple_of(k * SCAN_PITCH, 8), tiles)
        out = []
        for b in range(nb):
            hr, hi = carry[2 * b], carry[2 * b + 1]
            h_scr[0, b, rows, :] = hr
            h_scr[1, b, rows, :] = hi
            out += [ar * hr - ai * hi + s_scr[0, b, rows, :], ar * hi + ai * hr + s_scr[1, b, rows, :]]
        return tuple(out)

    init = tuple(scr[b] for b in range(nb) for scr in (cr_scr, ci_scr))
    fin = lax.fori_loop(0, SCAN_TILE, body, init, unroll=8)
    for b in range(nb):
        cr_scr[b] = fin[2 * b]
        ci_scr[b] = fin[2 * b + 1]
        for part, ref in enumerate((hr_ref, hi_ref)):
            for r in range(tiles):
                ref[b, :, r * LANES:(r + 1) * LANES] = (
                    h_scr[part, b, pl.ds(r, SCAN_TILE, stride=SCAN_PITCH), :].astype(BF16))


def _ssm_scan(s_re, s_im, a_re, a_im):
    bsz, k, cols = s_re.shape
    rows, lanes = a_re.shape
    blk = pl.BlockSpec((bsz, SCAN_TILE, cols), lambda i: (0, i, 0))
    return pl.pallas_call(
        _ssm_scan_kernel,
        out_shape=[jax.ShapeDtypeStruct(s_re.shape, BF16)] * 2,
        grid=(k // SCAN_TILE,),
        in_specs=[blk, blk, _resident((rows, lanes)), _resident((rows, lanes))],
        out_specs=[blk, blk],
        scratch_shapes=[pltpu.VMEM((bsz, rows, lanes), F32)] * 2
        + [pltpu.VMEM((2, bsz, SCAN_TILE * SCAN_PITCH, lanes), F32)] * 2,
        compiler_params=pltpu.CompilerParams(dimension_semantics=("arbitrary",)),
        name="ssm_scan",
    )(s_re, s_im, a_re, a_im)


def _ssm_out_kernel(u_ref, hr_ref, hi_ref, bd_ref, wr_ref, wi_ref, y_ref, t_scr, wt_scr):
    @pl.when(pl.program_id(1) == 0)
    def _():
        bd = bd_ref[0].astype(BF16)
        for s in range(SSM_CHUNK):
            rows = slice(s * LANES, (s + 1) * LANES)
            if s:
                t_scr[rows, :s * LANES] = jnp.zeros((LANES, s * LANES), BF16)
            t_scr[rows, s * LANES:] = bd[:, :(SSM_CHUNK - s) * LANES]
        _expand_block_diagonal(wr_ref, wi_ref, wt_scr)

    pair = 2 * LANES
    contract_last = (((1,), (1,)), ((), ()))
    for tb in range(SLAB // pair):
        k_rows = (tb + 1) * pair
        cols = slice(tb * pair, (tb + 1) * pair)
        y = jnp.dot(u_ref[:, :k_rows], t_scr[:k_rows, cols], preferred_element_type=F32)
        y += lax.dot_general(hr_ref[...], wt_scr[cols, :STATE_COLS], contract_last,
                             preferred_element_type=F32)
        y += lax.dot_general(hi_ref[...], wt_scr[cols, STATE_COLS:], contract_last,
                             preferred_element_type=F32)
        y_ref[:, cols] = y


def _ssm_out(u2, h_re, h_im, bd, w_re, w_im):
    n = u2.shape[0]
    state = pl.BlockSpec((SSM_TILE, STATE_COLS), lambda gb, i: (i, gb))
    slab = pl.BlockSpec((SSM_TILE, SLAB), lambda gb, i: (i, gb))
    compact = pl.BlockSpec((1, SLAB, 2 * STATE), lambda gb, i: (gb, 0, 0))
    return pl.pallas_call(
        _ssm_out_kernel,
        out_shape=jax.ShapeDtypeStruct(u2.shape, F32),
        grid=(N_GROUP_BLOCKS, n // SSM_TILE),
        in_specs=[slab, state, state,
                  pl.BlockSpec((1, LANES, SLAB), lambda gb, i: (gb, 0, 0)), compact, compact],
        out_specs=slab,
        scratch_shapes=[pltpu.VMEM((SLAB, SLAB), BF16), pltpu.VMEM((SLAB, 2 * STATE_COLS), BF16)],
        compiler_params=pltpu.CompilerParams(dimension_semantics=("arbitrary", "arbitrary"),
                                             vmem_limit_bytes=VMEM_LIMIT),
        name="ssm_out",
    )(u2, h_re, h_im, bd, w_re, w_im)


def _s5(u2, prep, bsz):
    bd, v_re, v_im, w_re, w_im = prep[:5]
    a16 = prep[5].reshape(N_SSM_GROUPS, SSM_GROUP, 2 * STATE)[:, 0]
    n = u2.shape[0]
    kc = n // bsz
    s_re, s_im = _ssm_state(u2, v_re, v_im)
    tile_shape = (N_SSM_GROUPS * STATE // LANES, LANES)
    a_re = a16[:, :STATE].reshape(tile_shape)
    a_im = a16[:, STATE:].reshape(tile_shape)
    h_re, h_im = _ssm_scan(s_re.reshape(bsz, kc, -1), s_im.reshape(bsz, kc, -1), a_re, a_im)
    return _ssm_out(u2, h_re.reshape(n, -1), h_im.reshape(n, -1), bd, w_re, w_im)


def _swap_lane_halves(x):
    packed = pltpu.bitcast(x, jnp.uint32)
    return pltpu.bitcast(pltpu.roll(packed, HEAD_DIM, axis=1), BF16)


def _attn_out_kernel(*refs, n_tiles, tiles_per_batch):
    step = pl.program_id(0)
    body = functools.partial(_attn_out_step, *refs, tiles_per_batch=tiles_per_batch)
    pl.when(step == 0)(functools.partial(body, do_attn=True, do_proj=False))
    pl.when((step > 0) & (step < n_tiles))(functools.partial(body, do_attn=True, do_proj=True))
    pl.when(step == n_tiles)(functools.partial(body, do_attn=False, do_proj=True))


def _attn_out_step(sink_ref, q_ref, za_ref, kv_ref, kvp_ref, y2_ref, zs_ref,
                   x_ref, gate_ref, ag_ref, gw_ref, gb_ref, sg_ref, wo_ref, fg_ref,
                   o_ref, kv_scr, attn_scr, y_scr, mix_scr, out_scr,
                   *, tiles_per_batch, do_attn, do_proj):
    step = pl.program_id(0)
    cur_slot = step % 2
    prev_slot = 1 - cur_slot
    first = step % tiles_per_batch == 0
    if do_attn:
        keep = jnp.where(first, 0.0, 1.0).astype(BF16)
        kv_scr[0:BLOCK, :] = kvp_ref[...] * keep
        kv_scr[BLOCK:, :] = kv_ref[...]

    older = (lax.broadcasted_iota(jnp.int32, (BLOCK, BLOCK), 1)
             > lax.broadcasted_iota(jnp.int32, (BLOCK, BLOCK), 0))
    kv_half = lax.broadcasted_iota(jnp.int32, (2 * BLOCK, LANES), 1) // HEAD_DIM
    pairs = Q_PER_KV // 2

    def both_halves(r0, col):
        tile, half = col // LANES, (col % LANES) // HEAD_DIM
        band = kv_scr[pl.ds(r0, 2 * BLOCK), tile * LANES:(tile + 1) * LANES]
        own = jnp.where(kv_half == half, band, jnp.zeros_like(band))
        other = _swap_lane_halves(own)
        return (own, other) if half == 0 else (other, own)

    contract_last = (((1,), (1,)), ((), ()))
    for gb in range(N_GROUP_BLOCKS if do_attn else 0):
        for tok in range(SSM_CHUNK):
            c0 = gb * SLAB + tok * LANES
            y_scr[gb, pl.ds(tok, CHUNK_ROWS, stride=SSM_CHUNK), :] = y2_ref[:, c0:c0 + LANES]

    heads = range(N_KV_HEADS)
    head_q = lambda h, p: h * Q_PER_KV * HEAD_DIM + p * PAIR

    def scores(j):
        r0 = j * BLOCK
        k_pars = [both_halves(r0, h * HEAD_DIM) for h in heads]
        qss = [jnp.concatenate(
            [q_ref[pl.ds(r0, BLOCK), head_q(h, p):head_q(h, p + 1)] for p in range(pairs)], axis=0)
            for h in heads]
        return {(h, par): lax.dot_general(qss[h], k_pars[h][par], contract_last,
                                          preferred_element_type=F32)
                for h in heads for par in range(2)}

    def softmax(j, s_alls):
        no_past = jnp.where(first, NEG, 0.0) if j == 0 else None
        probs, invs = {}, {}
        for h in heads:
            for par in range(2):
                s_all = s_alls[h, par]
                pr, iv = [], []
                for p in range(pairs):
                    past = s_all[p * BLOCK:(p + 1) * BLOCK, :BLOCK]
                    if no_past is not None:
                        past = past + no_past
                    s = jnp.where(older, past, s_all[p * BLOCK:(p + 1) * BLOCK, BLOCK:])
                    sink = sink_ref[h * Q_PER_KV + 2 * p + par] * LOG2_E
                    m = jnp.max(s, axis=-1, keepdims=True)
                    e = jnp.exp2(s - m)
                    l = jnp.sum(e, axis=-1, keepdims=True) + jnp.exp2(sink - m)
                    iv.append(1.0 / l)
                    e = e.astype(BF16)
                    zero = jnp.zeros_like(e)
                    pr.append(jnp.concatenate(
                        [jnp.where(older, e, zero), jnp.where(older, zero, e)], axis=1))
                probs[h, par], invs[h, par] = jnp.concatenate(pr, axis=0), iv
        return probs, invs

    def finish(j, probs, invs):
        r0 = j * BLOCK
        rows = pl.ds(r0, BLOCK)
        v_pars = [both_halves(r0, KV_WIDTH + h * HEAD_DIM) for h in heads]
        pvs = {(h, par): jnp.dot(probs[h, par], v_pars[h][par], preferred_element_type=F32)
               for h in heads for par in range(2)}
        for h in heads:
            for p in range(pairs):
                blk = slice(p * BLOCK, (p + 1) * BLOCK)
                attn_scr[:, head_q(h, p):head_q(h, p + 1)] = (
                    pvs[h, 0][blk] * invs[h, 0][p] + pvs[h, 1][blk] * invs[h, 1][p])
        a = attn_scr[...]
        a = a * lax.rsqrt(jnp.mean(a * a, axis=-1, keepdims=True) + EPS) * ag_ref[...]
        za = za_ref[rows, :]
        mix_scr[cur_slot, rows, :ATTN_WIDTH] = a.astype(BF16) * (za * jax.nn.sigmoid(za))
        y = jnp.concatenate([y_scr[gb, rows, :] for gb in range(N_GROUP_BLOCKS)], axis=1)
        y = 0.5 * y * (1.0 + lax.erf(y * math.sqrt(0.5)))
        glu = jnp.dot(y.astype(BF16), gw_ref[...], preferred_element_type=F32) + gb_ref[...]
        y = y * jax.nn.sigmoid(glu)
        y = y * lax.rsqrt(jnp.mean(y * y, axis=-1, keepdims=True) + EPS) * sg_ref[...]
        zs = zs_ref[rows, :]
        mix_scr[cur_slot, rows, ATTN_WIDTH:] = y.astype(BF16) * (zs * jax.nn.sigmoid(zs))

    n_blocks = q_ref.shape[0] // BLOCK
    s_next = scores(0) if do_attn else None
    for j in range(n_blocks):
        if do_attn:
            probs, invs = softmax(j, s_next)
            s_next = scores(j + 1) if j + 1 < n_blocks else None
        if do_proj:
            out_scr[j] = jnp.dot(mix_scr[prev_slot], wo_ref[:, j * OUT_CHUNK:(j + 1) * OUT_CHUNK],
                                 preferred_element_type=F32)
        if do_attn:
            finish(j, probs, invs)

    if do_proj:
        out = jnp.concatenate([out_scr[c] for c in range(out_scr.shape[0])], axis=1)
        res = x_ref[...] + gate_ref[0] * out
        o_ref[...] = (res * lax.rsqrt(jnp.mean(res * res, axis=-1, keepdims=True) + EPS)
                      * fg_ref[...])


def _attn_out(sinks, q, za, kv, y2, zs, x2, gate, attn_gain, glu_w, glu_b, ssm_gain, w_out,
              final_gain, rows_per_batch):
    t, d = x2.shape
    n_tiles = t // ROW_TILE
    tiles_per_batch = rows_per_batch // ROW_TILE
    blocks_per_tile = ROW_TILE // BLOCK
    cur = lambda i: jnp.minimum(i, n_tiles - 1)
    old = lambda i: jnp.maximum(i - 1, 0)
    row = lambda w: pl.BlockSpec((ROW_TILE, w), lambda i: (cur(i), 0))
    prev_block = pl.BlockSpec(
        (BLOCK, 2 * KV_WIDTH), lambda i: (jnp.maximum(cur(i) * blocks_per_tile - 1, 0), 0))
    out_chunks = d // OUT_CHUNK
    return pl.pallas_call(
        functools.partial(_attn_out_kernel, n_tiles=n_tiles, tiles_per_batch=tiles_per_batch),
        out_shape=jax.ShapeDtypeStruct((t, d), F32),
        grid=(n_tiles + 1,),
        in_specs=[pl.BlockSpec(memory_space=pltpu.SMEM),
                  row(ATTN_WIDTH), row(ATTN_WIDTH), row(2 * KV_WIDTH), prev_block,
                  pl.BlockSpec((CHUNK_ROWS, y2.shape[1]), lambda i: (cur(i), 0)),
                  row(SSM_WIDTH),
                  pl.BlockSpec((ROW_TILE, d), lambda i: (old(i), 0)),
                  pl.BlockSpec((1, 1, d), lambda i: (old(i) // tiles_per_batch, 0, 0)),
                  _resident((1, ATTN_WIDTH)),
                  _resident((SSM_WIDTH, SSM_WIDTH)), _resident((1, SSM_WIDTH)),
                  _resident((1, SSM_WIDTH)),
                  _resident((d, d)), _resident((1, d))],
        out_specs=pl.BlockSpec((ROW_TILE, d), lambda i: (old(i), 0)),
        scratch_shapes=[pltpu.VMEM((ROW_TILE + BLOCK, 2 * KV_WIDTH), BF16),
                        pltpu.VMEM((BLOCK, ATTN_WIDTH), F32),
                        pltpu.VMEM((N_GROUP_BLOCKS, ROW_TILE, LANES), F32),
                        pltpu.VMEM((2, ROW_TILE, d), BF16),
                        pltpu.VMEM((out_chunks, ROW_TILE, OUT_CHUNK), F32)],
        compiler_params=pltpu.CompilerParams(dimension_semantics=("arbitrary",),
                                             vmem_limit_bytes=60 * 1024 * 1024),
        name="attn_out",
    )(sinks, q, za, kv, kv, y2, zs, x2, gate, attn_gain, glu_w, glu_b, ssm_gain, w_out, final_gain)


def kernel(x, c, w_ada, b_ada, norm_gain, w_in, b_in, attn_sinks, attn_out_gain, ssm_lambda_re,
           ssm_lambda_im, ssm_log_step, ssm_b_re, ssm_b_im, ssm_c_re, ssm_c_im, ssm_d, glu_w, glu_b,
           ssm_out_gain, w_out, final_gain):
    bsz, seq, d = x.shape
    assert w_ada.shape[0] == 1, "single-layer trunk only"
    assert d == D_MODEL and w_in.shape[2] == 2 * (ATTN_WIDTH + KV_WIDTH + SSM_WIDTH)
    assert seq % ROW_TILE == 0 and (bsz * seq // SSM_CHUNK) % SSM_TILE == 0
    assert (seq // SSM_CHUNK) % SCAN_TILE == 0
    x2 = x.reshape(bsz * seq, d)
    mod, w_in16 = _ada(c, w_ada[0], b_ada[0], w_in[0])
    shift, scale, gate = (m.reshape(bsz, 1, d) for m in jnp.split(mod, 3, axis=-1))
    q, kv, za, zs, u2, w_out16 = _inproj(x2, scale, shift, norm_gain[0].reshape(1, d),
                                         w_in16, b_in[0].reshape(1, -1), w_out[0], seq)
    prep = _ssm_prep(ssm_lambda_re[0], ssm_lambda_im[0], ssm_log_step[0], ssm_b_re[0],
                     ssm_b_im[0], ssm_c_re[0], ssm_c_im[0], ssm_d[0])
    y2 = _s5(u2, prep, bsz)
    out = _attn_out(attn_sinks[0], q, za, kv, y2, zs, x2, gate,
                    attn_out_gain[0].reshape(1, -1), glu_w[0].astype(BF16),
                    glu_b[0].reshape(1, -1), ssm_out_gain[0].reshape(1, -1),
                    w_out16, final_gain.reshape(1, d), seq)
    return out.reshape(bsz, seq, d)
```
